```python
import numpy as np
import jax
import jax.numpy as jnp
from jax import lax

D_MODEL = 4096
BATCH = 4
SEQ = 2048
DEPTH = 2
DEC_BATCH = 8
DEC_SEQ = 1
PAST_LEN = 16384
PAGE_SIZE = 128

HEAD_DIM = 128
N_MIXERS = 4
GROUP_WIDTH = D_MODEL // N_MIXERS
FOX_HEADS = GROUP_WIDTH // HEAD_DIM
FOX_FORGET_BIAS = 2.0
POOL_WINDOWS = (2, 4, 8, 16)
POOL_CH = GROUP_WIDTH
POOL_GROUP = POOL_CH // len(POOL_WINDOWS)
POOL_BUF = max(POOL_WINDOWS) - 1
MOBA_HEADS = GROUP_WIDTH // HEAD_DIM
MOBA_KV_HEADS = MOBA_HEADS // 2
MOBA_BLOCK = 256
MOBA_TOPK = 3
NSA_HEADS = GROUP_WIDTH // HEAD_DIM
NSA_KV_HEADS = 2
NSA_CMP_BLOCK = 32
NSA_CMP_STRIDE = 16
NSA_SEL_BLOCK = 64
NSA_SEL_COUNT = 16
NSA_WINDOW = 512
D_FF = 4 * D_MODEL
Q_BLOCK = 128
GATHER_Q_BLOCK = 16
RMS_EPS = 1e-6
IN_SIZES = (FOX_HEADS * HEAD_DIM, FOX_HEADS * HEAD_DIM, FOX_HEADS * HEAD_DIM, FOX_HEADS,
            POOL_CH,
            MOBA_HEADS * HEAD_DIM, MOBA_KV_HEADS * HEAD_DIM, MOBA_KV_HEADS * HEAD_DIM,
            NSA_HEADS * HEAD_DIM) + (NSA_KV_HEADS * HEAD_DIM,) * 6 + (3 * NSA_HEADS,)
PROJ_WIDTH = sum(IN_SIZES)
SPLIT_AT = tuple(sum(IN_SIZES[:i + 1]) for i in range(len(IN_SIZES) - 1))

kernel_name = 'hymba_style_fox_pool_moba_nsa_decode_step'


def _rms(x, g):
    xf = x.astype(jnp.float32)
    y = xf * lax.rsqrt(jnp.mean(xf * xf, axis=-1, keepdims=True) + RMS_EPS)
    return (y * g.astype(jnp.float32)).astype(x.dtype)


def _masked_softmax(s, mask, axis):
    s = jnp.where(mask, s.astype(jnp.float32), -jnp.inf)
    m = jnp.max(s, axis=axis, keepdims=True)
    m = jnp.where(jnp.isfinite(m), m, 0.0)
    e = jnp.where(mask, jnp.exp(s - m), 0.0)
    return e / jnp.maximum(jnp.sum(e, axis=axis, keepdims=True), 1e-30)


def _qblock(n, qb):
    return qb if n % qb == 0 else n


def _sweep(fn, qb, *xs):
    n = xs[0].shape[0]
    nb = n // qb
    blocks = tuple(a.reshape((nb, qb) + a.shape[1:]) for a in xs)
    out = lax.map(lambda args: fn(*args), blocks)
    return out.reshape((n,) + out.shape[2:])


def _last_rows(x, n):
    L = x.shape[1]
    if L < n:
        x = jnp.pad(x, [(0, 0), (n - L, 0)] + [(0, 0)] * (x.ndim - 2))
    return x[:, x.shape[1] - n:]


def _fox(q, c_q, pos_q, k_all, v_all, c_all):
    pos_k = jnp.arange(k_all.shape[1])
    c_k = jnp.transpose(c_all, (0, 2, 1))
    scale = HEAD_DIM ** -0.5

    def block(qb, cb, pb):
        s = jnp.einsum('qbhd,blhd->bhql', qb, k_all).astype(jnp.float32) * scale
        s = s + jnp.transpose(cb, (1, 2, 0))[..., None] - c_k[:, :, None, :]
        p = _masked_softmax(s, pb[:, None] >= pos_k[None, :], -1)
        return jnp.einsum('bhql,blhd->qbhd', p.astype(v_all.dtype), v_all)

    return _sweep(block, _qblock(q.shape[0], Q_BLOCK), q, c_q, pos_q)


def _pool(u_all, n_ctx, pos0, w_pool, scale):
    T = u_all.shape[1] - n_ctx
    uf = u_all.astype(jnp.float32)
    cs = jnp.concatenate([jnp.zeros_like(uf[:, :1]), jnp.cumsum(uf, axis=1)], axis=1)
    e = n_ctx + jnp.arange(T)
    pos = pos0 + jnp.arange(T)
    outs = []
    for gi, w in enumerate(POOL_WINDOWS):
        sl = slice(gi * POOL_GROUP, (gi + 1) * POOL_GROUP)
        win_sum = cs[:, e + 1, sl] - cs[:, jnp.maximum(e + 1 - w, 0), sl]
        cnt = jnp.minimum(w, pos + 1).astype(jnp.float32)
        d = win_sum / cnt[None, :, None] - uf[:, n_ctx:, sl]
        outs.append(jnp.einsum('btc,ce->bte', d.astype(u_all.dtype), w_pool[gi]))
    return jnp.concatenate(outs, axis=-1) * scale


def _moba(q, pos_q, k_all, v_all):
    B, L = k_all.shape[:2]
    rep = MOBA_HEADS // MOBA_KV_HEADS
    n_blk = -(-L // MOBA_BLOCK)
    pad = n_blk * MOBA_BLOCK - L

    def to_blocks(x):
        x = jnp.pad(x, ((0, 0), (0, pad), (0, 0), (0, 0)))
        return x.reshape(B, n_blk, MOBA_BLOCK, MOBA_KV_HEADS, HEAD_DIM).transpose(0, 3, 1, 2, 4)

    kb, vb = to_blocks(k_all), to_blocks(v_all)
    kv_of_head = jnp.arange(MOBA_HEADS) // rep
    k_mean = jnp.mean(kb.astype(jnp.float32), axis=3)[:, kv_of_head]
    blk_ids = jnp.arange(n_blk)
    n_top = min(MOBA_TOPK, n_blk)
    b_ix = jnp.arange(B)[:, None, None, None]
    h_ix = kv_of_head[None, :, None, None]
    scale = HEAD_DIM ** -0.5

    def block(qb, pb):
        nq = qb.shape[0]
        own = pb // MOBA_BLOCK
        gate = jnp.einsum('qbhd,bhnd->bhqn', qb.astype(jnp.float32), k_mean)
        gate = jnp.where(blk_ids[None, :] < own[:, None], gate, -jnp.inf)
        top_v, top_i = lax.top_k(gate, n_top)
        valid = jnp.isfinite(top_v)
        own_b = jnp.broadcast_to(own[None, None, :, None], (B, MOBA_HEADS, nq, 1))
        idx = jnp.concatenate([jnp.where(valid, top_i, 0), own_b], axis=-1)
        valid = jnp.concatenate([valid, jnp.ones_like(valid[..., :1])], axis=-1)
        kg = kb[b_ix, h_ix, idx]
        vg = vb[b_ix, h_ix, idx]
        s = jnp.einsum('qbhd,bhqnkd->bhqnk', qb, kg).astype(jnp.float32) * scale
        kpos = idx[..., None] * MOBA_BLOCK + jnp.arange(MOBA_BLOCK)
        mask = valid[..., None] & (kpos <= pb[None, None, :, None, None])
        p = _masked_softmax(s, mask, (-2, -1))
        return jnp.einsum('bhqnk,bhqnkd->qbhd', p.astype(vg.dtype), vg)

    return _sweep(block, _qblock(q.shape[0], GATHER_Q_BLOCK), q, pos_q)


def _nsa_cmp_sel(q, pos_q, kc_all, vc_all, ks_all, vs_all, kc_gain, cmp_pe, cmp_w1, cmp_w2):
    B, L = kc_all.shape[:2]
    G, R = NSA_KV_HEADS, NSA_HEADS // NSA_KV_HEADS
    scale = HEAD_DIM ** -0.5
    n_c = (L - NSA_CMP_BLOCK) // NSA_CMP_STRIDE + 1
    tok = np.arange(n_c)[:, None] * NSA_CMP_STRIDE + np.arange(NSA_CMP_BLOCK)[None, :]

    def compress(x, j):
        blk = x[:, tok] + cmp_pe[j][None, None, :, None, :]
        hid = jax.nn.silu(jnp.einsum('bnrgd,rde->bnge', blk, cmp_w1[j]))
        return jnp.einsum('bnge,ef->bgnf', hid, cmp_w2[j])

    k_cmp = _rms(compress(kc_all, 0), kc_gain)
    v_cmp = compress(vc_all, 1)
    cmp_end = jnp.asarray(np.arange(n_c) * NSA_CMP_STRIDE + NSA_CMP_BLOCK - 1)
    n_sel = -(-L // NSA_SEL_BLOCK)
    pad = n_sel * NSA_SEL_BLOCK - L

    def to_blocks(x):
        x = jnp.pad(x, ((0, 0), (0, pad), (0, 0), (0, 0)))
        return x.reshape(B, n_sel, NSA_SEL_BLOCK, G, HEAD_DIM).transpose(0, 3, 1, 2, 4)

    ksb, vsb = to_blocks(ks_all), to_blocks(vs_all)
    jb = np.arange(n_sel)
    lo = np.clip((NSA_SEL_BLOCK * jb - NSA_CMP_BLOCK) // NSA_CMP_STRIDE + 1, 0, n_c)
    hi = np.clip(-(-(NSA_SEL_BLOCK * (jb + 1)) // NSA_CMP_STRIDE), 0, n_c)
    blk_ids = jnp.asarray(jb)
    n_top = min(NSA_SEL_COUNT - 1, n_sel)
    b_ix = jnp.arange(B)[:, None, None, None]
    g_ix = jnp.arange(G)[None, :, None, None]

    def block(qb, pb):
        nq = qb.shape[0]
        qg = qb.reshape(nq, B, G, R, HEAD_DIM)
        s = jnp.einsum('qbgrd,bgnd->bgrqn', qg, k_cmp).astype(jnp.float32) * scale
        p = _masked_softmax(s, cmp_end[None, :] <= pb[:, None], -1)
        o_cmp = jnp.einsum('bgrqn,bgnd->qbgrd', p.astype(v_cmp.dtype), v_cmp)
        cs = jnp.cumsum(p.sum(axis=2), axis=-1)
        cs = jnp.concatenate([jnp.zeros_like(cs[..., :1]), cs], axis=-1)
        imp = cs[..., hi] - cs[..., lo]
        own = pb // NSA_SEL_BLOCK
        imp = jnp.where(blk_ids[None, :] < own[:, None], imp, -jnp.inf)
        top_v, top_i = lax.top_k(imp, n_top)
        valid = jnp.isfinite(top_v)
        own_b = jnp.broadcast_to(own[None, None, :, None], (B, G, nq, 1))
        idx = jnp.concatenate([jnp.where(valid, top_i, 0), own_b], axis=-1)
        valid = jnp.concatenate([valid, jnp.ones_like(valid[..., :1])], axis=-1)
        kg = ksb[b_ix, g_ix, idx]
        vg = vsb[b_ix, g_ix, idx]
        s2 = jnp.einsum('qbgrd,bgqnkd->bgrqnk', qg, kg).astype(jnp.float32) * scale
        kpos = idx[..., None] * NSA_SEL_BLOCK + jnp.arange(NSA_SEL_BLOCK)
        mask = valid[..., None] & (kpos <= pb[None, None, :, None, None])
        p2 = _masked_softmax(s2, mask[:, :, None], (-2, -1))
        o_sel = jnp.einsum('bgrqnk,bgqnkd->qbgrd', p2.astype(vg.dtype), vg)
        return jnp.stack([o_cmp, o_sel], axis=1)

    return _sweep(block, _qblock(q.shape[0], GATHER_Q_BLOCK), q, pos_q)


def _nsa_win(q, pos_q, k_ext, v_ext, n_ctx, pos0):
    T = q.shape[0]
    B = k_ext.shape[0]
    G, R = NSA_KV_HEADS, NSA_HEADS // NSA_KV_HEADS
    padw = ((0, 0), (NSA_WINDOW, 0), (0, 0), (0, 0))
    k_pad, v_pad = jnp.pad(k_ext, padw), jnp.pad(v_ext, padw)
    qb_size = _qblock(T, Q_BLOCK)
    span = qb_size + NSA_WINDOW
    scale = HEAD_DIM ** -0.5

    def block(qb, pb):
        nq = qb.shape[0]
        start = pb[0] - pos0 + n_ctx
        kw = lax.dynamic_slice_in_dim(k_pad, start, span, axis=1)
        vw = lax.dynamic_slice_in_dim(v_pad, start, span, axis=1)
        kpos = pos0 - n_ctx - NSA_WINDOW + start + jnp.arange(span)
        qg = qb.reshape(nq, B, G, R, HEAD_DIM)
        s = jnp.einsum('qbgrd,bkgd->bgrqk', qg, kw).astype(jnp.float32) * scale
        diff = pb[:, None] - kpos[None, :]
        mask = (diff >= 0) & (diff < NSA_WINDOW) & (kpos[None, :] >= 0)
        p = _masked_softmax(s, mask, -1)
        return jnp.einsum('bgrqk,bkgd->qbgrd', p.astype(vw.dtype), vw)

    return _sweep(block, qb_size, q, pos_q)


def _layer(x, pos0, past, lw, win_buf):
    fox_kv_p, fox_lf_p, moba_kv_p, nsa_kv_p, win_p, pool_p = past
    B, T, _ = x.shape
    dt = x.dtype
    h = _rms(x, lw['norm1_g'])
    proj = jnp.einsum('btd,dp->btp', h, lw['w_in'])
    (fq, fk, fv, ff, pu, mq, mk, mv, nq, nkc, nvc, nks, nvs, nkw, nvw, ng) = jnp.split(proj, SPLIT_AT, axis=-1)

    def heads(a, n):
        return a.reshape(B, T, n, HEAD_DIM)

    def swap01(a):
        return jnp.moveaxis(a, 1, 0)

    def cat(old, new):
        return jnp.concatenate([old, new], axis=1)

    pos_q = pos0 + jnp.arange(T)

    fq = _rms(heads(fq, FOX_HEADS), lw['fox_qk_g'][0])
    fk = _rms(heads(fk, FOX_HEADS), lw['fox_qk_g'][1])
    fv = heads(fv, FOX_HEADS)
    f_lf = jax.nn.log_sigmoid((ff + lw['fox_f_bias']).astype(jnp.float32))
    c_all = jnp.cumsum(cat(fox_lf_p.astype(jnp.float32), f_lf), axis=1)
    n_past = fox_lf_p.shape[1]
    o_fox = _fox(swap01(fq), swap01(c_all[:, n_past:]), pos_q,
                 cat(fox_kv_p[:, :, 0], fk), cat(fox_kv_p[:, :, 1], fv), c_all)
    o_fox = swap01(o_fox).reshape(B, T, GROUP_WIDTH)

    pu_all = cat(pool_p, pu)
    o_pool = _pool(pu_all, pool_p.shape[1], pos0, lw['pool_w'], lw['pool_scale'])

    mq = _rms(heads(mq, MOBA_HEADS), lw['moba_qk_g'][0])
    mk = _rms(heads(mk, MOBA_KV_HEADS), lw['moba_qk_g'][1])
    mv = heads(mv, MOBA_KV_HEADS)
    o_moba = _moba(swap01(mq), pos_q, cat(moba_kv_p[:, :, 0], mk), cat(moba_kv_p[:, :, 1], mv))
    o_moba = swap01(o_moba).reshape(B, T, GROUP_WIDTH)

    g_n = lw['nsa_qk_g']
    nq = _rms(heads(nq, NSA_HEADS), g_n[0])
    nkc, nvc = heads(nkc, NSA_KV_HEADS), heads(nvc, NSA_KV_HEADS)
    nks, nvs = _rms(heads(nks, NSA_KV_HEADS), g_n[2]), heads(nvs, NSA_KV_HEADS)
    nkw, nvw = _rms(heads(nkw, NSA_KV_HEADS), g_n[3]), heads(nvw, NSA_KV_HEADS)
    nq_t = swap01(nq)
    o_cs = _nsa_cmp_sel(nq_t, pos_q, cat(nsa_kv_p[:, :, 0], nkc), cat(nsa_kv_p[:, :, 1], nvc),
                        cat(nsa_kv_p[:, :, 2], nks), cat(nsa_kv_p[:, :, 3], nvs),
                        g_n[1], lw['cmp_pe'], lw['cmp_w1'], lw['cmp_w2'])
    kw_all, vw_all = cat(win_p[:, :, 0], nkw), cat(win_p[:, :, 1], nvw)
    o_win = _nsa_win(nq_t, pos_q, kw_all, vw_all, win_p.shape[1], pos0)

    def bthd(o):
        return jnp.moveaxis(o, 0, 1).reshape(B, T, NSA_HEADS, HEAD_DIM)

    gates = jax.nn.sigmoid(ng.astype(jnp.float32)).astype(dt).reshape(B, T, 3, NSA_HEADS, 1)
    o_nsa = (gates[:, :, 0] * bthd(o_cs[:, 0]) + gates[:, :, 1] * bthd(o_cs[:, 1])
             + gates[:, :, 2] * bthd(o_win)).reshape(B, T, GROUP_WIDTH)

    mixed = jnp.concatenate([o_fox, o_pool, o_moba, o_nsa], axis=-1)
    x = x + jnp.einsum('btc,cd->btd', mixed, lw['w_out'])
    h2 = _rms(x, lw['norm2_g'])
    u = jax.nn.relu(jnp.einsum('btd,df->btf', h2, lw['w_up']))
    x = x + jnp.einsum('btf,fd->btd', u * u, lw['w_down'])

    new = (jnp.stack([fk, fv], axis=2),
           f_lf.astype(dt),
           jnp.stack([mk, mv], axis=2),
           jnp.stack([nkc, nvc, nks, nvs], axis=2),
           jnp.stack([_last_rows(kw_all, win_buf), _last_rows(vw_all, win_buf)], axis=2),
           _last_rows(pu_all, POOL_BUF))
    return x, new


def setup_inputs(seed: int = 0) -> dict:
    key = jax.random.key(seed)
    ks = jax.random.split(key, 24)
    n_pages = PAST_LEN // PAGE_SIZE
    n_pool = (5 * DEC_BATCH * n_pages + 3) // 4
    win_buf = min(NSA_WINDOW, PAST_LEN)

    def nrm(k, shape, s=1.0):
        return jax.random.normal(k, shape, jnp.float32) * s

    page_table = jax.random.permutation(ks[0], n_pool)[:DEC_BATCH * n_pages]
    page_table = page_table.reshape(DEC_BATCH, n_pages).astype(jnp.int32)
    return {
        'x_prompt': nrm(ks[1], (BATCH, SEQ, D_MODEL)),
        'x_sample': nrm(ks[2], (DEC_BATCH, DEC_SEQ, D_MODEL)),
        'cache_fox_kv': nrm(ks[3], (DEPTH, n_pool, PAGE_SIZE, 2, FOX_HEADS, HEAD_DIM)),
        'cache_fox_logf': jax.nn.log_sigmoid(FOX_FORGET_BIAS + nrm(ks[4], (DEPTH, n_pool, PAGE_SIZE, FOX_HEADS))),
        'cache_moba_kv': nrm(ks[5], (DEPTH, n_pool, PAGE_SIZE, 2, MOBA_KV_HEADS, HEAD_DIM)),
        'cache_nsa_kv': nrm(ks[6], (DEPTH, n_pool, PAGE_SIZE, 4, NSA_KV_HEADS, HEAD_DIM)),
        'state_nsa_win': nrm(ks[7], (DEPTH, DEC_BATCH, win_buf, 2, NSA_KV_HEADS, HEAD_DIM)),
        'state_pool': nrm(ks[8], (DEPTH, DEC_BATCH, POOL_BUF, POOL_CH)),
        'page_table': page_table,
        'norm1_g': 1.0 + nrm(ks[9], (DEPTH, D_MODEL), 0.1),
        'w_in': nrm(ks[10], (DEPTH, D_MODEL, PROJ_WIDTH), D_MODEL ** -0.5),
        'fox_f_bias': FOX_FORGET_BIAS + nrm(ks[11], (DEPTH, FOX_HEADS), 0.1),
        'fox_qk_g': 1.0 + nrm(ks[12], (DEPTH, 2, HEAD_DIM), 0.1),
        'moba_qk_g': 1.0 + nrm(ks[13], (DEPTH, 2, HEAD_DIM), 0.1),
        'nsa_qk_g': 1.0 + nrm(ks[14], (DEPTH, 4, HEAD_DIM), 0.1),
        'pool_w': nrm(ks[15], (DEPTH, len(POOL_WINDOWS), POOL_GROUP, POOL_GROUP), POOL_GROUP ** -0.5),
        'pool_scale': 1.0 + nrm(ks[16], (DEPTH, POOL_CH), 0.1),
        'cmp_pe': nrm(ks[17], (DEPTH, 2, NSA_CMP_BLOCK, HEAD_DIM), 0.1),
        'cmp_w1': nrm(ks[18], (DEPTH, 2, NSA_CMP_BLOCK, HEAD_DIM, HEAD_DIM), (NSA_CMP_BLOCK * HEAD_DIM) ** -0.5),
        'cmp_w2': nrm(ks[19], (DEPTH, 2, HEAD_DIM, HEAD_DIM), HEAD_DIM ** -0.5),
        'w_out': nrm(ks[20], (DEPTH, D_MODEL, D_MODEL), D_MODEL ** -0.5),
        'norm2_g': 1.0 + nrm(ks[21], (DEPTH, D_MODEL), 0.1),
        'w_up': nrm(ks[22], (DEPTH, D_MODEL, D_FF), D_MODEL ** -0.5),
        'w_down': nrm(ks[23], (DEPTH, D_FF, D_MODEL), D_FF ** -0.5),
    }


def reference(x_prompt, x_sample, cache_fox_kv, cache_fox_logf, cache_moba_kv, cache_nsa_kv,
              state_nsa_win, state_pool, page_table,
              norm1_g, w_in, fox_f_bias, fox_qk_g, moba_qk_g, nsa_qk_g, pool_w, pool_scale,
              cmp_pe, cmp_w1, cmp_w2, w_out, norm2_g, w_up, w_down):
    past_len = page_table.shape[1] * cache_fox_kv.shape[2]
    win_buf = state_nsa_win.shape[2]
    b_p = x_prompt.shape[0]
    dt = x_prompt.dtype

    def empty(tail):
        return jnp.zeros((b_p, 0) + tail, dt)

    prompt_past = (empty((2, FOX_HEADS, HEAD_DIM)), empty((FOX_HEADS,)),
                   empty((2, MOBA_KV_HEADS, HEAD_DIM)), empty((4, NSA_KV_HEADS, HEAD_DIM)),
                   empty((2, NSA_KV_HEADS, HEAD_DIM)), empty((POOL_CH,)))

    def gather(pool, l):
        g = pool[l, page_table]
        return g.reshape((g.shape[0], g.shape[1] * g.shape[2]) + g.shape[3:])

    y_prompt, y_sample = x_prompt, x_sample
    new_p, new_s = [], []
    for l in range(DEPTH):
        lw = {'norm1_g': norm1_g[l], 'w_in': w_in[l], 'fox_f_bias': fox_f_bias[l],
              'fox_qk_g': fox_qk_g[l], 'moba_qk_g': moba_qk_g[l], 'nsa_qk_g': nsa_qk_g[l],
              'pool_w': pool_w[l], 'pool_scale': pool_scale[l], 'cmp_pe': cmp_pe[l],
              'cmp_w1': cmp_w1[l], 'cmp_w2': cmp_w2[l], 'w_out': w_out[l],
              'norm2_g': norm2_g[l], 'w_up': w_up[l], 'w_down': w_down[l]}
        y_prompt, rows_p = _layer(y_prompt, 0, prompt_past, lw, win_buf)
        sample_past = (gather(cache_fox_kv, l), gather(cache_fox_logf, l), gather(cache_moba_kv, l),
                       gather(cache_nsa_kv, l), state_nsa_win[l], state_pool[l])
        y_sample, rows_s = _layer(y_sample, past_len, sample_past, lw, win_buf)
        new_p.append(rows_p)
        new_s.append(rows_s)

    def stk(rows, i):
        return jnp.stack([r[i] for r in rows], axis=0)

    fox_kv_prompt, fox_kv_sample = stk(new_p, 0), stk(new_s, 0)
    fox_logf_prompt, fox_logf_sample = stk(new_p, 1), stk(new_s, 1)
    moba_kv_prompt, moba_kv_sample = stk(new_p, 2), stk(new_s, 2)
    nsa_kv_prompt, nsa_kv_sample = stk(new_p, 3), stk(new_s, 3)
    nsa_win_prompt, nsa_win_sample = stk(new_p, 4), stk(new_s, 4)
    pool_prompt, pool_sample = stk(new_p, 5), stk(new_s, 5)
    return (y_prompt, y_sample, fox_kv_prompt, fox_kv_sample, fox_logf_prompt, fox_logf_sample,
            moba_kv_prompt, moba_kv_sample, nsa_kv_prompt, nsa_kv_sample,
            nsa_win_prompt, nsa_win_sample, pool_prompt, pool_sample)
```

```python
import functools

import jax
import jax.numpy as jnp
from jax import lax
from jax.experimental import pallas as pl
from jax.experimental.pallas import tpu as pltpu

F32 = jnp.float32
BF16 = jnp.bfloat16

HEAD_DIM = 128
N_MIXERS = 4
FOX_FORGET_BIAS = 2.0
POOL_WINDOWS = (2, 4, 8, 16)
POOL_BUF = max(POOL_WINDOWS) - 1
MOBA_BLOCK = 256
MOBA_TOPK = 3
NSA_KV_HEADS = 2
NSA_CMP_BLOCK = 32
NSA_CMP_STRIDE = 16
NSA_SEL_BLOCK = 64
NSA_SEL_COUNT = 16
NSA_WINDOW = 512
RMS_EPS = 1e-6
SCALE = HEAD_DIM ** -0.5

LANES = 128
VMEM_LIMIT = 56 * 1024 * 1024
NEG = -1e30

GW = 1024
_SEG = dict(fq=0, fk=1024, fv=2048, pu=3072, mq=4096, mk=5120, mv=5632, nq=6144,
            nkc=7168, nvc=7424, nks=7680, nvs=7936, nkw=8192, nvw=8448, small=8704)
PW = 9216
_ORIG = (("fq", 1024), ("fk", 1024), ("fv", 1024), ("ff", 8), ("pu", 1024), ("mq", 1024), ("mk", 512),
         ("mv", 512), ("nq", 1024), ("nkc", 256), ("nvc", 256), ("nks", 256), ("nvs", 256), ("nkw", 256),
         ("nvw", 256), ("ng", 24))
_NEW_ORDER = ("fq", "fk", "fv", "pu", "mq", "mk", "mv", "nq", "nkc", "nvc", "nks", "nvs", "nkw", "nvw", "ff", "ng")
_SLOT = dict(fq=0, fk=8, fv=16, mq=24, mk=32, mv=36, nq=40, nks=48, nvs=50, nkw=52, nvw=54)
NSLOT = 56
GATE_LANE0 = 8


def _cparams(sem):
    return pltpu.CompilerParams(dimension_semantics=sem, vmem_limit_bytes=VMEM_LIMIT)


def _iota(shape, dim):
    return lax.broadcasted_iota(jnp.int32, shape, dim)


def _dot(a, b):
    return jnp.dot(a, b, preferred_element_type=F32)


def _dot_nt(a, b):
    return lax.dot_general(a, b, (((1,), (1,)), ((), ())), preferred_element_type=F32)


def _split3(x):
    hi = x.astype(BF16)
    r1 = x - hi.astype(F32)
    mid = r1.astype(BF16)
    lo = (r1 - mid.astype(F32)).astype(BF16)
    return hi, mid, lo


def _dot3(x, m):
    hi, mid, lo = _split3(x)
    return _dot(hi, m) + _dot(mid, m) + _dot(lo, m)


def _softmax_unnorm(s, mask):
    sm = jnp.where(mask, s, NEG)
    m = jnp.max(sm, axis=-1, keepdims=True)
    e = jnp.where(mask, jnp.exp(sm - m), 0.0)
    l = jnp.maximum(jnp.sum(e, axis=-1, keepdims=True), 1e-30)
    return e, l


def _rms_kernel(x_ref, g_ref, o_ref):
    x = x_ref[...]
    y = x * lax.rsqrt(jnp.mean(x * x, axis=-1, keepdims=True) + RMS_EPS)
    o_ref[...] = (y * g_ref[...]).astype(o_ref.dtype)


def _rms(x, g, tm):
    M, D = x.shape
    return pl.pallas_call(
        _rms_kernel,
        grid=(M // tm,),
        in_specs=[pl.BlockSpec((tm, D), lambda i: (i, 0)), pl.BlockSpec((1, D), lambda i: (0, 0))],
        out_specs=pl.BlockSpec((tm, D), lambda i: (i, 0)),
        out_shape=jax.ShapeDtypeStruct((M, D), BF16),
        compiler_params=_cparams(("parallel",)),
        name="rms",
    )(x, g.reshape(1, D))


def _mm_kernel(a_ref, w_ref, o_ref, *, act):
    acc = _dot(a_ref[...], w_ref[...])
    if act == "relu2":
        acc = jnp.maximum(acc, 0.0)
        acc = acc * acc
    o_ref[...] = acc.astype(o_ref.dtype)


def _matmul(a, w, *, tm, tn, act=None, out_dtype=F32, name="mm"):
    M, K = a.shape
    N = w.shape[1]
    return pl.pallas_call(
        functools.partial(_mm_kernel, act=act),
        grid=(M // tm, N // tn),
        in_specs=[pl.BlockSpec((tm, K), lambda i, j: (i, 0)), pl.BlockSpec((K, tn), lambda i, j: (0, j))],
        out_specs=pl.BlockSpec((tm, tn), lambda i, j: (i, j)),
        out_shape=jax.ShapeDtypeStruct((M, N), out_dtype),
        compiler_params=_cparams(("parallel", "parallel")),
        name=name,
    )(a, w)


def _mm_out_kernel(x_ref, a0, a1, a2, a3, w_ref, o_ref):
    acc = x_ref[...]
    for i, a in enumerate((a0, a1, a2, a3)):
        acc = acc + _dot(a[...], w_ref[i * GW:(i + 1) * GW, :])
    o_ref[...] = acc


def _matmul_out(x, parts, w, *, tm, tn):
    M, D = x.shape
    a_spec = pl.BlockSpec((tm, GW), lambda i, j: (i, 0))
    return pl.pallas_call(
        _mm_out_kernel,
        grid=(M // tm, D // tn),
        in_specs=[pl.BlockSpec((tm, tn), lambda i, j: (i, j)), a_spec, a_spec, a_spec, a_spec,
                  pl.BlockSpec((N_MIXERS * GW, tn), lambda i, j: (0, j))],
        out_specs=pl.BlockSpec((tm, tn), lambda i, j: (i, j)),
        out_shape=jax.ShapeDtypeStruct((M, D), F32),
        compiler_params=_cparams(("parallel", "parallel")),
        name="mm_out",
    )(x, *parts, w)


def _mm_down_kernel(x_ref, a_ref, w_ref, o_ref):
    @pl.when(pl.program_id(2) == 0)
    def _():
        o_ref[...] = x_ref[...]

    o_ref[...] += _dot(a_ref[...], w_ref[...])


def _matmul_down(x, a, w, *, tm, tn, tk):
    M, K = a.shape
    N = w.shape[1]
    return pl.pallas_call(
        _mm_down_kernel,
        grid=(M // tm, N // tn, K // tk),
        in_specs=[pl.BlockSpec((tm, tn), lambda i, j, k: (i, j)), pl.BlockSpec((tm, tk), lambda i, j, k: (i, k)),
                  pl.BlockSpec((tk, tn), lambda i, j, k: (k, j))],
        out_specs=pl.BlockSpec((tm, tn), lambda i, j, k: (i, j)),
        out_shape=jax.ShapeDtypeStruct((M, N), F32),
        compiler_params=_cparams(("parallel", "parallel", "arbitrary")),
        name="mm_down",
    )(x, a, w)


def _log_sigmoid(x):
    return jnp.minimum(x, 0.0) - jnp.log1p(jnp.exp(-jnp.abs(x)))


def _head_rms(x, g):
    return x * lax.rsqrt(jnp.mean(x * x, axis=-1, keepdims=True) + RMS_EPS) * g


def _prep_kernel(p_ref, gf_ref, gm_ref, gn_ref, fb_ref, *refs, tm, prompt, tiles_per_seq):
    if prompt:
        (fox_ref, moba_ref, nsa_ref, win_ref, sm_ref, hm_ref, cmpin_ref, cc_ref, crow_ref, km_ref, carry_ref) = refs
    else:
        (fox_ref, moba_ref, nsa_ref, win_ref, sm_ref, hm_ref) = refs

    def seg(name, h):
        c0 = _SEG[name] + h * HEAD_DIM
        return p_ref[:, c0:c0 + HEAD_DIM]

    gfq, gfk = gf_ref[0:1, :], gf_ref[1:2, :]
    gmq, gmk = gm_ref[0:1, :], gm_ref[1:2, :]
    gnq, gnks, gnkw = gn_ref[0:1, :], gn_ref[2:3, :], gn_ref[3:4, :]

    for h in range(8):
        hm_ref[_SLOT["fq"] + h] = (_head_rms(seg("fq", h), gfq) * SCALE).astype(BF16)
        k = _head_rms(seg("fk", h), gfk)
        fox_ref[:, h * 128:(h + 1) * 128] = k
        hm_ref[_SLOT["fk"] + h] = k.astype(BF16)
        v = seg("fv", h)
        fox_ref[:, GW + h * 128:GW + (h + 1) * 128] = v
        hm_ref[_SLOT["fv"] + h] = v.astype(BF16)
        hm_ref[_SLOT["mq"] + h] = (_head_rms(seg("mq", h), gmq) * SCALE).astype(BF16)
        hm_ref[_SLOT["nq"] + h] = (_head_rms(seg("nq", h), gnq) * SCALE).astype(BF16)
    for h in range(4):
        k = _head_rms(seg("mk", h), gmk)
        moba_ref[:, h * 128:(h + 1) * 128] = k
        hm_ref[_SLOT["mk"] + h] = k.astype(BF16)
        if prompt:
            km_ref[h:h + 1, :] = jnp.mean(k, axis=0, keepdims=True)
        v = seg("mv", h)
        moba_ref[:, 512 + h * 128:512 + (h + 1) * 128] = v
        hm_ref[_SLOT["mv"] + h] = v.astype(BF16)
    for g in range(2):
        kc, vc = seg("nkc", g), seg("nvc", g)
        nsa_ref[:, g * 128:(g + 1) * 128] = kc
        nsa_ref[:, 256 + g * 128:256 + (g + 1) * 128] = vc
        if prompt:
            cmpin_ref[g] = kc
            cmpin_ref[2 + g] = vc
        ks = _head_rms(seg("nks", g), gnks)
        nsa_ref[:, 512 + g * 128:512 + (g + 1) * 128] = ks
        hm_ref[_SLOT["nks"] + g] = ks.astype(BF16)
        vs = seg("nvs", g)
        nsa_ref[:, 768 + g * 128:768 + (g + 1) * 128] = vs
        hm_ref[_SLOT["nvs"] + g] = vs.astype(BF16)
        kw = _head_rms(seg("nkw", g), gnkw)
        win_ref[:, g * 128:(g + 1) * 128] = kw
        hm_ref[_SLOT["nkw"] + g] = kw.astype(BF16)
        vw = seg("nvw", g)
        win_ref[:, 256 + g * 128:256 + (g + 1) * 128] = vw
        hm_ref[_SLOT["nvw"] + g] = vw.astype(BF16)

    sblk = p_ref[:, _SEG["small"]:_SEG["small"] + LANES]
    lane = _iota(sblk.shape, 1)
    logf = _log_sigmoid(sblk + fb_ref[...])
    gates = 1.0 / (1.0 + jnp.exp(-sblk))
    logf = jnp.where(lane < 8, logf, 0.0)
    sm_ref[...] = jnp.where(lane < 8, logf, jnp.where(lane < 32, gates, 0.0))

    if prompt:
        t = pl.program_id(0) % tiles_per_seq

        @pl.when(t == 0)
        def _():
            carry_ref[...] = jnp.zeros_like(carry_ref)

        tri = (_iota((tm, tm), 1) <= _iota((tm, tm), 0)).astype(BF16)
        c = _dot3_left(tri, logf) + carry_ref[...]
        cc_ref[...] = c
        carry_ref[...] = c[tm - 1:tm, :]
        crow_ref[...] = c.T[0:8, :]


def _dot3_left(m, x):
    hi, mid, lo = _split3(x)
    return _dot(m, hi) + _dot(m, mid) + _dot(m, lo)


def _prep(proj, gf, gm, gn, fbias, *, B, T, prompt):
    M = B * T
    tm = 256 if prompt else M
    nt = M // tm
    row = lambda w: pl.BlockSpec((tm, w), lambda i: (i, 0))
    full = lambda a: pl.BlockSpec(a.shape, lambda i: (0,) * a.ndim)
    fb = jnp.zeros((1, LANES), F32).at[0, :8].set(fbias)
    tiles_per_seq = T // tm if prompt else 1
    out_shape = [jax.ShapeDtypeStruct((M, 2 * GW), F32), jax.ShapeDtypeStruct((M, GW), F32),
                 jax.ShapeDtypeStruct((M, GW), F32), jax.ShapeDtypeStruct((M, 512), F32),
                 jax.ShapeDtypeStruct((M, LANES), F32), jax.ShapeDtypeStruct((B, NSLOT, T, HEAD_DIM), BF16)]
    hm_map = (lambda i: (i // tiles_per_seq, 0, i % tiles_per_seq, 0))
    out_specs = [row(2 * GW), row(GW), row(GW), row(512), row(LANES),
                 pl.BlockSpec((None, NSLOT, tm, HEAD_DIM), hm_map)]
    scratch = []
    if prompt:
        out_shape += [jax.ShapeDtypeStruct((B, 4, T, HEAD_DIM), F32), jax.ShapeDtypeStruct((M, LANES), F32),
                      jax.ShapeDtypeStruct((B, 8, T), F32),
                      jax.ShapeDtypeStruct((B, T // MOBA_BLOCK, 4, HEAD_DIM), F32)]
        out_specs += [pl.BlockSpec((None, 4, tm, HEAD_DIM), hm_map), row(LANES),
                      pl.BlockSpec((None, 8, tm), lambda i: (i // tiles_per_seq, 0, i % tiles_per_seq)),
                      pl.BlockSpec((None, None, 4, HEAD_DIM), lambda i: (i // tiles_per_seq, i % tiles_per_seq, 0, 0))]
        scratch = [pltpu.VMEM((1, LANES), F32)]
    return pl.pallas_call(
        functools.partial(_prep_kernel, tm=tm, prompt=prompt, tiles_per_seq=tiles_per_seq),
        grid=(nt,),
        in_specs=[row(PW), full(gf), full(gm), full(gn), full(fb)],
        out_specs=out_specs,
        out_shape=out_shape,
        scratch_shapes=scratch,
        compiler_params=_cparams(("arbitrary",)),
        name="prep_prompt" if prompt else "prep_sample",
    )(proj, gf, gm, gn, fb)


def _lane_pick(x, lane_idx):
    return jnp.sum(jnp.where(_iota(x.shape, 1) == lane_idx, x, 0.0), axis=1, keepdims=True)


def _fox_kernel(q_ref, k_ref, v_ref, cc_ref, crow_ref, o_ref, *, tq):
    h, qi = pl.program_id(1), pl.program_id(2)
    T = k_ref.shape[0]
    s = _dot_nt(q_ref[...], k_ref[...])
    cq = _lane_pick(cc_ref[...], h)
    ck = crow_ref[pl.ds(h, 1), :]
    s = s + cq - ck
    pos = qi * tq + _iota((tq, 1), 0)
    mask = _iota((1, T), 1) <= pos
    e, l = _softmax_unnorm(s, mask)
    o_ref[...] = (_dot(e.astype(BF16), v_ref[...]) / l).astype(o_ref.dtype)


def _fox_prompt(hm, cc, crow, *, B, T):
    tq = 256
    nq = T // tq
    slot = lambda s0: pl.BlockSpec((None, None, T, HEAD_DIM), lambda b, h, q: (b, s0 + h, 0, 0))
    return pl.pallas_call(
        functools.partial(_fox_kernel, tq=tq),
        grid=(B, 8, nq),
        in_specs=[pl.BlockSpec((None, None, tq, HEAD_DIM), lambda b, h, q: (b, _SLOT["fq"] + h, q, 0)),
                  slot(_SLOT["fk"]), slot(_SLOT["fv"]),
                  pl.BlockSpec((tq, LANES), lambda b, h, q: (b * nq + q, 0)),
                  pl.BlockSpec((None, 8, T), lambda b, h, q: (b, 0, 0))],
        out_specs=pl.BlockSpec((tq, HEAD_DIM), lambda b, h, q: (b * nq + q, h)),
        out_shape=jax.ShapeDtypeStruct((B * T, GW), BF16),
        compiler_params=_cparams(("parallel", "parallel", "arbitrary")),
        name="fox_prompt",
    )(hm, hm, hm, cc, crow)


def _rank_select(score, n_cand, own, n_keep):
    lane = _iota(score.shape, 1)
    rank = jnp.zeros(score.shape, jnp.int32)
    for j in range(n_cand):
        sj = score[:, j:j + 1]
        beats = (sj > score) | ((sj == score) & (j < lane))
        rank = rank + jnp.where(beats & (j < own), 1, 0)
    return ((lane < own) & (rank < n_keep)) | (lane == own)


def _moba_kernel(q_ref, k_ref, v_ref, km_ref, o_ref, *, tq):
    qi = pl.program_id(2)
    T = k_ref.shape[0]
    n_blk = T // MOBA_BLOCK
    q = q_ref[...]
    gate = _dot_nt(q, km_ref[...].astype(BF16))
    pos = qi * tq + _iota((tq, 1), 0)
    own = pos // MOBA_BLOCK
    sel = _rank_select(gate, n_blk, own, MOBA_TOPK)
    self32 = jnp.where(sel, 1.0, 0.0)
    selk = jnp.concatenate([jnp.broadcast_to(self32[:, j:j + 1], (tq, MOBA_BLOCK)) for j in range(n_blk)], axis=1)
    mask = (selk > 0.5) & (_iota((1, T), 1) <= pos)
    s = _dot_nt(q, k_ref[...])
    e, l = _softmax_unnorm(s, mask)
    o_ref[...] = (_dot(e.astype(BF16), v_ref[...]) / l).astype(o_ref.dtype)


def _moba_prompt(hm, kmean, *, B, T):
    tq = MOBA_BLOCK
    nq = T // tq
    n_blk = T // MOBA_BLOCK
    slot = lambda s0: pl.BlockSpec((None, None, T, HEAD_DIM), lambda b, h, q: (b, s0 + h // 2, 0, 0))
    return pl.pallas_call(
        functools.partial(_moba_kernel, tq=tq),
        grid=(B, 8, nq),
        in_specs=[pl.BlockSpec((None, None, tq, HEAD_DIM), lambda b, h, q: (b, _SLOT["mq"] + h, q, 0)),
                  slot(_SLOT["mk"]), slot(_SLOT["mv"]),
                  pl.BlockSpec((None, None, n_blk, HEAD_DIM), lambda b, h, q: (b, h // 2, 0, 0))],
        out_specs=pl.BlockSpec((tq, HEAD_DIM), lambda b, h, q: (b * nq + q, h)),
        out_shape=jax.ShapeDtypeStruct((B * T, GW), BF16),
        compiler_params=_cparams(("parallel", "parallel", "arbitrary")),
        name="moba_prompt",
    )(hm, hm, hm, kmean)


def _sel_matrix(n_cmp_rows, n_cols, n_c):
    n = _iota((n_cmp_rows, n_cols), 0)
    j = _iota((n_cmp_rows, n_cols), 1)
    ratio = NSA_SEL_BLOCK // NSA_CMP_STRIDE
    lo = jnp.clip(ratio * j - NSA_CMP_BLOCK // NSA_CMP_STRIDE + 1, 0, n_c)
    hi = jnp.clip(ratio * (j + 1), 0, n_c)
    return ((n >= lo) & (n < hi)).astype(BF16)


def _nsa_kernel(q_ref, kc_ref, vc_ref, ks_ref, vs_ref, kw_ref, vw_ref, sm_ref, o_ref, *, tq):
    g, qi = pl.program_id(1), pl.program_id(2)
    T = ks_ref.shape[0]
    NC = kc_ref.shape[0]
    n_c = NC - 1
    n_sel = T // NSA_SEL_BLOCK
    R = q_ref.shape[0]
    pos = qi * tq + _iota((tq, 1), 0)
    col = _iota((1, T), 1)

    kc = kc_ref[...].astype(BF16)
    vc = vc_ref[...].astype(BF16)
    ncol = _iota((1, NC), 1)
    mask_c = (ncol * NSA_CMP_STRIDE + (NSA_CMP_BLOCK - 1) <= pos) & (ncol < n_c)
    o_cmp = []
    psum = jnp.zeros((tq, NC), F32)
    for r in range(R):
        e, l = _softmax_unnorm(_dot_nt(q_ref[r], kc), mask_c)
        p = e / l
        psum = psum + p
        o_cmp.append(_dot(p.astype(BF16), vc))
    imp = _dot3(psum, _sel_matrix(NC, LANES, n_c))
    own = pos // NSA_SEL_BLOCK
    sel = _rank_select(imp, n_sel, own, NSA_SEL_COUNT - 1)
    expand = (_iota((LANES, T), 1) // NSA_SEL_BLOCK == _iota((LANES, T), 0)).astype(BF16)
    selk = _dot(jnp.where(sel, 1.0, 0.0).astype(BF16), expand)
    mask_s = (selk > 0.5) & (col <= pos)
    diff = pos - col
    mask_w = (diff >= 0) & (diff < NSA_WINDOW)

    ks, vs, kw, vw = ks_ref[...], vs_ref[...], kw_ref[...], vw_ref[...]
    gates = sm_ref[...]
    for r in range(R):
        q = q_ref[r]
        hh = g * R + r
        e, l = _softmax_unnorm(_dot_nt(q, ks), mask_s)
        o_sel = _dot(e.astype(BF16), vs) / l
        e, l = _softmax_unnorm(_dot_nt(q, kw), mask_w)
        o_win = _dot(e.astype(BF16), vw) / l
        g0 = _lane_pick(gates, GATE_LANE0 + hh)
        g1 = _lane_pick(gates, GATE_LANE0 + 8 + hh)
        g2 = _lane_pick(gates, GATE_LANE0 + 16 + hh)
        o = g0 * o_cmp[r] + g1 * o_sel + g2 * o_win
        o_ref[:, r * HEAD_DIM:(r + 1) * HEAD_DIM] = o.astype(o_ref.dtype)


def _nsa_prompt(hm, cmp_kv, sm, *, B, T):
    tq = 256
    nq = T // tq
    G = NSA_KV_HEADS
    R = 8 // G
    NC = cmp_kv.shape[2]
    slot = lambda s0: pl.BlockSpec((None, None, T, HEAD_DIM), lambda b, g, q: (b, s0 + g, 0, 0))
    return pl.pallas_call(
        functools.partial(_nsa_kernel, tq=tq),
        grid=(B, G, nq),
        in_specs=[pl.BlockSpec((None, R, tq, HEAD_DIM), lambda b, g, q: (b, _SLOT["nq"] // R + g, q, 0)),
                  pl.BlockSpec((None, None, NC, HEAD_DIM), lambda b, g, q: (b, g, 0, 0)),
                  pl.BlockSpec((None, None, NC, HEAD_DIM), lambda b, g, q: (b, G + g, 0, 0)),
                  slot(_SLOT["nks"]), slot(_SLOT["nvs"]), slot(_SLOT["nkw"]), slot(_SLOT["nvw"]),
                  pl.BlockSpec((tq, LANES), lambda b, g, q: (b * nq + q, 0))],
        out_specs=pl.BlockSpec((tq, R * HEAD_DIM), lambda b, g, q: (b * nq + q, g)),
        out_shape=jax.ShapeDtypeStruct((B * T, GW), BF16),
        compiler_params=_cparams(("parallel", "parallel", "arbitrary")),
        name="nsa_prompt",
    )(hm, cmp_kv, cmp_kv, hm, hm, hm, hm, sm)


def _compress_kernel(c_ref, pe_ref, w1_ref, w2_ref, g_ref, o_ref):
    c = pl.program_id(1)
    x = c_ref[...]
    NC = x.shape[0]
    a = _dot((x + pe_ref[0]).astype(BF16), w1_ref[0])
    bm = _dot((x + pe_ref[1]).astype(BF16), w1_ref[1])
    pre = a + pltpu.roll(bm, NC - 1, 0)
    hid = pre * (1.0 / (1.0 + jnp.exp(-pre)))
    out = _dot(hid.astype(BF16), w2_ref[...])
    out = jnp.where(c < NSA_KV_HEADS, _head_rms(out, g_ref[...]), out)
    o_ref[...] = jnp.where(_iota(out.shape, 0) < NC - 1, out, 0.0)


def _compress(cmp_in, pe, w1, w2, gain):
    B, C, L, _ = cmp_in.shape
    NC = L // NSA_CMP_STRIDE
    W = NSA_CMP_STRIDE * HEAD_DIM
    x = cmp_in.reshape(B, C, NC, W)
    G = NSA_KV_HEADS
    return pl.pallas_call(
        _compress_kernel,
        grid=(B, C),
        in_specs=[pl.BlockSpec((None, None, NC, W), lambda b, c: (b, c, 0, 0)),
                  pl.BlockSpec((None, 2, 1, W), lambda b, c: (c // G, 0, 0, 0)),
                  pl.BlockSpec((None, 2, W, HEAD_DIM), lambda b, c: (c // G, 0, 0, 0)),
                  pl.BlockSpec((None, HEAD_DIM, HEAD_DIM), lambda b, c: (c // G, 0, 0)),
                  pl.BlockSpec((1, HEAD_DIM), lambda b, c: (0, 0))],
        out_specs=pl.BlockSpec((None, None, NC, HEAD_DIM), lambda b, c: (b, c, 0, 0)),
        out_shape=jax.ShapeDtypeStruct((B, C, NC, HEAD_DIM), F32),
        compiler_params=_cparams(("parallel", "parallel")),
        name="nsa_compress",
    )(x, pe, w1, w2, gain)


def _pool_kernel(u_ref, halo_ref, w_ref, sc_ref, o_ref, ext_ref, *, tp):
    t = pl.program_id(1)
    HALO = halo_ref.shape[0]
    halo = halo_ref[...]
    ext_ref[0:HALO, :] = jnp.where(t > 0, halo, 0.0)
    ext_ref[HALO:HALO + tp, :] = u_ref[...]
    pos = t * tp + _iota((tp, 1), 0)
    PG = GW // len(POOL_WINDOWS)
    for gi, w in enumerate(POOL_WINDOWS):
        c0 = gi * PG
        u = ext_ref[HALO:HALO + tp, c0:c0 + PG]
        win = u
        for k in range(1, w):
            win = win + ext_ref[HALO - k:HALO - k + tp, c0:c0 + PG]
        cnt = jnp.minimum(w, pos + 1).astype(F32)
        d = win / cnt - u
        o = _dot(d.astype(BF16), w_ref[gi]) * sc_ref[:, c0:c0 + PG]
        o_ref[:, c0:c0 + PG] = o.astype(o_ref.dtype)


def _pool_prompt(proj, pool_w, pool_scale, *, B, T):
    tp = 512
    HALO = 16
    nt = T // tp
    cb = _SEG["pu"] // GW
    return pl.pallas_call(
        functools.partial(_pool_kernel, tp=tp),
        grid=(B, nt),
        in_specs=[pl.BlockSpec((tp, GW), lambda b, t: (b * nt + t, cb)),
                  pl.BlockSpec((HALO, GW), lambda b, t: (jnp.maximum((b * nt + t) * (tp // HALO) - 1, 0), cb)),
                  pl.BlockSpec(pool_w.shape, lambda b, t: (0, 0, 0)),
                  pl.BlockSpec((1, GW), lambda b, t: (0, 0))],
        out_specs=pl.BlockSpec((tp, GW), lambda b, t: (b * nt + t, 0)),
        out_shape=jax.ShapeDtypeStruct((B * T, GW), BF16),
        scratch_shapes=[pltpu.VMEM((HALO + tp, GW), F32)],
        compiler_params=_cparams(("parallel", "arbitrary")),
        name="pool_prompt",
    )(proj, proj, pool_w, pool_scale)


def _blockdiag(x, n):
    t = jnp.concatenate([x] * n, axis=1)
    keep = _iota(t.shape, 1) // HEAD_DIM == _iota(t.shape, 0)
    return jnp.where(keep, t, jnp.zeros_like(t))


def _fold_diag(x, n):
    keep = _iota(x.shape, 1) // HEAD_DIM == _iota(x.shape, 0)
    xm = jnp.where(keep, x, 0.0)
    out = xm[:, 0:HEAD_DIM]
    for h in range(1, n):
        out = out + xm[:, h * HEAD_DIM:(h + 1) * HEAD_DIM]
    return out


def _diag_col(row, lane0, n):
    b = jnp.broadcast_to(row, (n, row.shape[1]))
    keep = _iota(b.shape, 1) == _iota(b.shape, 0) + lane0
    return jnp.sum(jnp.where(keep, b, 0.0), axis=1, keepdims=True)


def _pad_rows(x, n):
    return jnp.concatenate([x, jnp.zeros((n - x.shape[0],) + x.shape[1:], x.dtype)], axis=0)


def _sfox_kernel(pt_ref, kv_ref, lf_ref, q_ref, new_ref, sm_ref, o_ref, m_ref, l_ref, acc_ref, carry_ref):
    b, i = pl.program_id(0), pl.program_id(1)
    H = 8
    q = q_ref[...]

    @pl.when(i == 0)
    def _():
        knew, vnew = new_ref[0:H, :], new_ref[H:2 * H, :]
        m_ref[...] = jnp.sum(q * knew, axis=1, keepdims=True)
        l_ref[...] = jnp.ones_like(l_ref)
        acc_ref[...] = _blockdiag(vnew, H)
        carry_ref[...] = _diag_col(sm_ref[pl.ds(b, 1), :], 0, H)

    kv = kv_ref[...]
    W = H * HEAD_DIM
    kb = kv[:, 0:W].astype(BF16)
    vb = kv[:, W:2 * W].astype(BF16)
    P = kv.shape[0]
    qbd = _pad_rows(_blockdiag(q, H), 16).astype(BF16)
    s = _dot_nt(qbd, kb)[0:H, :]
    lf = lf_ref[...]
    later = (_iota((P, P), 0) > _iota((P, P), 1)).astype(BF16)
    s = s + _dot3(lf, later) + carry_ref[...]
    m_new = jnp.maximum(m_ref[...], jnp.max(s, axis=1, keepdims=True))
    alpha = jnp.exp(m_ref[...] - m_new)
    p = jnp.exp(s - m_new)
    l_ref[...] = alpha * l_ref[...] + jnp.sum(p, axis=1, keepdims=True)
    pv = _dot(_pad_rows(p, 16).astype(BF16), vb)[0:H, :]
    acc_ref[...] = alpha * acc_ref[...] + pv
    m_ref[...] = m_new
    carry_ref[...] = carry_ref[...] + jnp.sum(lf, axis=1, keepdims=True)

    @pl.when(i == pl.num_programs(1) - 1)
    def _():
        o_ref[...] = _fold_diag(acc_ref[...], H) / jnp.maximum(l_ref[...], 1e-30)


def _fox_sample(layer, page_table, cache_kv, cache_lfT, hm_s, fox_new, sm_s, *, Bs):
    NP = page_table.shape[1]
    PS = cache_kv.shape[2]
    kvw = cache_kv.shape[3]
    grid_spec = pltpu.PrefetchScalarGridSpec(
        num_scalar_prefetch=1,
        grid=(Bs, NP),
        in_specs=[pl.BlockSpec((None, None, PS, kvw), lambda b, i, pt: (layer, pt[b, NP - 1 - i], 0, 0)),
                  pl.BlockSpec((None, None, 8, PS), lambda b, i, pt: (layer, pt[b, NP - 1 - i], 0, 0)),
                  pl.BlockSpec((None, None, 8, HEAD_DIM), lambda b, i, pt: (b, _SLOT["fq"] // 8, 0, 0)),
                  pl.BlockSpec((None, 16, HEAD_DIM), lambda b, i, pt: (b, 0, 0)),
                  pl.BlockSpec(sm_s.shape, lambda b, i, pt: (0, 0))],
        out_specs=pl.BlockSpec((None, 8, HEAD_DIM), lambda b, i, pt: (b, 0, 0)),
        scratch_shapes=[pltpu.VMEM((8, 1), F32), pltpu.VMEM((8, 1), F32), pltpu.VMEM((8, 8 * HEAD_DIM), F32),
                        pltpu.VMEM((8, 1), F32)],
    )
    return pl.pallas_call(
        _sfox_kernel,
        grid_spec=grid_spec,
        out_shape=jax.ShapeDtypeStruct((Bs, 8, HEAD_DIM), F32),
        compiler_params=_cparams(("parallel", "arbitrary")),
        name="fox_sample",
    )(page_table, cache_kv, cache_lfT, hm_s.reshape(hm_s.shape[0], NSLOT // 8, 8, HEAD_DIM), fox_new, sm_s)


def _argmax_rounds(score, n_rounds, out_lanes):
    rows, L = score.shape
    lane = _iota(score.shape, 1)
    olane = _iota((rows, out_lanes), 1)
    out = jnp.zeros((rows, out_lanes), jnp.int32)
    for t in range(n_rounds):
        m = jnp.max(score, axis=1, keepdims=True)
        idx = jnp.min(jnp.where(score == m, lane, L), axis=1, keepdims=True)
        ok = jnp.where(m > NEG, 1, 0)
        out = jnp.where(olane == t, idx, out)
        out = jnp.where(olane == n_rounds + t, ok, out)
        score = jnp.where(lane == idx, NEG, score)
    return out


def _smoba_gate_kernel(pt_ref, k0_ref, k1_ref, q_ref, o_ref, g_ref):
    i = pl.program_id(1)
    NB = pl.num_programs(1)
    H = 8

    @pl.when(i == 0)
    def _():
        g_ref[...] = jnp.full_like(g_ref, NEG)

    ksum = jnp.sum(k0_ref[...], axis=0, keepdims=True) + jnp.sum(k1_ref[...], axis=0, keepdims=True)
    kmean = ksum / float(MOBA_BLOCK)
    rows = [kmean[:, (h // 2) * HEAD_DIM:(h // 2 + 1) * HEAD_DIM] for h in range(H)]
    kme = jnp.concatenate(rows, axis=0)
    gate = jnp.sum(q_ref[...] * kme, axis=1, keepdims=True)
    g_ref[...] = jnp.where(_iota(g_ref.shape, 1) == i, gate, g_ref[...])

    @pl.when(i == NB - 1)
    def _():
        o_ref[...] = _argmax_rounds(g_ref[...], MOBA_TOPK, LANES)


def _moba_sample_gate(layer, page_table, cache_kv, hm_s, *, Bs):
    NP = page_table.shape[1]
    PS = cache_kv.shape[2]
    ppb = MOBA_BLOCK // PS
    assert ppb == 2
    NB = NP // ppb
    kw = cache_kv.shape[3] // 2
    page = lambda j: pl.BlockSpec((None, None, PS, kw), lambda b, i, pt: (layer, pt[b, ppb * i + j], 0, 0))
    grid_spec = pltpu.PrefetchScalarGridSpec(
        num_scalar_prefetch=1,
        grid=(Bs, NB),
        in_specs=[page(0), page(1),
                  pl.BlockSpec((None, None, 8, HEAD_DIM), lambda b, i, pt: (b, _SLOT["mq"] // 8, 0, 0))],
        out_specs=pl.BlockSpec((None, 8, LANES), lambda b, i, pt: (b, 0, 0)),
        scratch_shapes=[pltpu.VMEM((8, max(LANES, NB)), F32)],
    )
    return pl.pallas_call(
        _smoba_gate_kernel,
        grid_spec=grid_spec,
        out_shape=jax.ShapeDtypeStruct((Bs, 8, LANES), jnp.int32),
        compiler_params=_cparams(("parallel", "arbitrary")),
        name="moba_sample_gate",
    )(page_table, cache_kv, cache_kv, hm_s.reshape(hm_s.shape[0], NSLOT // 8, 8, HEAD_DIM))


def _smoba_attn_kernel(pg_ref, ok_ref, k_ref, v_ref, q_ref, new_ref, o_ref, ks_ref, vs_ref, *, n_pages, ppb):
    b, h, j = pl.program_id(0), pl.program_id(1), pl.program_id(2)
    ks_ref[j] = k_ref[...]
    vs_ref[j] = v_ref[...]

    @pl.when(j == n_pages - 1)
    def _():
        PS = k_ref.shape[0]
        kvh = h // 2
        q = q_ref[pl.ds(h, 1), :]
        knew = new_ref[pl.ds(kvh, 1), :]
        vnew = new_ref[pl.ds(4 + kvh, 1), :]
        s_new = jnp.sum(q * knew, axis=1, keepdims=True)
        s, ok = [], []
        for t in range(n_pages):
            s.append(jnp.sum(ks_ref[t] * q, axis=1, keepdims=True))
            ok.append(ok_ref[(b * 8 + h) * (n_pages // ppb) + t // ppb] > 0)
        m = s_new
        for t in range(n_pages):
            m = jnp.maximum(m, jnp.max(jnp.where(ok[t], s[t], NEG), axis=0, keepdims=True))
        l = jnp.exp(s_new - m)
        acc = l * vnew
        for t in range(n_pages):
            e = jnp.where(ok[t], jnp.exp(s[t] - m), 0.0)
            l = l + jnp.sum(e, axis=0, keepdims=True)
            acc = acc + jnp.sum(e * vs_ref[t], axis=0, keepdims=True)
        o_ref[pl.ds(h, 1), :] = acc / jnp.maximum(l, 1e-30)


def _moba_sample_attn(layer, pages, oks, cache_kv, hm_s, moba_new, *, Bs):
    PS = cache_kv.shape[2]
    ppb = MOBA_BLOCK // PS
    n_pages = MOBA_TOPK * ppb
    nkv = 4
    idx = lambda b, h, j, pg, ok: pg[(b * 8 + h) * n_pages + j]
    grid_spec = pltpu.PrefetchScalarGridSpec(
        num_scalar_prefetch=2,
        grid=(Bs, 8, n_pages),
        in_specs=[pl.BlockSpec((None, None, PS, HEAD_DIM), lambda b, h, j, pg, ok: (layer, idx(b, h, j, pg, ok), 0, h // 2)),
                  pl.BlockSpec((None, None, PS, HEAD_DIM), lambda b, h, j, pg, ok: (layer, idx(b, h, j, pg, ok), 0, nkv + h // 2)),
                  pl.BlockSpec((None, None, 8, HEAD_DIM), lambda b, h, j, pg, ok: (b, _SLOT["mq"] // 8, 0, 0)),
                  pl.BlockSpec((None, 8, HEAD_DIM), lambda b, h, j, pg, ok: (b, 0, 0))],
        out_specs=pl.BlockSpec((None, 8, HEAD_DIM), lambda b, h, j, pg, ok: (b, 0, 0)),
        scratch_shapes=[pltpu.VMEM((n_pages, PS, HEAD_DIM), F32), pltpu.VMEM((n_pages, PS, HEAD_DIM), F32)],
    )
    return pl.pallas_call(
        functools.partial(_smoba_attn_kernel, n_pages=n_pages, ppb=ppb),
        grid_spec=grid_spec,
        out_shape=jax.ShapeDtypeStruct((Bs, 8, HEAD_DIM), F32),
        compiler_params=_cparams(("parallel", "arbitrary", "arbitrary")),
        name="moba_sample_attn",
    )(pages, oks, cache_kv, cache_kv, hm_s.reshape(hm_s.shape[0], NSLOT // 8, 8, HEAD_DIM), moba_new)


def _snsa_gather_kernel(pt_ref, x_ref, o_ref):
    for c in range(o_ref.shape[0]):
        o_ref[c] = x_ref[:, c * HEAD_DIM:(c + 1) * HEAD_DIM]


def _nsa_sample_gather(layer, page_table, cache_kv, *, Bs):
    NP = page_table.shape[1]
    PS = cache_kv.shape[2]
    C = 2 * NSA_KV_HEADS
    grid_spec = pltpu.PrefetchScalarGridSpec(
        num_scalar_prefetch=1,
        grid=(Bs, NP),
        in_specs=[pl.BlockSpec((None, None, PS, C * HEAD_DIM), lambda b, i, pt: (layer, pt[b, i], 0, 0))],
        out_specs=pl.BlockSpec((None, C, PS, HEAD_DIM), lambda b, i, pt: (b, 0, i, 0)),
    )
    return pl.pallas_call(
        _snsa_gather_kernel,
        grid_spec=grid_spec,
        out_shape=jax.ShapeDtypeStruct((Bs, C, NP * PS, HEAD_DIM), F32),
        compiler_params=_cparams(("parallel", "arbitrary")),
        name="nsa_sample_gather",
    )(page_table, cache_kv)


def _snsa_cmp_kernel(q_ref, kc_ref, vc_ref, o_ref, sel_ref, *, n_sel_past):
    R = q_ref.shape[0]
    NC = kc_ref.shape[0]
    n_c = NC - 1
    q = _pad_rows(q_ref[...], 16).astype(BF16)
    s = _dot_nt(q, kc_ref[...].astype(BF16))[0:R, :]
    mask = _iota((1, NC), 1) < n_c
    e, l = _softmax_unnorm(s, mask)
    p = e / l
    o_ref[...] = _dot(_pad_rows(p, 16).astype(BF16), vc_ref[...].astype(BF16))[0:R, :]
    psum = jnp.sum(p, axis=0, keepdims=True)
    imp = _dot3(_pad_rows(psum, 16), _sel_matrix(NC, n_sel_past, n_c))[0:1, :]
    sel_ref[...] = _argmax_rounds(imp, NSA_SEL_COUNT - 1, LANES)


def _nsa_sample_cmp(hm_s, cmp_kv, *, Bs):
    G = NSA_KV_HEADS
    R = 8 // G
    NC = cmp_kv.shape[2]
    n_sel_past = NC * NSA_CMP_STRIDE // NSA_SEL_BLOCK
    return pl.pallas_call(
        functools.partial(_snsa_cmp_kernel, n_sel_past=n_sel_past),
        grid=(Bs, G),
        in_specs=[pl.BlockSpec((None, None, R, HEAD_DIM), lambda b, g: (b, _SLOT["nq"] // R + g, 0, 0)),
                  pl.BlockSpec((None, None, NC, HEAD_DIM), lambda b, g: (b, g, 0, 0)),
                  pl.BlockSpec((None, None, NC, HEAD_DIM), lambda b, g: (b, G + g, 0, 0))],
        out_specs=[pl.BlockSpec((None, None, R, HEAD_DIM), lambda b, g: (b, g, 0, 0)),
                   pl.BlockSpec((None, None, 1, LANES), lambda b, g: (b, g, 0, 0))],
        out_shape=[jax.ShapeDtypeStruct((Bs, G, R, HEAD_DIM), F32), jax.ShapeDtypeStruct((Bs, G, 1, LANES), jnp.int32)],
        compiler_params=_cparams(("parallel", "parallel")),
        name="nsa_sample_cmp",
    )(hm_s.reshape(hm_s.shape[0], NSLOT // R, R, HEAD_DIM), cmp_kv, cmp_kv)


def _snsa_final_kernel(pg_ref, hf_ref, ok_ref, k_ref, v_ref, q_ref, new_ref, kw_ref, vw_ref, wnew_ref, ocmp_ref,
                       sm_ref, o_ref, ks_ref, vs_ref, *, n_sel):
    b, g, j = pl.program_id(0), pl.program_id(1), pl.program_id(2)
    ks_ref[j] = k_ref[...]
    vs_ref[j] = v_ref[...]

    @pl.when(j == n_sel - 1)
    def _():
        R = q_ref.shape[0]
        SB = k_ref.shape[0]
        q = _pad_rows(q_ref[...], 16).astype(BF16)
        qf = q_ref[...]
        G = NSA_KV_HEADS
        ks_new = new_ref[pl.ds(2 * G + g, 1), :]
        vs_new = new_ref[pl.ds(3 * G + g, 1), :]
        s_new = jnp.sum(qf * ks_new, axis=1, keepdims=True)
        s, ok = [], []
        for t in range(n_sel):
            s.append(_dot_nt(q, ks_ref[t].astype(BF16))[0:R, :])
            ok.append(ok_ref[(b * G + g) * n_sel + t] > 0)
        m = s_new
        for t in range(n_sel):
            m = jnp.maximum(m, jnp.max(jnp.where(ok[t], s[t], NEG), axis=1, keepdims=True))
        l = jnp.exp(s_new - m)
        acc = l * vs_new
        for t in range(n_sel):
            e = jnp.where(ok[t], jnp.exp(s[t] - m), 0.0)
            l = l + jnp.sum(e, axis=1, keepdims=True)
            acc = acc + _dot(_pad_rows(e, 16).astype(BF16), vs_ref[t].astype(BF16))[0:R, :]
        o_sel = acc / jnp.maximum(l, 1e-30)
        WB = kw_ref.shape[0]
        kw_new = wnew_ref[pl.ds(g, 1), :]
        vw_new = wnew_ref[pl.ds(G + g, 1), :]
        sw = _dot_nt(q, kw_ref[...].astype(BF16))[0:R, :]
        mask_w = (WB - _iota((1, WB), 1)) < NSA_WINDOW
        sw_new = jnp.sum(qf * kw_new, axis=1, keepdims=True)
        mw = jnp.maximum(sw_new, jnp.max(jnp.where(mask_w, sw, NEG), axis=1, keepdims=True))
        ew = jnp.where(mask_w, jnp.exp(sw - mw), 0.0)
        ew_new = jnp.exp(sw_new - mw)
        lw = ew_new + jnp.sum(ew, axis=1, keepdims=True)
        o_win = (ew_new * vw_new + _dot(_pad_rows(ew, 16).astype(BF16), vw_ref[...].astype(BF16))[0:R, :]) / jnp.maximum(lw, 1e-30)
        gates = sm_ref[pl.ds(b, 1), :]
        g0 = _diag_col(gates, GATE_LANE0 + g * R, R)
        g1 = _diag_col(gates, GATE_LANE0 + 8 + g * R, R)
        g2 = _diag_col(gates, GATE_LANE0 + 16 + g * R, R)
        o_ref[...] = g0 * ocmp_ref[...] + g1 * o_sel + g2 * o_win


def _nsa_sample_final(layer, pages, halves, oks, cache_kv, hm_s, nsa_new, win_state, win_new, o_cmp, sm_s, *, Bs):
    G = NSA_KV_HEADS
    R = 8 // G
    n_sel = NSA_SEL_COUNT - 1
    SB = NSA_SEL_BLOCK
    WB = win_state.shape[2]
    idx = lambda b, g, j: (b * G + g) * n_sel + j
    grid_spec = pltpu.PrefetchScalarGridSpec(
        num_scalar_prefetch=3,
        grid=(Bs, G, n_sel),
        in_specs=[pl.BlockSpec((None, None, SB, HEAD_DIM), lambda b, g, j, pg, hf, ok: (layer, pg[idx(b, g, j)], hf[idx(b, g, j)], 2 * G + g)),
                  pl.BlockSpec((None, None, SB, HEAD_DIM), lambda b, g, j, pg, hf, ok: (layer, pg[idx(b, g, j)], hf[idx(b, g, j)], 3 * G + g)),
                  pl.BlockSpec((None, None, R, HEAD_DIM), lambda b, g, j, pg, hf, ok: (b, _SLOT["nq"] // R + g, 0, 0)),
                  pl.BlockSpec((None, 8, HEAD_DIM), lambda b, g, j, pg, hf, ok: (b, 0, 0)),
                  pl.BlockSpec((None, None, WB, HEAD_DIM), lambda b, g, j, pg, hf, ok: (layer, b, 0, g)),
                  pl.BlockSpec((None, None, WB, HEAD_DIM), lambda b, g, j, pg, hf, ok: (layer, b, 0, G + g)),
                  pl.BlockSpec((None, 2 * G, HEAD_DIM), lambda b, g, j, pg, hf, ok: (b, 0, 0)),
                  pl.BlockSpec((None, None, R, HEAD_DIM), lambda b, g, j, pg, hf, ok: (b, g, 0, 0)),
                  pl.BlockSpec(sm_s.shape, lambda b, g, j, pg, hf, ok: (0, 0))],
        out_specs=pl.BlockSpec((None, None, R, HEAD_DIM), lambda b, g, j, pg, hf, ok: (b, g, 0, 0)),
        scratch_shapes=[pltpu.VMEM((n_sel, SB, HEAD_DIM), F32), pltpu.VMEM((n_sel, SB, HEAD_DIM), F32)],
    )
    return pl.pallas_call(
        functools.partial(_snsa_final_kernel, n_sel=n_sel),
        grid_spec=grid_spec,
        out_shape=jax.ShapeDtypeStruct((Bs, G, R, HEAD_DIM), F32),
        compiler_params=_cparams(("parallel", "parallel", "arbitrary")),
        name="nsa_sample_final",
    )(pages, halves, oks, cache_kv, cache_kv, hm_s.reshape(hm_s.shape[0], NSLOT // R, R, HEAD_DIM), nsa_new, win_state, win_state, win_new, o_cmp, sm_s)


def _spool_kernel(st_ref, u_ref, w_ref, sc_ref, o_ref, *, Bs, pos0):
    u_all = u_ref[...]
    PG = GW // len(POOL_WINDOWS)
    NB = st_ref.shape[1]
    out = []
    for gi, w in enumerate(POOL_WINDOWS):
        c0 = gi * PG
        u = u_all[:, c0:c0 + PG]
        win = u
        for k in range(1, w):
            prev = st_ref[:, NB - k, c0:c0 + PG]
            if prev.shape[0] != u.shape[0]:
                prev = _pad_rows(prev, u.shape[0])
            win = win + prev
        d = win / float(min(w, pos0 + 1)) - u
        out.append(_dot(d.astype(BF16), w_ref[gi]) * sc_ref[:, c0:c0 + PG])
    o_ref[...] = jnp.concatenate(out, axis=1).astype(o_ref.dtype)


def _pool_sample(layer, state_pool, proj_s, pool_w, pool_scale, *, Bs, pos0):
    Ms = proj_s.shape[0]
    NB = state_pool.shape[2]
    assert NB >= max(POOL_WINDOWS) - 1 and pos0 >= NB
    cb = _SEG["pu"] // GW
    return pl.pallas_call(
        functools.partial(_spool_kernel, Bs=Bs, pos0=pos0),
        grid=(1,),
        in_specs=[pl.BlockSpec((None, Bs, NB, GW), lambda i: (layer, 0, 0, 0)),
                  pl.BlockSpec((Ms, GW), lambda i: (0, cb)),
                  pl.BlockSpec(pool_w.shape, lambda i: (0, 0, 0)),
                  pl.BlockSpec((1, GW), lambda i: (0, 0))],
        out_specs=pl.BlockSpec((Ms, GW), lambda i: (0, 0)),
        out_shape=jax.ShapeDtypeStruct((Ms, GW), BF16),
        compiler_params=_cparams(("arbitrary",)),
        name="pool_sample",
    )(state_pool, proj_s, pool_w, pool_scale)


def _layer_weights(l, norm1_g, w_in, fox_f_bias, fox_qk_g, moba_qk_g, nsa_qk_g, pool_w, pool_scale, cmp_pe, cmp_w1,
                   cmp_w2, w_out, norm2_g, w_up, w_down):
    D = w_in.shape[1]
    segs, c = {}, 0
    for name, n in _ORIG:
        segs[name] = w_in[l, :, c:c + n]
        c += n
    cols = [segs[n] for n in _NEW_ORDER]
    used = sum(x.shape[1] for x in cols)
    cols.append(jnp.zeros((D, PW - used), w_in.dtype))
    W = NSA_CMP_STRIDE * HEAD_DIM
    return dict(
        norm1_g=norm1_g[l], norm2_g=norm2_g[l],
        w_in=jnp.concatenate(cols, axis=1).astype(BF16),
        w_out=w_out[l].astype(BF16), w_up=w_up[l].astype(BF16), w_down=w_down[l].astype(BF16),
        fox_f_bias=fox_f_bias[l], fox_qk_g=fox_qk_g[l], moba_qk_g=moba_qk_g[l], nsa_qk_g=nsa_qk_g[l],
        pool_w=pool_w[l].astype(BF16), pool_scale=pool_scale[l].reshape(1, GW),
        cmp_pe=cmp_pe[l].reshape(2, 2, 1, W), cmp_w1=cmp_w1[l].reshape(2, 2, W, HEAD_DIM).astype(BF16),
        cmp_w2=cmp_w2[l].astype(BF16), kc_gain=nsa_qk_g[l][1:2],
    )


def _mlp(x1, lw, *, tm):
    h2 = _rms(x1, lw["norm2_g"], tm=min(tm, 512))
    u = _matmul(h2, lw["w_up"], tm=tm, tn=512 if tm > 16 else 1024, act="relu2", out_dtype=BF16, name="mm_up")
    return _matmul_down(x1, u, lw["w_down"], tm=tm, tn=1024, tk=2048)


def _prompt_layer(x, lw, *, B, T):
    tm = min(1024, B * T)
    h = _rms(x, lw["norm1_g"], tm=min(tm, 512))
    proj = _matmul(h, lw["w_in"], tm=tm, tn=512, name="mm_in")
    fox_kv, moba_kv, nsa_kv, win, sm, hm, cmp_in, cc, crow, kmean = _prep(
        proj, lw["fox_qk_g"], lw["moba_qk_g"], lw["nsa_qk_g"], lw["fox_f_bias"], B=B, T=T, prompt=True)
    o_fox = _fox_prompt(hm, cc, crow, B=B, T=T)
    o_pool = _pool_prompt(proj, lw["pool_w"], lw["pool_scale"], B=B, T=T)
    o_moba = _moba_prompt(hm, jnp.swapaxes(kmean, 1, 2), B=B, T=T)
    cmp_kv = _compress(cmp_in, lw["cmp_pe"], lw["cmp_w1"], lw["cmp_w2"], lw["kc_gain"])
    o_nsa = _nsa_prompt(hm, cmp_kv, sm, B=B, T=T)
    x1 = _matmul_out(x, (o_fox, o_pool, o_moba, o_nsa), lw["w_out"], tm=tm, tn=512)
    y = _mlp(x1, lw, tm=tm)
    wb = min(NSA_WINDOW, T)
    new = (fox_kv.reshape(B, T, 2, 8, HEAD_DIM), sm[:, 0:8].reshape(B, T, 8),
           moba_kv.reshape(B, T, 2, 4, HEAD_DIM), nsa_kv.reshape(B, T, 4, NSA_KV_HEADS, HEAD_DIM),
           win.reshape(B, T, 2, NSA_KV_HEADS, HEAD_DIM)[:, T - wb:],
           proj.reshape(B, T, PW)[:, T - POOL_BUF:, _SEG["pu"]:_SEG["pu"] + GW])
    return y, new


def _sample_layer(x, l, lw, caches, page_table, *, Bs):
    cache_fox_kv, cache_fox_lfT, cache_moba_kv, cache_nsa_kv, state_nsa_win, state_pool = caches
    Ms = x.shape[0]
    NP = page_table.shape[1]
    PS = cache_fox_kv.shape[2]
    past = NP * PS
    G = NSA_KV_HEADS
    h = _rms(x, lw["norm1_g"], tm=Ms)
    proj = _matmul(h, lw["w_in"], tm=Ms, tn=1024, name="mm_in_s")
    fox_kv, moba_kv, nsa_kv, win, sm, hm = _prep(
        proj, lw["fox_qk_g"], lw["moba_qk_g"], lw["nsa_qk_g"], lw["fox_f_bias"], B=1, T=Ms, prompt=False)
    hm_s = jnp.swapaxes(hm[0], 0, 1).astype(F32)
    fox_new = fox_kv.reshape(Ms, 16, HEAD_DIM)
    moba_new = moba_kv.reshape(Ms, 8, HEAD_DIM)
    nsa_new = nsa_kv.reshape(Ms, 8, HEAD_DIM)
    win_new = win.reshape(Ms, 4, HEAD_DIM)

    o_fox = _fox_sample(l, page_table, cache_fox_kv, cache_fox_lfT, hm_s, fox_new, sm, Bs=Bs)

    o_pool = _pool_sample(l, state_pool, proj, lw["pool_w"], lw["pool_scale"], Bs=Bs, pos0=past)

    top = _moba_sample_gate(l, page_table, cache_moba_kv, hm_s, Bs=Bs)
    ppb = MOBA_BLOCK // PS
    blk = top[:, :, 0:MOBA_TOPK]
    oks = top[:, :, MOBA_TOPK:2 * MOBA_TOPK]
    blk = jnp.where(oks > 0, blk, 0)
    pidx = (blk[..., None] * ppb + jnp.arange(ppb, dtype=jnp.int32)).reshape(Bs, 8 * MOBA_TOPK * ppb)
    pages = jnp.take_along_axis(page_table, pidx, axis=1).reshape(-1)
    o_moba = _moba_sample_attn(l, pages, oks.reshape(-1), cache_moba_kv, hm_s, moba_new, Bs=Bs)

    cmp_in = _nsa_sample_gather(l, page_table, cache_nsa_kv, Bs=Bs)
    cmp_kv = _compress(cmp_in, lw["cmp_pe"], lw["cmp_w1"], lw["cmp_w2"], lw["kc_gain"])
    o_cmp, sel = _nsa_sample_cmp(hm_s, cmp_kv, Bs=Bs)
    n_sel = NSA_SEL_COUNT - 1
    sblk = sel[:, :, 0, 0:n_sel]
    soks = sel[:, :, 0, n_sel:2 * n_sel]
    sblk = jnp.where(soks > 0, sblk, 0)
    spp = PS // NSA_SEL_BLOCK
    spages = jnp.take_along_axis(page_table, (sblk // spp).reshape(Bs, G * n_sel), axis=1).reshape(-1)
    o_nsa = _nsa_sample_final(l, spages, (sblk % spp).reshape(-1), soks.reshape(-1), cache_nsa_kv, hm_s, nsa_new,
                              state_nsa_win, win_new, o_cmp, sm, Bs=Bs)

    def rows(o):
        o = o.reshape(Bs, GW).astype(BF16)
        return jnp.concatenate([o, jnp.zeros((Ms - Bs, GW), BF16)], axis=0)

    x1 = _matmul_out(x, (rows(o_fox), o_pool, rows(o_moba), rows(o_nsa)), lw["w_out"], tm=Ms, tn=1024)
    y = _mlp(x1, lw, tm=Ms)
    new = (fox_kv[:Bs].reshape(Bs, 1, 2, 8, HEAD_DIM), sm[:Bs, 0:8].reshape(Bs, 1, 8),
           moba_kv[:Bs].reshape(Bs, 1, 2, 4, HEAD_DIM), nsa_kv[:Bs].reshape(Bs, 1, 4, G, HEAD_DIM),
           win[:Bs].reshape(Bs, 1, 2, G, HEAD_DIM), proj[:Bs, _SEG["pu"]:_SEG["pu"] + GW].reshape(Bs, 1, GW))
    return y, new


def kernel(x_prompt, x_sample, cache_fox_kv, cache_fox_logf, cache_moba_kv, cache_nsa_kv, state_nsa_win, state_pool,
           page_table, norm1_g, w_in, fox_f_bias, fox_qk_g, moba_qk_g, nsa_qk_g, pool_w, pool_scale, cmp_pe, cmp_w1,
           cmp_w2, w_out, norm2_g, w_up, w_down):
    B, T, D = x_prompt.shape
    Bs, Ts, _ = x_sample.shape
    assert Ts == 1 and D == N_MIXERS * GW
    depth = w_in.shape[0]
    n_pool, PS = cache_fox_kv.shape[1:3]
    WB = state_nsa_win.shape[2]
    Ms = 16
    assert Bs <= Ms

    caches = (cache_fox_kv.reshape(depth, n_pool, PS, 2 * GW),
              jnp.swapaxes(cache_fox_logf, 2, 3),
              cache_moba_kv.reshape(depth, n_pool, PS, GW),
              cache_nsa_kv.reshape(depth, n_pool, PS, GW),
              state_nsa_win.reshape(depth, Bs, WB, 2 * NSA_KV_HEADS * HEAD_DIM),
              state_pool)

    xp = x_prompt.reshape(B * T, D)
    xs = jnp.concatenate([x_sample.reshape(Bs, D), jnp.zeros((Ms - Bs, D), x_sample.dtype)], axis=0)
    new_p, new_s = [], []
    for l in range(depth):
        lw = _layer_weights(l, norm1_g, w_in, fox_f_bias, fox_qk_g, moba_qk_g, nsa_qk_g, pool_w, pool_scale, cmp_pe,
                            cmp_w1, cmp_w2, w_out, norm2_g, w_up, w_down)
        xp, rows_p = _prompt_layer(xp, lw, B=B, T=T)
        xs, rows_s = _sample_layer(xs, l, lw, caches, page_table, Bs=Bs)
        new_p.append(rows_p)
        new_s.append(rows_s)

    def stk(rows, i):
        return jnp.stack([r[i] for r in rows], axis=0)

    win_s = jnp.concatenate([state_nsa_win[:, :, 1:], stk(new_s, 4)], axis=2) if WB > 0 else stk(new_s, 4)[:, :, :0]
    pool_s = jnp.concatenate([state_pool[:, :, 1:], stk(new_s, 5)], axis=2)
    return (xp.reshape(B, T, D), xs[:Bs].reshape(Bs, 1, D),
            stk(new_p, 0), stk(new_s, 0), stk(new_p, 1), stk(new_s, 1), stk(new_p, 2), stk(new_s, 2),
            stk(new_p, 3), stk(new_s, 3), stk(new_p, 4), win_s, stk(new_p, 5), pool_s)
```

```python
import functools

import jax
import jax.numpy as jnp
from jax import lax
from jax.experimental import pallas as pl
from jax.experimental.pallas import tpu as pltpu

F32 = jnp.float32
BF16 = jnp.bfloat16

HEAD_DIM = 128
N_MIXERS = 4
FOX_FORGET_BIAS = 2.0
POOL_WINDOWS = (2, 4, 8, 16)
POOL_BUF = max(POOL_WINDOWS) - 1
MOBA_BLOCK = 256
MOBA_TOPK = 3
NSA_KV_HEADS = 2
NSA_CMP_BLOCK = 32
NSA_CMP_STRIDE = 16
NSA_SEL_BLOCK = 64
NSA_SEL_COUNT = 16
NSA_WINDOW = 512
RMS_EPS = 1e-6
SCALE = HEAD_DIM ** -0.5

LANES = 128
VMEM_LIMIT = 56 * 1024 * 1024
NEG = -1e30

GW = 1024
_SEG = dict(fq=0, fk=1024, fv=2048, pu=3072, mq=4096, mk=5120, mv=5632, nq=6144,
            nkc=7168, nvc=7424, nks=7680, nvs=7936, nkw=8192, nvw=8448, small=8704)
PW = 9216
_ORIG = (("fq", 1024), ("fk", 1024), ("fv", 1024), ("ff", 8), ("pu", 1024), ("mq", 1024), ("mk", 512),
         ("mv", 512), ("nq", 1024), ("nkc", 256), ("nvc", 256), ("nks", 256), ("nvs", 256), ("nkw", 256),
         ("nvw", 256), ("ng", 24))
_NEW_ORDER = ("fq", "fk", "fv", "pu", "mq", "mk", "mv", "nq", "nkc", "nvc", "nks", "nvs", "nkw", "nvw", "ff", "ng")
_SLOT = dict(fq=0, fk=8, fv=16, mq=24, mk=32, mv=36, nq=40, nks=48, nvs=50, nkw=52, nvw=54)
NSLOT = 56
GATE_LANE0 = 8


def _cparams(sem):
    return pltpu.CompilerParams(dimension_semantics=sem, vmem_limit_bytes=VMEM_LIMIT)


def _iota(shape, dim):
    return lax.broadcasted_iota(jnp.int32, shape, dim)


def _dot(a, b):
    return jnp.dot(a, b, preferred_element_type=F32)


def _dot_nt(a, b):
    return lax.dot_general(a, b, (((1,), (1,)), ((), ())), preferred_element_type=F32)


def _split3(x):
    hi = x.astype(BF16)
    r1 = x - hi.astype(F32)
    mid = r1.astype(BF16)
    lo = (r1 - mid.astype(F32)).astype(BF16)
    return hi, mid, lo


def _dot3(x, m):
    hi, mid, lo = _split3(x)
    return _dot(hi, m) + _dot(mid, m) + _dot(lo, m)


def _softmax_unnorm(s, mask):
    sm = jnp.where(mask, s, NEG)
    m = jnp.max(sm, axis=-1, keepdims=True)
    e = jnp.where(mask, jnp.exp(sm - m), 0.0)
    l = jnp.maximum(jnp.sum(e, axis=-1, keepdims=True), 1e-30)
    return e, l


def _rms_kernel(x_ref, g_ref, o_ref):
    x = x_ref[...]
    y = x * lax.rsqrt(jnp.mean(x * x, axis=-1, keepdims=True) + RMS_EPS)
    o_ref[...] = (y * g_ref[...]).astype(o_ref.dtype)


def _rms(x, g, tm):
    M, D = x.shape
    return pl.pallas_call(
        _rms_kernel,
        grid=(M // tm,),
        in_specs=[pl.BlockSpec((tm, D), lambda i: (i, 0)), pl.BlockSpec((1, D), lambda i: (0, 0))],
        out_specs=pl.BlockSpec((tm, D), lambda i: (i, 0)),
        out_shape=jax.ShapeDtypeStruct((M, D), BF16),
        compiler_params=_cparams(("parallel",)),
        name="rms",
    )(x, g.reshape(1, D))


def _mm_kernel(a_ref, w_ref, o_ref, *, act):
    acc = _dot(a_ref[...], w_ref[...])
    if act == "relu2":
        acc = jnp.maximum(acc, 0.0)
        acc = acc * acc
    o_ref[...] = acc.astype(o_ref.dtype)


def _matmul(a, w, *, tm, tn, act=None, out_dtype=F32, name="mm"):
    M, K = a.shape
    N = w.shape[1]
    return pl.pallas_call(
        functools.partial(_mm_kernel, act=act),
        grid=(M // tm, N // tn),
        in_specs=[pl.BlockSpec((tm, K), lambda i, j: (i, 0)), pl.BlockSpec((K, tn), lambda i, j: (0, j))],
        out_specs=pl.BlockSpec((tm, tn), lambda i, j: (i, j)),
        out_shape=jax.ShapeDtypeStruct((M, N), out_dtype),
        compiler_params=_cparams(("parallel", "parallel")),
        name=name,
    )(a, w)


def _mm_out_kernel(x_ref, a0, a1, a2, a3, w_ref, o_ref):
    acc = x_ref[...]
    for i, a in enumerate((a0, a1, a2, a3)):
        acc = acc + _dot(a[...], w_ref[i * GW:(i + 1) * GW, :])
    o_ref[...] = acc


def _matmul_out(x, parts, w, *, tm, tn):
    M, D = x.shape
    a_spec = pl.BlockSpec((tm, GW), lambda i, j: (i, 0))
    return pl.pallas_call(
        _mm_out_kernel,
        grid=(M // tm, D // tn),
        in_specs=[pl.BlockSpec((tm, tn), lambda i, j: (i, j)), a_spec, a_spec, a_spec, a_spec,
                  pl.BlockSpec((N_MIXERS * GW, tn), lambda i, j: (0, j))],
        out_specs=pl.BlockSpec((tm, tn), lambda i, j: (i, j)),
        out_shape=jax.ShapeDtypeStruct((M, D), F32),
        compiler_params=_cparams(("parallel", "parallel")),
        name="mm_out",
    )(x, *parts, w)


def _mm_down_kernel(x_ref, a_ref, w_ref, o_ref):
    @pl.when(pl.program_id(2) == 0)
    def _():
        o_ref[...] = x_ref[...]

    o_ref[...] += _dot(a_ref[...], w_ref[...])


def _matmul_down(x, a, w, *, tm, tn, tk):
    M, K = a.shape
    N = w.shape[1]
    return pl.pallas_call(
        _mm_down_kernel,
        grid=(M // tm, N // tn, K // tk),
        in_specs=[pl.BlockSpec((tm, tn), lambda i, j, k: (i, j)), pl.BlockSpec((tm, tk), lambda i, j, k: (i, k)),
                  pl.BlockSpec((tk, tn), lambda i, j, k: (k, j))],
        out_specs=pl.BlockSpec((tm, tn), lambda i, j, k: (i, j)),
        out_shape=jax.ShapeDtypeStruct((M, N), F32),
        compiler_params=_cparams(("parallel", "parallel", "arbitrary")),
        name="mm_down",
    )(x, a, w)


def _log_sigmoid(x):
    return jnp.minimum(x, 0.0) - jnp.log1p(jnp.exp(-jnp.abs(x)))


def _head_rms(x, g):
    return x * lax.rsqrt(jnp.mean(x * x, axis=-1, keepdims=True) + RMS_EPS) * g


def _prep_kernel(p_ref, gf_ref, gm_ref, gn_ref, fb_ref, *refs, tm, prompt, tiles_per_seq):
    if prompt:
        (fox_ref, moba_ref, nsa_ref, win_ref, sm_ref, hm_ref, cmpin_ref, cc_ref, crow_ref, km_ref, carry_ref) = refs
    else:
        (fox_ref, moba_ref, nsa_ref, win_ref, sm_ref, hm_ref) = refs

    def seg(name, h):
        c0 = _SEG[name] + h * HEAD_DIM
        return p_ref[:, c0:c0 + HEAD_DIM]

    gfq, gfk = gf_ref[0:1, :], gf_ref[1:2, :]
    gmq, gmk = gm_ref[0:1, :], gm_ref[1:2, :]
    gnq, gnks, gnkw = gn_ref[0:1, :], gn_ref[2:3, :], gn_ref[3:4, :]

    for h in range(8):
        hm_ref[_SLOT["fq"] + h] = (_head_rms(seg("fq", h), gfq) * SCALE).astype(BF16)
        k = _head_rms(seg("fk", h), gfk)
        fox_ref[:, h * 128:(h + 1) * 128] = k
        hm_ref[_SLOT["fk"] + h] = k.astype(BF16)
        v = seg("fv", h)
        fox_ref[:, GW + h * 128:GW + (h + 1) * 128] = v
        hm_ref[_SLOT["fv"] + h] = v.astype(BF16)
        hm_ref[_SLOT["mq"] + h] = (_head_rms(seg("mq", h), gmq) * SCALE).astype(BF16)
        hm_ref[_SLOT["nq"] + h] = (_head_rms(seg("nq", h), gnq) * SCALE).astype(BF16)
    for h in range(4):
        k = _head_rms(seg("mk", h), gmk)
        moba_ref[:, h * 128:(h + 1) * 128] = k
        hm_ref[_SLOT["mk"] + h] = k.astype(BF16)
        if prompt:
            km_ref[h:h + 1, :] = jnp.mean(k, axis=0, keepdims=True)
        v = seg("mv", h)
        moba_ref[:, 512 + h * 128:512 + (h + 1) * 128] = v
        hm_ref[_SLOT["mv"] + h] = v.astype(BF16)
    for g in range(2):
        kc, vc = seg("nkc", g), seg("nvc", g)
        nsa_ref[:, g * 128:(g + 1) * 128] = kc
        nsa_ref[:, 256 + g * 128:256 + (g + 1) * 128] = vc
        if prompt:
            cmpin_ref[g] = kc
            cmpin_ref[2 + g] = vc
        ks = _head_rms(seg("nks", g), gnks)
        nsa_ref[:, 512 + g * 128:512 + (g + 1) * 128] = ks
        hm_ref[_SLOT["nks"] + g] = ks.astype(BF16)
        vs = seg("nvs", g)
        nsa_ref[:, 768 + g * 128:768 + (g + 1) * 128] = vs
        hm_ref[_SLOT["nvs"] + g] = vs.astype(BF16)
        kw = _head_rms(seg("nkw", g), gnkw)
        win_ref[:, g * 128:(g + 1) * 128] = kw
        hm_ref[_SLOT["nkw"] + g] = kw.astype(BF16)
        vw = seg("nvw", g)
        win_ref[:, 256 + g * 128:256 + (g + 1) * 128] = vw
        hm_ref[_SLOT["nvw"] + g] = vw.astype(BF16)

    sblk = p_ref[:, _SEG["small"]:_SEG["small"] + LANES]
    lane = _iota(sblk.shape, 1)
    logf = _log_sigmoid(sblk + fb_ref[...])
    gates = 1.0 / (1.0 + jnp.exp(-sblk))
    logf = jnp.where(lane < 8, logf, 0.0)
    sm_ref[...] = jnp.where(lane < 8, logf, jnp.where(lane < 32, gates, 0.0))

    if prompt:
        t = pl.program_id(0) % tiles_per_seq

        @pl.when(t == 0)
        def _():
            carry_ref[...] = jnp.zeros_like(carry_ref)

        tri = (_iota((tm, tm), 1) <= _iota((tm, tm), 0)).astype(BF16)
        c = _dot3_left(tri, logf) + carry_ref[...]
        cc_ref[...] = c
        carry_ref[...] = c[tm - 1:tm, :]
        crow_ref[...] = c.T[0:8, :]


def _dot3_left(m, x):
    hi, mid, lo = _split3(x)
    return _dot(m, hi) + _dot(m, mid) + _dot(m, lo)


def _prep(proj, gf, gm, gn, fbias, *, B, T, prompt):
    M = B * T
    tm = 256 if prompt else M
    nt = M // tm
    row = lambda w: pl.BlockSpec((tm, w), lambda i: (i, 0))
    full = lambda a: pl.BlockSpec(a.shape, lambda i: (0,) * a.ndim)
    fb = jnp.zeros((1, LANES), F32).at[0, :8].set(fbias)
    tiles_per_seq = T // tm if prompt else 1
    out_shape = [jax.ShapeDtypeStruct((M, 2 * GW), F32), jax.ShapeDtypeStruct((M, GW), F32),
                 jax.ShapeDtypeStruct((M, GW), F32), jax.ShapeDtypeStruct((M, 512), F32),
                 jax.ShapeDtypeStruct((M, LANES), F32), jax.ShapeDtypeStruct((B, NSLOT, T, HEAD_DIM), BF16)]
    hm_map = (lambda i: (i // tiles_per_seq, 0, i % tiles_per_seq, 0))
    out_specs = [row(2 * GW), row(GW), row(GW), row(512), row(LANES),
                 pl.BlockSpec((None, NSLOT, tm, HEAD_DIM), hm_map)]
    scratch = []
    if prompt:
        out_shape += [jax.ShapeDtypeStruct((B, 4, T, HEAD_DIM), F32), jax.ShapeDtypeStruct((M, LANES), F32),
                      jax.ShapeDtypeStruct((B, T // tm, 8, tm), F32),
                      jax.ShapeDtypeStruct((B, T // MOBA_BLOCK, 4, HEAD_DIM), F32)]
        out_specs += [pl.BlockSpec((None, 4, tm, HEAD_DIM), hm_map), row(LANES),
                      pl.BlockSpec((None, None, 8, tm), lambda i: (i // tiles_per_seq, i % tiles_per_seq, 0, 0)),
                      pl.BlockSpec((None, None, 4, HEAD_DIM), lambda i: (i // tiles_per_seq, i % tiles_per_seq, 0, 0))]
        scratch = [pltpu.VMEM((1, LANES), F32)]
    return pl.pallas_call(
        functools.partial(_prep_kernel, tm=tm, prompt=prompt, tiles_per_seq=tiles_per_seq),
        grid=(nt,),
        in_specs=[row(PW), full(gf), full(gm), full(gn), full(fb)],
        out_specs=out_specs,
        out_shape=out_shape,
        scratch_shapes=scratch,
        compiler_params=_cparams(("arbitrary",)),
        name="prep_prompt" if prompt else "prep_sample",
    )(proj, gf, gm, gn, fb)


def _lane_pick(x, lane_idx):
    return jnp.sum(jnp.where(_iota(x.shape, 1) == lane_idx, x, 0.0), axis=1, keepdims=True)


def _online_update(carry, s, v):
    m, l, acc = carry
    m_new = jnp.maximum(m, jnp.max(s, axis=1, keepdims=True))
    alpha = jnp.exp(m - m_new)
    p = jnp.exp(s - m_new)
    l = alpha * l + jnp.sum(p, axis=1, keepdims=True)
    acc = alpha * acc + _dot(p.astype(BF16), v)
    return m_new, l, acc


def _online_init(tq):
    return (jnp.full((tq, 1), NEG, F32), jnp.zeros((tq, 1), F32), jnp.zeros((tq, HEAD_DIM), F32))


def _chunk(ref, j, tk):
    return ref[pl.ds(pl.multiple_of(j * tk, tk), tk), :]


def _fox_kernel(q_ref, k_ref, v_ref, cc_ref, crow_ref, o_ref, *, tq):
    h, qi = pl.program_id(1), pl.program_id(2)
    q = q_ref[...]
    cq = _lane_pick(cc_ref[...], h)
    pos = qi * tq + _iota((tq, 1), 0)

    def scores(j):
        return _dot_nt(q, _chunk(k_ref, j, tq)) + (cq - crow_ref[j, pl.ds(h, 1), :])

    carry = lax.fori_loop(0, qi, lambda j, c: _online_update(c, scores(j), _chunk(v_ref, j, tq)), _online_init(tq))
    s = jnp.where(qi * tq + _iota((1, tq), 1) <= pos, scores(qi), NEG)
    _, l, acc = _online_update(carry, s, _chunk(v_ref, qi, tq))
    o_ref[...] = (acc / jnp.maximum(l, 1e-30)).astype(o_ref.dtype)


def _fox_prompt(hm, cc, crow, *, B, T):
    tq = 256
    nq = T // tq
    slot = lambda s0: pl.BlockSpec((None, None, T, HEAD_DIM), lambda b, h, q: (b, s0 + h, 0, 0))
    return pl.pallas_call(
        functools.partial(_fox_kernel, tq=tq),
        grid=(B, 8, nq),
        in_specs=[pl.BlockSpec((None, None, tq, HEAD_DIM), lambda b, h, q: (b, _SLOT["fq"] + h, q, 0)),
                  slot(_SLOT["fk"]), slot(_SLOT["fv"]),
                  pl.BlockSpec((tq, LANES), lambda b, h, q: (b * nq + q, 0)),
                  pl.BlockSpec((None, nq, 8, tq), lambda b, h, q: (b, 0, 0, 0))],
        out_specs=pl.BlockSpec((tq, HEAD_DIM), lambda b, h, q: (b * nq + q, h)),
        out_shape=jax.ShapeDtypeStruct((B * T, GW), BF16),
        compiler_params=_cparams(("parallel", "parallel", "arbitrary")),
        name="fox_prompt",
    )(hm, hm, hm, cc, crow)


def _rank_select(score, n_cand, own, n_keep):
    lane = _iota(score.shape, 1)
    rank = jnp.zeros(score.shape, jnp.int32)
    for j in range(n_cand):
        sj = score[:, j:j + 1]
        beats = (sj > score) | ((sj == score) & (j < lane))
        rank = rank + jnp.where(beats & (j < own), 1, 0)
    return ((lane < own) & (rank < n_keep)) | (lane == own)


def _moba_kernel(q_ref, k_ref, v_ref, km_ref, o_ref, *, tq):
    qi = pl.program_id(2)
    T = k_ref.shape[0]
    n_blk = T // MOBA_BLOCK
    q = q_ref[...]
    gate = _dot_nt(q, km_ref[...].astype(BF16))
    pos = qi * tq + _iota((tq, 1), 0)
    own = pos // MOBA_BLOCK
    sel = jnp.where(_rank_select(gate, n_blk, own, MOBA_TOPK), 1.0, 0.0)

    def step(j, c):
        keep = _lane_pick(sel, j) > 0.5
        s = jnp.where(keep, _dot_nt(q, _chunk(k_ref, j, tq)), NEG)
        return _online_update(c, s, _chunk(v_ref, j, tq))

    carry = lax.fori_loop(0, qi, step, _online_init(tq))
    s = jnp.where(qi * tq + _iota((1, tq), 1) <= pos, _dot_nt(q, _chunk(k_ref, qi, tq)), NEG)
    _, l, acc = _online_update(carry, s, _chunk(v_ref, qi, tq))
    o_ref[...] = (acc / jnp.maximum(l, 1e-30)).astype(o_ref.dtype)


def _moba_prompt(hm, kmean, *, B, T):
    tq = MOBA_BLOCK
    nq = T // tq
    n_blk = T // MOBA_BLOCK
    slot = lambda s0: pl.BlockSpec((None, None, T, HEAD_DIM), lambda b, h, q: (b, s0 + h // 2, 0, 0))
    return pl.pallas_call(
        functools.partial(_moba_kernel, tq=tq),
        grid=(B, 8, nq),
        in_specs=[pl.BlockSpec((None, None, tq, HEAD_DIM), lambda b, h, q: (b, _SLOT["mq"] + h, q, 0)),
                  slot(_SLOT["mk"]), slot(_SLOT["mv"]),
                  pl.BlockSpec((None, None, n_blk, HEAD_DIM), lambda b, h, q: (b, h // 2, 0, 0))],
        out_specs=pl.BlockSpec((tq, HEAD_DIM), lambda b, h, q: (b * nq + q, h)),
        out_shape=jax.ShapeDtypeStruct((B * T, GW), BF16),
        compiler_params=_cparams(("parallel", "parallel", "arbitrary")),
        name="moba_prompt",
    )(hm, hm, hm, kmean)


def _sel_matrix(n_cmp_rows, n_cols, n_c):
    n = _iota((n_cmp_rows, n_cols), 0)
    j = _iota((n_cmp_rows, n_cols), 1)
    ratio = NSA_SEL_BLOCK // NSA_CMP_STRIDE
    lo = jnp.clip(ratio * j - NSA_CMP_BLOCK // NSA_CMP_STRIDE + 1, 0, n_c)
    hi = jnp.clip(ratio * (j + 1), 0, n_c)
    return ((n >= lo) & (n < hi)).astype(BF16)


def _nsa_kernel(q_ref, kc_ref, vc_ref, ks_ref, vs_ref, kw_ref, vw_ref, sm_ref, o_ref, *, tq):
    g, qi = pl.program_id(1), pl.program_id(2)
    T = ks_ref.shape[0]
    NC = kc_ref.shape[0]
    n_c = NC - 1
    n_sel = T // NSA_SEL_BLOCK
    R = q_ref.shape[0]
    pos = qi * tq + _iota((tq, 1), 0)

    kc = kc_ref[...].astype(BF16)
    vc = vc_ref[...].astype(BF16)
    ncol = _iota((1, NC), 1)
    mask_c = (ncol * NSA_CMP_STRIDE + (NSA_CMP_BLOCK - 1) <= pos) & (ncol < n_c)
    o_cmp = []
    psum = jnp.zeros((tq, NC), F32)
    for r in range(R):
        e, l = _softmax_unnorm(_dot_nt(q_ref[r], kc), mask_c)
        p = e / l
        psum = psum + p
        o_cmp.append(_dot(p.astype(BF16), vc))
    imp = _dot3(psum, _sel_matrix(NC, LANES, n_c))
    own = pos // NSA_SEL_BLOCK
    sel = jnp.where(_rank_select(imp, n_sel, own, NSA_SEL_COUNT - 1), 1.0, 0.0).astype(BF16)
    kcol = _iota((1, tq), 1)

    def multi_update(carries, k, v, mask):
        out = []
        for r in range(R):
            out.append(_online_update(carries[r], jnp.where(mask, _dot_nt(q_ref[r], k), NEG), v))
        return tuple(out)

    def finish(carries, k, v, mask):
        res = []
        for m, l, acc in multi_update(carries, k, v, mask):
            res.append(acc / jnp.maximum(l, 1e-30))
        return res

    init = tuple(_online_init(tq) for _ in range(R))

    def sel_mask(j):
        blk_of_key = (j * tq + _iota((LANES, tq), 1)) // NSA_SEL_BLOCK
        expand = (blk_of_key == _iota((LANES, tq), 0)).astype(BF16)
        return _dot(sel, expand) > 0.5

    def sel_step(j, c):
        return multi_update(c, _chunk(ks_ref, j, tq), _chunk(vs_ref, j, tq), sel_mask(j))

    c_sel = lax.fori_loop(0, qi, sel_step, init)
    causal = qi * tq + kcol <= pos
    o_sel = finish(c_sel, _chunk(ks_ref, qi, tq), _chunk(vs_ref, qi, tq), sel_mask(qi) & causal)

    def win_step(j, c):
        diff = pos - (j * tq + kcol)
        return multi_update(c, _chunk(kw_ref, j, tq), _chunk(vw_ref, j, tq), diff < NSA_WINDOW)

    first = jnp.maximum(qi - (NSA_WINDOW + tq - 1) // tq, 0)
    c_win = lax.fori_loop(first, qi, win_step, init)
    o_win = finish(c_win, _chunk(kw_ref, qi, tq), _chunk(vw_ref, qi, tq), causal)

    gates = sm_ref[...]
    for r in range(R):
        hh = g * R + r
        g0 = _lane_pick(gates, GATE_LANE0 + hh)
        g1 = _lane_pick(gates, GATE_LANE0 + 8 + hh)
        g2 = _lane_pick(gates, GATE_LANE0 + 16 + hh)
        o = g0 * o_cmp[r] + g1 * o_sel[r] + g2 * o_win[r]
        o_ref[:, r * HEAD_DIM:(r + 1) * HEAD_DIM] = o.astype(o_ref.dtype)


def _nsa_prompt(hm, cmp_kv, sm, *, B, T):
    tq = 256
    nq = T // tq
    G = NSA_KV_HEADS
    R = 8 // G
    NC = cmp_kv.shape[2]
    slot = lambda s0: pl.BlockSpec((None, None, T, HEAD_DIM), lambda b, g, q: (b, s0 + g, 0, 0))
    return pl.pallas_call(
        functools.partial(_nsa_kernel, tq=tq),
        grid=(B, G, nq),
        in_specs=[pl.BlockSpec((None, R, tq, HEAD_DIM), lambda b, g, q: (b, _SLOT["nq"] // R + g, q, 0)),
                  pl.BlockSpec((None, None, NC, HEAD_DIM), lambda b, g, q: (b, g, 0, 0)),
                  pl.BlockSpec((None, None, NC, HEAD_DIM), lambda b, g, q: (b, G + g, 0, 0)),
                  slot(_SLOT["nks"]), slot(_SLOT["nvs"]), slot(_SLOT["nkw"]), slot(_SLOT["nvw"]),
                  pl.BlockSpec((tq, LANES), lambda b, g, q: (b * nq + q, 0))],
        out_specs=pl.BlockSpec((tq, R * HEAD_DIM), lambda b, g, q: (b * nq + q, g)),
        out_shape=jax.ShapeDtypeStruct((B * T, GW), BF16),
        compiler_params=_cparams(("parallel", "parallel", "arbitrary")),
        name="nsa_prompt",
    )(hm, cmp_kv, cmp_kv, hm, hm, hm, hm, sm)


def _compress_kernel(c_ref, pe_ref, w1_ref, w2_ref, g_ref, o_ref):
    c = pl.program_id(1)
    x = c_ref[...]
    NC = x.shape[0]
    a = _dot((x + pe_ref[0]).astype(BF16), w1_ref[0])
    bm = _dot((x + pe_ref[1]).astype(BF16), w1_ref[1])
    pre = a + pltpu.roll(bm, NC - 1, 0)
    hid = pre * (1.0 / (1.0 + jnp.exp(-pre)))
    out = _dot(hid.astype(BF16), w2_ref[...])
    out = jnp.where(c < NSA_KV_HEADS, _head_rms(out, g_ref[...]), out)
    o_ref[...] = jnp.where(_iota(out.shape, 0) < NC - 1, out, 0.0)


def _compress(x, pe, w1, w2, gain):
    B, C, NC, W = x.shape
    G = NSA_KV_HEADS
    return pl.pallas_call(
        _compress_kernel,
        grid=(B, C),
        in_specs=[pl.BlockSpec((None, None, NC, W), lambda b, c: (b, c, 0, 0)),
                  pl.BlockSpec((None, 2, 1, W), lambda b, c: (c // G, 0, 0, 0)),
                  pl.BlockSpec((None, 2, W, HEAD_DIM), lambda b, c: (c // G, 0, 0, 0)),
                  pl.BlockSpec((None, HEAD_DIM, HEAD_DIM), lambda b, c: (c // G, 0, 0)),
                  pl.BlockSpec((1, HEAD_DIM), lambda b, c: (0, 0))],
        out_specs=pl.BlockSpec((None, None, NC, HEAD_DIM), lambda b, c: (b, c, 0, 0)),
        out_shape=jax.ShapeDtypeStruct((B, C, NC, HEAD_DIM), F32),
        compiler_params=_cparams(("parallel", "parallel")),
        name="nsa_compress",
    )(x, pe, w1, w2, gain)


def _pool_kernel(u_ref, halo_ref, w_ref, sc_ref, o_ref, ext_ref, *, tp):
    t = pl.program_id(1)
    HALO = halo_ref.shape[0]
    halo = halo_ref[...]
    ext_ref[0:HALO, :] = jnp.where(t > 0, halo, 0.0)
    ext_ref[HALO:HALO + tp, :] = u_ref[...]
    pos = t * tp + _iota((tp, 1), 0)
    PG = GW // len(POOL_WINDOWS)
    for gi, w in enumerate(POOL_WINDOWS):
        c0 = gi * PG
        u = ext_ref[HALO:HALO + tp, c0:c0 + PG]
        win = u
        for k in range(1, w):
            win = win + ext_ref[HALO - k:HALO - k + tp, c0:c0 + PG]
        cnt = jnp.minimum(w, pos + 1).astype(F32)
        d = win / cnt - u
        o = _dot(d.astype(BF16), w_ref[gi]) * sc_ref[:, c0:c0 + PG]
        o_ref[:, c0:c0 + PG] = o.astype(o_ref.dtype)


def _pool_prompt(proj, pool_w, pool_scale, *, B, T):
    tp = 512
    HALO = 16
    nt = T // tp
    cb = _SEG["pu"] // GW
    return pl.pallas_call(
        functools.partial(_pool_kernel, tp=tp),
        grid=(B, nt),
        in_specs=[pl.BlockSpec((tp, GW), lambda b, t: (b * nt + t, cb)),
                  pl.BlockSpec((HALO, GW), lambda b, t: (jnp.maximum((b * nt + t) * (tp // HALO) - 1, 0), cb)),
                  pl.BlockSpec(pool_w.shape, lambda b, t: (0, 0, 0)),
                  pl.BlockSpec((1, GW), lambda b, t: (0, 0))],
        out_specs=pl.BlockSpec((tp, GW), lambda b, t: (b * nt + t, 0)),
        out_shape=jax.ShapeDtypeStruct((B * T, GW), BF16),
        scratch_shapes=[pltpu.VMEM((HALO + tp, GW), F32)],
        compiler_params=_cparams(("parallel", "arbitrary")),
        name="pool_prompt",
    )(proj, proj, pool_w, pool_scale)


def _diag_col(row, lane0, n):
    b = jnp.broadcast_to(row, (n, row.shape[1]))
    keep = _iota(b.shape, 1) == _iota(b.shape, 0) + lane0
    return jnp.sum(jnp.where(keep, b, 0.0), axis=1, keepdims=True)


def _pad_rows(x, n):
    return jnp.concatenate([x, jnp.zeros((n - x.shape[0],) + x.shape[1:], x.dtype)], axis=0)


def _sfox_kernel(pt_ref, kv_ref, lf_ref, later_ref, q_ref, new_ref, sm_ref, o_ref, m_ref, l_ref, acc_ref, carry_ref):
    b, i = pl.program_id(0), pl.program_id(1)
    H = 8
    P = kv_ref.shape[0]
    q = q_ref[...]

    @pl.when(i == 0)
    def _():
        knew, vnew = new_ref[0:H, :], new_ref[H:2 * H, :]
        s_new = jnp.sum(q * knew, axis=1, keepdims=True)
        m_ref[...] = jnp.broadcast_to(s_new, m_ref.shape)
        l_ref[...] = jnp.ones_like(l_ref)
        acc_ref[...] = vnew
        carry_ref[...] = jnp.broadcast_to(_diag_col(sm_ref[pl.ds(b, 1), :], 0, H), carry_ref.shape)

    k = kv_ref[:, 0]
    v = kv_ref[:, 1]
    lf = lf_ref[...]
    z = k * q[None] + later_ref[...] * lf[None]
    ones = jnp.ones((HEAD_DIM, HEAD_DIM), BF16)
    z2 = z.reshape(P * H, HEAD_DIM)
    hi = z2.astype(BF16)
    mid = (z2 - hi.astype(F32)).astype(BF16)
    s = (_dot(hi, ones) + _dot(mid, ones)).reshape(P, H, HEAD_DIM) + carry_ref[...][None]
    m_old = m_ref[...]
    m_new = jnp.maximum(m_old, jnp.max(s, axis=0))
    alpha = jnp.exp(m_old - m_new)
    p = jnp.exp(s - m_new[None])
    l_ref[...] = alpha * l_ref[...] + jnp.sum(p, axis=0)
    acc_ref[...] = alpha * acc_ref[...] + jnp.sum(p * v, axis=0)
    m_ref[...] = m_new
    carry_ref[...] = carry_ref[...] + jnp.sum(lf, axis=1, keepdims=True)

    @pl.when(i == pl.num_programs(1) - 1)
    def _():
        o_ref[...] = acc_ref[...] / jnp.maximum(l_ref[...], 1e-30)


def _fox_sample(layer, page_table, cache_kv, cache_lfT, hm_s, fox_new, sm_s, *, Bs):
    NP = page_table.shape[1]
    PS = cache_kv.shape[2]
    H = 8
    r = jnp.arange(PS, dtype=jnp.int32)
    later = jnp.broadcast_to((r[None, :] > r[:, None]).astype(F32)[:, None, :], (PS, H, PS))
    rep = pltpu.VMEM((H, HEAD_DIM), F32)
    grid_spec = pltpu.PrefetchScalarGridSpec(
        num_scalar_prefetch=1,
        grid=(Bs, NP),
        in_specs=[pl.BlockSpec((None, None, PS, 2, H, HEAD_DIM), lambda b, i, pt: (layer, pt[b, NP - 1 - i], 0, 0, 0, 0)),
                  pl.BlockSpec((None, None, H, PS), lambda b, i, pt: (layer, pt[b, NP - 1 - i], 0, 0)),
                  pl.BlockSpec((PS, H, PS), lambda b, i, pt: (0, 0, 0)),
                  pl.BlockSpec((None, None, H, HEAD_DIM), lambda b, i, pt: (b, _SLOT["fq"] // H, 0, 0)),
                  pl.BlockSpec((None, 2 * H, HEAD_DIM), lambda b, i, pt: (b, 0, 0)),
                  pl.BlockSpec(sm_s.shape, lambda b, i, pt: (0, 0))],
        out_specs=pl.BlockSpec((None, H, HEAD_DIM), lambda b, i, pt: (b, 0, 0)),
        scratch_shapes=[rep, rep, rep, rep],
    )
    assert PS == HEAD_DIM
    return pl.pallas_call(
        _sfox_kernel,
        grid_spec=grid_spec,
        out_shape=jax.ShapeDtypeStruct((Bs, H, HEAD_DIM), F32),
        compiler_params=_cparams(("parallel", "arbitrary")),
        name="fox_sample",
    )(page_table, cache_kv, cache_lfT, later, hm_s.reshape(hm_s.shape[0], NSLOT // H, H, HEAD_DIM), fox_new, sm_s)


def _argmax_rounds(score, n_rounds, out_lanes):
    rows, L = score.shape
    lane = _iota(score.shape, 1)
    olane = _iota((rows, out_lanes), 1)
    out = jnp.zeros((rows, out_lanes), jnp.int32)
    for t in range(n_rounds):
        m = jnp.max(score, axis=1, keepdims=True)
        idx = jnp.min(jnp.where(score == m, lane, L), axis=1, keepdims=True)
        ok = jnp.where(m > NEG, 1, 0)
        out = jnp.where(olane == t, idx, out)
        out = jnp.where(olane == n_rounds + t, ok, out)
        score = jnp.where(lane == idx, NEG, score)
    return out


def _smoba_gate_kernel(pt_ref, *refs, ppb, bps):
    k_refs = refs[:bps * ppb]
    q_ref, o_ref, g_ref = refs[bps * ppb:]
    i = pl.program_id(1)
    H = 8

    @pl.when(i == 0)
    def _():
        g_ref[...] = jnp.full_like(g_ref, NEG)

    q = q_ref[...]
    g = g_ref[...]
    for jb in range(bps):
        ksum = jnp.sum(k_refs[jb * ppb][...], axis=0)
        for t in range(1, ppb):
            ksum = ksum + jnp.sum(k_refs[jb * ppb + t][...], axis=0)
        kmean = ksum / float(MOBA_BLOCK)
        kme = jnp.concatenate([kmean[h // 2:h // 2 + 1, :] for h in range(H)], axis=0)
        gate = jnp.sum(q * kme, axis=1, keepdims=True)
        g = jnp.where(_iota(g.shape, 1) == i * bps + jb, gate, g)
    g_ref[...] = g

    @pl.when(i == pl.num_programs(1) - 1)
    def _():
        o_ref[...] = _argmax_rounds(g, MOBA_TOPK, LANES)


def _moba_sample_gate(layer, page_table, cache_kv, hm_s, *, Bs):
    NP = page_table.shape[1]
    PS = cache_kv.shape[2]
    nkv = cache_kv.shape[4]
    ppb = MOBA_BLOCK // PS
    NB = NP // ppb
    bps = 4 if NB % 4 == 0 else 1
    assert NP % ppb == 0
    page = lambda j: pl.BlockSpec((None, None, PS, None, nkv, HEAD_DIM),
                                  lambda b, i, pt: (layer, pt[b, bps * ppb * i + j], 0, 0, 0, 0))
    grid_spec = pltpu.PrefetchScalarGridSpec(
        num_scalar_prefetch=1,
        grid=(Bs, NB // bps),
        in_specs=[page(j) for j in range(bps * ppb)]
        + [pl.BlockSpec((None, None, 8, HEAD_DIM), lambda b, i, pt: (b, _SLOT["mq"] // 8, 0, 0))],
        out_specs=pl.BlockSpec((None, 8, LANES), lambda b, i, pt: (b, 0, 0)),
        scratch_shapes=[pltpu.VMEM((8, max(LANES, NB)), F32)],
    )
    return pl.pallas_call(
        functools.partial(_smoba_gate_kernel, ppb=ppb, bps=bps),
        grid_spec=grid_spec,
        out_shape=jax.ShapeDtypeStruct((Bs, 8, LANES), jnp.int32),
        compiler_params=_cparams(("parallel", "arbitrary")),
        name="moba_sample_gate",
    )(page_table, *([cache_kv] * (bps * ppb)), hm_s.reshape(hm_s.shape[0], NSLOT // 8, 8, HEAD_DIM))


def _smoba_attn_kernel(pg_ref, ok_ref, *refs, n_pages, ppb):
    H = 8
    kv_refs = refs[:H]
    q_ref, new_ref, o_ref, ks_ref, vs_ref = refs[H:]
    b, j = pl.program_id(0), pl.program_id(1)
    for h in range(H):
        ks_ref[h, j] = kv_refs[h][:, 0, h // 2, :]
        vs_ref[h, j] = kv_refs[h][:, 1, h // 2, :]

    @pl.when(j == n_pages - 1)
    def _():
        for h in range(H):
            kvh = h // 2
            q = q_ref[h:h + 1, :]
            knew = new_ref[kvh:kvh + 1, :]
            vnew = new_ref[4 + kvh:5 + kvh, :]
            s_new = jnp.sum(q * knew, axis=1, keepdims=True)
            s, ok = [], []
            for t in range(n_pages):
                s.append(jnp.sum(ks_ref[h, t] * q, axis=1, keepdims=True))
                ok.append(ok_ref[(b * H + h) * (n_pages // ppb) + t // ppb] > 0)
            m = s_new
            for t in range(n_pages):
                m = jnp.maximum(m, jnp.max(jnp.where(ok[t], s[t], NEG), axis=0, keepdims=True))
            l = jnp.exp(s_new - m)
            acc = l * vnew
            for t in range(n_pages):
                e = jnp.where(ok[t], jnp.exp(s[t] - m), 0.0)
                l = l + jnp.sum(e, axis=0, keepdims=True)
                acc = acc + jnp.sum(e * vs_ref[h, t], axis=0, keepdims=True)
            o_ref[h:h + 1, :] = acc / jnp.maximum(l, 1e-30)


def _moba_sample_attn(layer, pages, oks, cache_kv, hm_s, moba_new, *, Bs):
    PS = cache_kv.shape[2]
    nkv = cache_kv.shape[4]
    ppb = MOBA_BLOCK // PS
    n_pages = MOBA_TOPK * ppb
    H = 8
    page = lambda h: pl.BlockSpec((None, None, PS, 2, nkv, HEAD_DIM),
                                  lambda b, j, pg, ok: (layer, pg[(b * H + h) * n_pages + j], 0, 0, 0, 0))
    grid_spec = pltpu.PrefetchScalarGridSpec(
        num_scalar_prefetch=2,
        grid=(Bs, n_pages),
        in_specs=[page(h) for h in range(H)]
        + [pl.BlockSpec((None, None, 8, HEAD_DIM), lambda b, j, pg, ok: (b, _SLOT["mq"] // 8, 0, 0)),
           pl.BlockSpec((None, 8, HEAD_DIM), lambda b, j, pg, ok: (b, 0, 0))],
        out_specs=pl.BlockSpec((None, 8, HEAD_DIM), lambda b, j, pg, ok: (b, 0, 0)),
        scratch_shapes=[pltpu.VMEM((H, n_pages, PS, HEAD_DIM), F32), pltpu.VMEM((H, n_pages, PS, HEAD_DIM), F32)],
    )
    return pl.pallas_call(
        functools.partial(_smoba_attn_kernel, n_pages=n_pages, ppb=ppb),
        grid_spec=grid_spec,
        out_shape=jax.ShapeDtypeStruct((Bs, 8, HEAD_DIM), F32),
        compiler_params=_cparams(("parallel", "arbitrary")),
        name="moba_sample_attn",
    )(pages, oks, *([cache_kv] * H), hm_s.reshape(hm_s.shape[0], NSLOT // 8, 8, HEAD_DIM), moba_new)


def _snsa_gather_kernel(pt_ref, *refs, pps):
    x_refs, o_ref = refs[:pps], refs[pps]
    PS = x_refs[0].shape[0]
    G = x_refs[0].shape[2]
    S = NSA_CMP_STRIDE
    cpp = PS // S
    for k in range(pps):
        for j in range(2):
            for g in range(G):
                for r in range(S):
                    o_ref[j * G + g, k * cpp:(k + 1) * cpp, r * HEAD_DIM:(r + 1) * HEAD_DIM] = (
                        x_refs[k][pl.ds(r, cpp, stride=S), j, g, :])


def _nsa_sample_gather(layer, page_table, cache_kv, *, Bs):
    NP = page_table.shape[1]
    PS = cache_kv.shape[2]
    G = cache_kv.shape[4]
    C = 2 * G
    S = NSA_CMP_STRIDE
    pps = 4 if NP % 4 == 0 else 1
    cpp = PS // S
    page = lambda k: pl.BlockSpec((None, None, PS, 2, G, HEAD_DIM),
                                  lambda b, i, pt: (layer, pt[b, pps * i + k], 0, 0, 0, 0))
    grid_spec = pltpu.PrefetchScalarGridSpec(
        num_scalar_prefetch=1,
        grid=(Bs, NP // pps),
        in_specs=[page(k) for k in range(pps)],
        out_specs=pl.BlockSpec((None, C, pps * cpp, S * HEAD_DIM), lambda b, i, pt: (b, 0, i, 0)),
    )
    return pl.pallas_call(
        functools.partial(_snsa_gather_kernel, pps=pps),
        grid_spec=grid_spec,
        out_shape=jax.ShapeDtypeStruct((Bs, C, NP * cpp, S * HEAD_DIM), F32),
        compiler_params=_cparams(("parallel", "arbitrary")),
        name="nsa_sample_gather",
    )(page_table, *([cache_kv] * pps))


def _snsa_cmp_kernel(q_ref, kc_ref, vc_ref, o_ref, sel_ref, *, n_sel_past):
    R = q_ref.shape[0]
    NC = kc_ref.shape[0]
    n_c = NC - 1
    q = _pad_rows(q_ref[...], 16).astype(BF16)
    s = _dot_nt(q, kc_ref[...].astype(BF16))[0:R, :]
    mask = _iota((1, NC), 1) < n_c
    e, l = _softmax_unnorm(s, mask)
    p = e / l
    o_ref[...] = _dot(_pad_rows(p, 16).astype(BF16), vc_ref[...].astype(BF16))[0:R, :]
    psum = jnp.sum(p, axis=0, keepdims=True)
    imp = _dot3(_pad_rows(psum, 16), _sel_matrix(NC, n_sel_past, n_c))[0:1, :]
    sel_ref[...] = _argmax_rounds(imp, NSA_SEL_COUNT - 1, LANES)


def _nsa_sample_cmp(hm_s, cmp_kv, *, Bs):
    G = NSA_KV_HEADS
    R = 8 // G
    NC = cmp_kv.shape[2]
    n_sel_past = NC * NSA_CMP_STRIDE // NSA_SEL_BLOCK
    return pl.pallas_call(
        functools.partial(_snsa_cmp_kernel, n_sel_past=n_sel_past),
        grid=(Bs, G),
        in_specs=[pl.BlockSpec((None, None, R, HEAD_DIM), lambda b, g: (b, _SLOT["nq"] // R + g, 0, 0)),
                  pl.BlockSpec((None, None, NC, HEAD_DIM), lambda b, g: (b, g, 0, 0)),
                  pl.BlockSpec((None, None, NC, HEAD_DIM), lambda b, g: (b, G + g, 0, 0))],
        out_specs=[pl.BlockSpec((None, None, R, HEAD_DIM), lambda b, g: (b, g, 0, 0)),
                   pl.BlockSpec((None, None, 1, LANES), lambda b, g: (b, g, 0, 0))],
        out_shape=[jax.ShapeDtypeStruct((Bs, G, R, HEAD_DIM), F32), jax.ShapeDtypeStruct((Bs, G, 1, LANES), jnp.int32)],
        compiler_params=_cparams(("parallel", "parallel")),
        name="nsa_sample_cmp",
    )(hm_s.reshape(hm_s.shape[0], NSLOT // R, R, HEAD_DIM), cmp_kv, cmp_kv)


def _snsa_final_kernel(pg_ref, hf_ref, ok_ref, *refs, n_sel):
    G = NSA_KV_HEADS
    blk_refs = refs[:G]
    q_ref, new_ref, win_ref, wnew_ref, ocmp_ref, sm_ref, o_ref, ks_ref, vs_ref = refs[G:]
    b, j = pl.program_id(0), pl.program_id(1)
    for g in range(G):
        ks_ref[g, j] = blk_refs[g][:, 0, g, :]
        vs_ref[g, j] = blk_refs[g][:, 1, g, :]

    @pl.when(j == n_sel - 1)
    def _():
        R = q_ref.shape[0] // G
        gates = sm_ref[pl.ds(b, 1), :]
        for g in range(G):
            qf = q_ref[g * R:(g + 1) * R, :]
            q = _pad_rows(qf, 16).astype(BF16)
            ks_new = new_ref[2 * G + g:2 * G + g + 1, :]
            vs_new = new_ref[3 * G + g:3 * G + g + 1, :]
            s_new = jnp.sum(qf * ks_new, axis=1, keepdims=True)
            s, ok = [], []
            for t in range(n_sel):
                s.append(_dot_nt(q, ks_ref[g, t].astype(BF16))[0:R, :])
                ok.append(ok_ref[(b * G + g) * n_sel + t] > 0)
            m = s_new
            for t in range(n_sel):
                m = jnp.maximum(m, jnp.max(jnp.where(ok[t], s[t], NEG), axis=1, keepdims=True))
            l = jnp.exp(s_new - m)
            acc = l * vs_new
            for t in range(n_sel):
                e = jnp.where(ok[t], jnp.exp(s[t] - m), 0.0)
                l = l + jnp.sum(e, axis=1, keepdims=True)
                acc = acc + _dot(_pad_rows(e, 16).astype(BF16), vs_ref[g, t].astype(BF16))[0:R, :]
            o_sel = acc / jnp.maximum(l, 1e-30)
            WB = win_ref.shape[0]
            kw_new = wnew_ref[g:g + 1, :]
            vw_new = wnew_ref[G + g:G + g + 1, :]
            sw = _dot_nt(q, win_ref[:, 0, g, :].astype(BF16))[0:R, :]
            mask_w = (WB - _iota((1, WB), 1)) < NSA_WINDOW
            sw_new = jnp.sum(qf * kw_new, axis=1, keepdims=True)
            mw = jnp.maximum(sw_new, jnp.max(jnp.where(mask_w, sw, NEG), axis=1, keepdims=True))
            ew = jnp.where(mask_w, jnp.exp(sw - mw), 0.0)
            ew_new = jnp.exp(sw_new - mw)
            lw = ew_new + jnp.sum(ew, axis=1, keepdims=True)
            pv = _dot(_pad_rows(ew, 16).astype(BF16), win_ref[:, 1, g, :].astype(BF16))[0:R, :]
            o_win = (ew_new * vw_new + pv) / jnp.maximum(lw, 1e-30)
            g0 = _diag_col(gates, GATE_LANE0 + g * R, R)
            g1 = _diag_col(gates, GATE_LANE0 + 8 + g * R, R)
            g2 = _diag_col(gates, GATE_LANE0 + 16 + g * R, R)
            o_ref[g * R:(g + 1) * R, :] = g0 * ocmp_ref[g * R:(g + 1) * R, :] + g1 * o_sel + g2 * o_win


def _nsa_sample_final(layer, pages, halves, oks, cache_kv, hm_s, nsa_new, win_state, win_new, o_cmp, sm_s, *, Bs):
    G = NSA_KV_HEADS
    n_sel = NSA_SEL_COUNT - 1
    SB = NSA_SEL_BLOCK
    WB = win_state.shape[2]
    idx = lambda b, g, j: (b * G + g) * n_sel + j
    blk = lambda g: pl.BlockSpec((None, None, SB, 2, G, HEAD_DIM),
                                 lambda b, j, pg, hf, ok: (layer, pg[idx(b, g, j)], hf[idx(b, g, j)], 1, 0, 0))
    grid_spec = pltpu.PrefetchScalarGridSpec(
        num_scalar_prefetch=3,
        grid=(Bs, n_sel),
        in_specs=[blk(g) for g in range(G)]
        + [pl.BlockSpec((None, None, 8, HEAD_DIM), lambda b, j, pg, hf, ok: (b, _SLOT["nq"] // 8, 0, 0)),
           pl.BlockSpec((None, 8, HEAD_DIM), lambda b, j, pg, hf, ok: (b, 0, 0)),
           pl.BlockSpec((None, None, WB, 2, G, HEAD_DIM), lambda b, j, pg, hf, ok: (layer, b, 0, 0, 0, 0)),
           pl.BlockSpec((None, 2 * G, HEAD_DIM), lambda b, j, pg, hf, ok: (b, 0, 0)),
           pl.BlockSpec((None, 8, HEAD_DIM), lambda b, j, pg, hf, ok: (b, 0, 0)),
           pl.BlockSpec(sm_s.shape, lambda b, j, pg, hf, ok: (0, 0))],
        out_specs=pl.BlockSpec((None, 8, HEAD_DIM), lambda b, j, pg, hf, ok: (b, 0, 0)),
        scratch_shapes=[pltpu.VMEM((G, n_sel, SB, HEAD_DIM), F32), pltpu.VMEM((G, n_sel, SB, HEAD_DIM), F32)],
    )
    return pl.pallas_call(
        functools.partial(_snsa_final_kernel, n_sel=n_sel),
        grid_spec=grid_spec,
        out_shape=jax.ShapeDtypeStruct((Bs, 8, HEAD_DIM), F32),
        compiler_params=_cparams(("parallel", "arbitrary")),
        name="nsa_sample_final",
    )(pages, halves, oks, *([cache_kv] * G), hm_s.reshape(hm_s.shape[0], NSLOT // 8, 8, HEAD_DIM), nsa_new, win_state,
      win_new, o_cmp.reshape(Bs, 8, HEAD_DIM), sm_s)


def _spool_kernel(st_ref, u_ref, w_ref, sc_ref, o_ref, *, Bs, pos0):
    u_all = u_ref[...]
    PG = GW // len(POOL_WINDOWS)
    NB = st_ref.shape[1]
    out = []
    for gi, w in enumerate(POOL_WINDOWS):
        c0 = gi * PG
        u = u_all[:, c0:c0 + PG]
        win = u
        for k in range(1, w):
            prev = st_ref[:, NB - k, c0:c0 + PG]
            if prev.shape[0] != u.shape[0]:
                prev = _pad_rows(prev, u.shape[0])
            win = win + prev
        d = win / float(min(w, pos0 + 1)) - u
        out.append(_dot(d.astype(BF16), w_ref[gi]) * sc_ref[:, c0:c0 + PG])
    o_ref[...] = jnp.concatenate(out, axis=1).astype(o_ref.dtype)


def _pool_sample(layer, state_pool, proj_s, pool_w, pool_scale, *, Bs, pos0):
    Ms = proj_s.shape[0]
    NB = state_pool.shape[2]
    assert NB >= max(POOL_WINDOWS) - 1 and pos0 >= NB
    cb = _SEG["pu"] // GW
    return pl.pallas_call(
        functools.partial(_spool_kernel, Bs=Bs, pos0=pos0),
        grid=(1,),
        in_specs=[pl.BlockSpec((None, Bs, NB, GW), lambda i: (layer, 0, 0, 0)),
                  pl.BlockSpec((Ms, GW), lambda i: (0, cb)),
                  pl.BlockSpec(pool_w.shape, lambda i: (0, 0, 0)),
                  pl.BlockSpec((1, GW), lambda i: (0, 0))],
        out_specs=pl.BlockSpec((Ms, GW), lambda i: (0, 0)),
        out_shape=jax.ShapeDtypeStruct((Ms, GW), BF16),
        compiler_params=_cparams(("arbitrary",)),
        name="pool_sample",
    )(state_pool, proj_s, pool_w, pool_scale)


def _layer_weights(l, norm1_g, w_in, fox_f_bias, fox_qk_g, moba_qk_g, nsa_qk_g, pool_w, pool_scale, cmp_pe, cmp_w1,
                   cmp_w2, w_out, norm2_g, w_up, w_down):
    D = w_in.shape[1]
    segs, c = {}, 0
    for name, n in _ORIG:
        segs[name] = w_in[l, :, c:c + n]
        c += n
    cols = [segs[n] for n in _NEW_ORDER]
    used = sum(x.shape[1] for x in cols)
    cols.append(jnp.zeros((D, PW - used), w_in.dtype))
    W = NSA_CMP_STRIDE * HEAD_DIM
    return dict(
        norm1_g=norm1_g[l], norm2_g=norm2_g[l],
        w_in=jnp.concatenate(cols, axis=1).astype(BF16),
        w_out=w_out[l].astype(BF16), w_up=w_up[l].astype(BF16), w_down=w_down[l].astype(BF16),
        fox_f_bias=fox_f_bias[l], fox_qk_g=fox_qk_g[l], moba_qk_g=moba_qk_g[l], nsa_qk_g=nsa_qk_g[l],
        pool_w=pool_w[l].astype(BF16), pool_scale=pool_scale[l].reshape(1, GW),
        cmp_pe=cmp_pe[l].reshape(2, 2, 1, W), cmp_w1=cmp_w1[l].reshape(2, 2, W, HEAD_DIM).astype(BF16),
        cmp_w2=cmp_w2[l].astype(BF16), kc_gain=nsa_qk_g[l][1:2],
    )


def _mlp(x1, lw, *, tm):
    h2 = _rms(x1, lw["norm2_g"], tm=min(tm, 512))
    u = _matmul(h2, lw["w_up"], tm=tm, tn=512 if tm > 16 else 1024, act="relu2", out_dtype=BF16, name="mm_up")
    return _matmul_down(x1, u, lw["w_down"], tm=tm, tn=1024, tk=2048)


def _prompt_layer(x, lw, *, B, T):
    tm = min(1024, B * T)
    h = _rms(x, lw["norm1_g"], tm=min(tm, 512))
    proj = _matmul(h, lw["w_in"], tm=tm, tn=512, name="mm_in")
    fox_kv, moba_kv, nsa_kv, win, sm, hm, cmp_in, cc, crow, kmean = _prep(
        proj, lw["fox_qk_g"], lw["moba_qk_g"], lw["nsa_qk_g"], lw["fox_f_bias"], B=B, T=T, prompt=True)
    o_fox = _fox_prompt(hm, cc, crow, B=B, T=T)
    o_pool = _pool_prompt(proj, lw["pool_w"], lw["pool_scale"], B=B, T=T)
    o_moba = _moba_prompt(hm, jnp.swapaxes(kmean, 1, 2), B=B, T=T)
    cmp_in = cmp_in.reshape(B, 4, T // NSA_CMP_STRIDE, NSA_CMP_STRIDE * HEAD_DIM)
    cmp_kv = _compress(cmp_in, lw["cmp_pe"], lw["cmp_w1"], lw["cmp_w2"], lw["kc_gain"])
    o_nsa = _nsa_prompt(hm, cmp_kv, sm, B=B, T=T)
    x1 = _matmul_out(x, (o_fox, o_pool, o_moba, o_nsa), lw["w_out"], tm=tm, tn=512)
    y = _mlp(x1, lw, tm=tm)
    wb = min(NSA_WINDOW, T)
    new = (fox_kv.reshape(B, T, 2, 8, HEAD_DIM), sm[:, 0:8].reshape(B, T, 8),
           moba_kv.reshape(B, T, 2, 4, HEAD_DIM), nsa_kv.reshape(B, T, 4, NSA_KV_HEADS, HEAD_DIM),
           win.reshape(B, T, 2, NSA_KV_HEADS, HEAD_DIM)[:, T - wb:],
           proj.reshape(B, T, PW)[:, T - POOL_BUF:, _SEG["pu"]:_SEG["pu"] + GW])
    return y, new


def _sample_layer(x, l, lw, caches, page_table, *, Bs):
    cache_fox_kv, cache_fox_lfT, cache_moba_kv, cache_nsa_kv, state_nsa_win, state_pool = caches
    Ms = x.shape[0]
    NP = page_table.shape[1]
    PS = cache_fox_kv.shape[2]
    past = NP * PS
    G = NSA_KV_HEADS
    h = _rms(x, lw["norm1_g"], tm=Ms)
    proj = _matmul(h, lw["w_in"], tm=Ms, tn=1024, name="mm_in_s")
    fox_kv, moba_kv, nsa_kv, win, sm, hm = _prep(
        proj, lw["fox_qk_g"], lw["moba_qk_g"], lw["nsa_qk_g"], lw["fox_f_bias"], B=1, T=Ms, prompt=False)
    hm_s = jnp.swapaxes(hm[0], 0, 1).astype(F32)
    fox_new = fox_kv.reshape(Ms, 16, HEAD_DIM)
    moba_new = moba_kv.reshape(Ms, 8, HEAD_DIM)
    nsa_new = nsa_kv.reshape(Ms, 8, HEAD_DIM)
    win_new = win.reshape(Ms, 4, HEAD_DIM)

    o_fox = _fox_sample(l, page_table, cache_fox_kv, cache_fox_lfT, hm_s, fox_new, sm, Bs=Bs)

    o_pool = _pool_sample(l, state_pool, proj, lw["pool_w"], lw["pool_scale"], Bs=Bs, pos0=past)

    top = _moba_sample_gate(l, page_table, cache_moba_kv, hm_s, Bs=Bs)
    ppb = MOBA_BLOCK // PS
    blk = top[:, :, 0:MOBA_TOPK]
    oks = top[:, :, MOBA_TOPK:2 * MOBA_TOPK]
    blk = jnp.where(oks > 0, blk, 0)
    pidx = (blk[..., None] * ppb + jnp.arange(ppb, dtype=jnp.int32)).reshape(Bs, 8 * MOBA_TOPK * ppb)
    pages = jnp.take_along_axis(page_table, pidx, axis=1).reshape(-1)
    o_moba = _moba_sample_attn(l, pages, oks.reshape(-1), cache_moba_kv, hm_s, moba_new, Bs=Bs)

    cmp_in = _nsa_sample_gather(l, page_table, cache_nsa_kv, Bs=Bs)
    cmp_kv = _compress(cmp_in, lw["cmp_pe"], lw["cmp_w1"], lw["cmp_w2"], lw["kc_gain"])
    o_cmp, sel = _nsa_sample_cmp(hm_s, cmp_kv, Bs=Bs)
    n_sel = NSA_SEL_COUNT - 1
    sblk = sel[:, :, 0, 0:n_sel]
    soks = sel[:, :, 0, n_sel:2 * n_sel]
    sblk = jnp.where(soks > 0, sblk, 0)
    spp = PS // NSA_SEL_BLOCK
    spages = jnp.take_along_axis(page_table, (sblk // spp).reshape(Bs, G * n_sel), axis=1).reshape(-1)
    o_nsa = _nsa_sample_final(l, spages, (sblk % spp).reshape(-1), soks.reshape(-1), cache_nsa_kv, hm_s, nsa_new,
                              state_nsa_win, win_new, o_cmp, sm, Bs=Bs)

    def rows(o):
        o = o.reshape(Bs, GW).astype(BF16)
        return jnp.concatenate([o, jnp.zeros((Ms - Bs, GW), BF16)], axis=0)

    x1 = _matmul_out(x, (rows(o_fox), o_pool, rows(o_moba), rows(o_nsa)), lw["w_out"], tm=Ms, tn=1024)
    y = _mlp(x1, lw, tm=Ms)
    new = (fox_kv[:Bs].reshape(Bs, 1, 2, 8, HEAD_DIM), sm[:Bs, 0:8].reshape(Bs, 1, 8),
           moba_kv[:Bs].reshape(Bs, 1, 2, 4, HEAD_DIM), nsa_kv[:Bs].reshape(Bs, 1, 4, G, HEAD_DIM),
           win[:Bs].reshape(Bs, 1, 2, G, HEAD_DIM), proj[:Bs, _SEG["pu"]:_SEG["pu"] + GW].reshape(Bs, 1, GW))
    return y, new


def kernel(x_prompt, x_sample, cache_fox_kv, cache_fox_logf, cache_moba_kv, cache_nsa_kv, state_nsa_win, state_pool,
           page_table, norm1_g, w_in, fox_f_bias, fox_qk_g, moba_qk_g, nsa_qk_g, pool_w, pool_scale, cmp_pe, cmp_w1,
           cmp_w2, w_out, norm2_g, w_up, w_down):
    B, T, D = x_prompt.shape
    Bs, Ts, _ = x_sample.shape
    assert Ts == 1 and D == N_MIXERS * GW
    depth = w_in.shape[0]
    n_pool, PS = cache_fox_kv.shape[1:3]
    WB = state_nsa_win.shape[2]
    Ms = 16
    assert Bs <= Ms

    caches = (cache_fox_kv, jnp.swapaxes(cache_fox_logf, 2, 3), cache_moba_kv, cache_nsa_kv, state_nsa_win, state_pool)

    xp = x_prompt.reshape(B * T, D)
    xs = jnp.concatenate([x_sample.reshape(Bs, D), jnp.zeros((Ms - Bs, D), x_sample.dtype)], axis=0)
    new_p, new_s = [], []
    for l in range(depth):
        lw = _layer_weights(l, norm1_g, w_in, fox_f_bias, fox_qk_g, moba_qk_g, nsa_qk_g, pool_w, pool_scale, cmp_pe,
                            cmp_w1, cmp_w2, w_out, norm2_g, w_up, w_down)
        xp, rows_p = _prompt_layer(xp, lw, B=B, T=T)
        xs, rows_s = _sample_layer(xs, l, lw, caches, page_table, Bs=Bs)
        new_p.append(rows_p)
        new_s.append(rows_s)

    def stk(rows, i):
        return jnp.stack([r[i] for r in rows], axis=0)

    win_s = jnp.concatenate([state_nsa_win[:, :, 1:], stk(new_s, 4)], axis=2) if WB > 0 else stk(new_s, 4)[:, :, :0]
    pool_s = jnp.concatenate([state_pool[:, :, 1:], stk(new_s, 5)], axis=2)
    return (xp.reshape(B, T, D), xs[:Bs].reshape(Bs, 1, D),
            stk(new_p, 0), stk(new_s, 0), stk(new_p, 1), stk(new_s, 1), stk(new_p, 2), stk(new_s, 2),
            stk(new_p, 3), stk(new_s, 3), stk(new_p, 4), win_s, stk(new_p, 5), pool_s)
```

```python
import functools

import jax
import jax.numpy as jnp
from jax import lax
from jax.experimental import pallas as pl
from jax.experimental.pallas import tpu as pltpu

F32 = jnp.float32
BF16 = jnp.bfloat16

HEAD_DIM = 128
N_MIXERS = 4
FOX_FORGET_BIAS = 2.0
POOL_WINDOWS = (2, 4, 8, 16)
POOL_BUF = max(POOL_WINDOWS) - 1
MOBA_BLOCK = 256
MOBA_TOPK = 3
NSA_KV_HEADS = 2
NSA_CMP_BLOCK = 32
NSA_CMP_STRIDE = 16
NSA_SEL_BLOCK = 64
NSA_SEL_COUNT = 16
NSA_WINDOW = 512
RMS_EPS = 1e-6
SCALE = HEAD_DIM ** -0.5

LANES = 128
VMEM_LIMIT = 56 * 1024 * 1024
NEG = -1e30

GW = 1024
_SEG = dict(fq=0, fk=1024, fv=2048, pu=3072, mq=4096, mk=5120, mv=5632, nq=6144,
            nkc=7168, nvc=7424, nks=7680, nvs=7936, nkw=8192, nvw=8448, small=8704)
PW = 9216
_ORIG = (("fq", 1024), ("fk", 1024), ("fv", 1024), ("ff", 8), ("pu", 1024), ("mq", 1024), ("mk", 512),
         ("mv", 512), ("nq", 1024), ("nkc", 256), ("nvc", 256), ("nks", 256), ("nvs", 256), ("nkw", 256),
         ("nvw", 256), ("ng", 24))
_NEW_ORDER = ("fq", "fk", "fv", "pu", "mq", "mk", "mv", "nq", "nkc", "nvc", "nks", "nvs", "nkw", "nvw", "ff", "ng")
_SLOT = dict(fq=0, fk=8, fv=16, mq=24, mk=32, mv=36, nq=40, nks=48, nvs=50, nkw=52, nvw=54)
NSLOT = 56
GATE_LANE0 = 8


def _cparams(sem):
    return pltpu.CompilerParams(dimension_semantics=sem, vmem_limit_bytes=VMEM_LIMIT)


def _iota(shape, dim):
    return lax.broadcasted_iota(jnp.int32, shape, dim)


def _dot(a, b):
    return jnp.dot(a, b, preferred_element_type=F32)


def _dot_nt(a, b):
    return lax.dot_general(a, b, (((1,), (1,)), ((), ())), preferred_element_type=F32)


def _split3(x):
    hi = x.astype(BF16)
    r1 = x - hi.astype(F32)
    mid = r1.astype(BF16)
    lo = (r1 - mid.astype(F32)).astype(BF16)
    return hi, mid, lo


def _dot3(x, m):
    hi, mid, lo = _split3(x)
    return _dot(hi, m) + _dot(mid, m) + _dot(lo, m)


def _softmax_unnorm(s, mask):
    sm = jnp.where(mask, s, NEG)
    m = jnp.max(sm, axis=-1, keepdims=True)
    e = jnp.where(mask, jnp.exp(sm - m), 0.0)
    l = jnp.maximum(jnp.sum(e, axis=-1, keepdims=True), 1e-30)
    return e, l


def _rms_kernel(x_ref, g_ref, o_ref):
    x = x_ref[...]
    y = x * lax.rsqrt(jnp.mean(x * x, axis=-1, keepdims=True) + RMS_EPS)
    o_ref[...] = (y * g_ref[...]).astype(o_ref.dtype)


def _rms(x, g, tm):
    M, D = x.shape
    return pl.pallas_call(
        _rms_kernel,
        grid=(M // tm,),
        in_specs=[pl.BlockSpec((tm, D), lambda i: (i, 0)), pl.BlockSpec((1, D), lambda i: (0, 0))],
        out_specs=pl.BlockSpec((tm, D), lambda i: (i, 0)),
        out_shape=jax.ShapeDtypeStruct((M, D), BF16),
        compiler_params=_cparams(("parallel",)),
        name="rms",
    )(x, g.reshape(1, D))


def _cast_kernel(x_ref, o_ref):
    o_ref[...] = x_ref[...].astype(o_ref.dtype)


def _cast_bf16(w_stack, l):
    _, K, N = w_stack.shape
    tk = max(8, min(K, (8 * 1024 * 1024) // (4 * N)))
    assert K % tk == 0
    return pl.pallas_call(
        _cast_kernel,
        grid=(K // tk,),
        in_specs=[pl.BlockSpec((None, tk, N), lambda i: (l, i, 0))],
        out_specs=pl.BlockSpec((tk, N), lambda i: (i, 0)),
        out_shape=jax.ShapeDtypeStruct((K, N), BF16),
        compiler_params=_cparams(("parallel",)),
        name="cast_bf16",
    )(w_stack)


def _mm_kernel(a_ref, w_ref, o_ref, *, act):
    acc = _dot(a_ref[...], w_ref[...])
    if act == "relu2":
        acc = jnp.maximum(acc, 0.0)
        acc = acc * acc
    o_ref[...] = acc.astype(o_ref.dtype)


def _matmul(a, w, *, tm, tn, act=None, out_dtype=F32, name="mm"):
    M, K = a.shape
    N = w.shape[1]
    return pl.pallas_call(
        functools.partial(_mm_kernel, act=act),
        grid=(M // tm, N // tn),
        in_specs=[pl.BlockSpec((tm, K), lambda i, j: (i, 0)), pl.BlockSpec((K, tn), lambda i, j: (0, j))],
        out_specs=pl.BlockSpec((tm, tn), lambda i, j: (i, j)),
        out_shape=jax.ShapeDtypeStruct((M, N), out_dtype),
        compiler_params=_cparams(("parallel", "parallel")),
        name=name,
    )(a, w)


def _mm_out_kernel(x_ref, a0, a1, a2, a3, w_ref, o_ref):
    acc = x_ref[...]
    for i, a in enumerate((a0, a1, a2, a3)):
        acc = acc + _dot(a[...], w_ref[i * GW:(i + 1) * GW, :])
    o_ref[...] = acc


def _matmul_out(x, parts, w, *, tm, tn):
    M, D = x.shape
    a_spec = pl.BlockSpec((tm, GW), lambda i, j: (i, 0))
    return pl.pallas_call(
        _mm_out_kernel,
        grid=(M // tm, D // tn),
        in_specs=[pl.BlockSpec((tm, tn), lambda i, j: (i, j)), a_spec, a_spec, a_spec, a_spec,
                  pl.BlockSpec((N_MIXERS * GW, tn), lambda i, j: (0, j))],
        out_specs=pl.BlockSpec((tm, tn), lambda i, j: (i, j)),
        out_shape=jax.ShapeDtypeStruct((M, D), F32),
        compiler_params=_cparams(("parallel", "parallel")),
        name="mm_out",
    )(x, *parts, w)


def _mm_down_kernel(x_ref, a_ref, w_ref, o_ref):
    @pl.when(pl.program_id(2) == 0)
    def _():
        o_ref[...] = x_ref[...]

    o_ref[...] += _dot(a_ref[...], w_ref[...])


def _matmul_down(x, a, w, *, tm, tn, tk):
    M, K = a.shape
    N = w.shape[1]
    return pl.pallas_call(
        _mm_down_kernel,
        grid=(M // tm, N // tn, K // tk),
        in_specs=[pl.BlockSpec((tm, tn), lambda i, j, k: (i, j)), pl.BlockSpec((tm, tk), lambda i, j, k: (i, k)),
                  pl.BlockSpec((tk, tn), lambda i, j, k: (k, j))],
        out_specs=pl.BlockSpec((tm, tn), lambda i, j, k: (i, j)),
        out_shape=jax.ShapeDtypeStruct((M, N), F32),
        compiler_params=_cparams(("parallel", "parallel", "arbitrary")),
        name="mm_down",
    )(x, a, w)


def _log_sigmoid(x):
    return jnp.minimum(x, 0.0) - jnp.log1p(jnp.exp(-jnp.abs(x)))


def _head_rms(x, g):
    return x * lax.rsqrt(jnp.mean(x * x, axis=-1, keepdims=True) + RMS_EPS) * g


def _prep_kernel(p_ref, gf_ref, gm_ref, gn_ref, fb_ref, *refs, tm, prompt, tiles_per_seq, n_alias):
    refs = refs[n_alias:]
    if prompt:
        (fox_ref, moba_ref, nsa_ref, win_ref, sm_ref, hm_ref, cmpin_ref, cc_ref, crow_ref, km_ref, carry_ref) = refs
    else:
        (fox_ref, moba_ref, nsa_ref, win_ref, sm_ref, hm_ref) = refs

    def seg(name, h):
        c0 = _SEG[name] + h * HEAD_DIM
        return p_ref[:, c0:c0 + HEAD_DIM]

    gfq, gfk = gf_ref[0:1, :], gf_ref[1:2, :]
    gmq, gmk = gm_ref[0:1, :], gm_ref[1:2, :]
    gnq, gnks, gnkw = gn_ref[0:1, :], gn_ref[2:3, :], gn_ref[3:4, :]

    for h in range(8):
        hm_ref[_SLOT["fq"] + h] = (_head_rms(seg("fq", h), gfq) * SCALE).astype(BF16)
        k = _head_rms(seg("fk", h), gfk)
        fox_ref[:, h * 128:(h + 1) * 128] = k
        hm_ref[_SLOT["fk"] + h] = k.astype(BF16)
        v = seg("fv", h)
        fox_ref[:, GW + h * 128:GW + (h + 1) * 128] = v
        hm_ref[_SLOT["fv"] + h] = v.astype(BF16)
        hm_ref[_SLOT["mq"] + h] = (_head_rms(seg("mq", h), gmq) * SCALE).astype(BF16)
        hm_ref[_SLOT["nq"] + h] = (_head_rms(seg("nq", h), gnq) * SCALE).astype(BF16)
    for h in range(4):
        k = _head_rms(seg("mk", h), gmk)
        moba_ref[:, h * 128:(h + 1) * 128] = k
        hm_ref[_SLOT["mk"] + h] = k.astype(BF16)
        if prompt:
            km_ref[h:h + 1, :] = jnp.mean(k, axis=0, keepdims=True)
        v = seg("mv", h)
        moba_ref[:, 512 + h * 128:512 + (h + 1) * 128] = v
        hm_ref[_SLOT["mv"] + h] = v.astype(BF16)
    for g in range(2):
        kc, vc = seg("nkc", g), seg("nvc", g)
        nsa_ref[:, g * 128:(g + 1) * 128] = kc
        nsa_ref[:, 256 + g * 128:256 + (g + 1) * 128] = vc
        if prompt:
            cmpin_ref[g] = kc
            cmpin_ref[2 + g] = vc
        ks = _head_rms(seg("nks", g), gnks)
        nsa_ref[:, 512 + g * 128:512 + (g + 1) * 128] = ks
        hm_ref[_SLOT["nks"] + g] = ks.astype(BF16)
        vs = seg("nvs", g)
        nsa_ref[:, 768 + g * 128:768 + (g + 1) * 128] = vs
        hm_ref[_SLOT["nvs"] + g] = vs.astype(BF16)
        kw = _head_rms(seg("nkw", g), gnkw)
        win_ref[:, g * 128:(g + 1) * 128] = kw
        hm_ref[_SLOT["nkw"] + g] = kw.astype(BF16)
        vw = seg("nvw", g)
        win_ref[:, 256 + g * 128:256 + (g + 1) * 128] = vw
        hm_ref[_SLOT["nvw"] + g] = vw.astype(BF16)

    sblk = p_ref[:, _SEG["small"]:_SEG["small"] + LANES]
    lane = _iota(sblk.shape, 1)
    logf = _log_sigmoid(sblk + fb_ref[...])
    gates = 1.0 / (1.0 + jnp.exp(-sblk))
    logf = jnp.where(lane < 8, logf, 0.0)
    sm_ref[...] = jnp.where(lane < 8, logf, jnp.where(lane < 32, gates, 0.0))

    if prompt:
        t = pl.program_id(0) % tiles_per_seq

        @pl.when(t == 0)
        def _():
            carry_ref[...] = jnp.zeros_like(carry_ref)

        tri = (_iota((tm, tm), 1) <= _iota((tm, tm), 0)).astype(BF16)
        c = _dot3_left(tri, logf) + carry_ref[...]
        cc_ref[...] = c
        carry_ref[...] = c[tm - 1:tm, :]
        crow_ref[...] = c.T[0:8, :]


def _dot3_left(m, x):
    hi, mid, lo = _split3(x)
    return _dot(m, hi) + _dot(m, mid) + _dot(m, lo)


def _prep(proj, gf, gm, gn, fbias, *, B, T, prompt, layer=0, depth=1, stacks=None):
    M = B * T
    tm = 256 if prompt else M
    nt = M // tm
    row = lambda w: pl.BlockSpec((tm, w), lambda i: (i, 0))
    slab = lambda w: pl.BlockSpec((None, tm, w), lambda i: (layer, i, 0))
    full = lambda a: pl.BlockSpec(a.shape, lambda i: (0,) * a.ndim)
    fb = jnp.zeros((1, LANES), F32).at[0, :8].set(fbias)
    tiles_per_seq = T // tm if prompt else 1
    out_shape = [jax.ShapeDtypeStruct((depth, M, 2 * GW), F32), jax.ShapeDtypeStruct((depth, M, GW), F32),
                 jax.ShapeDtypeStruct((depth, M, GW), F32), jax.ShapeDtypeStruct((M, 512), F32),
                 jax.ShapeDtypeStruct((M, LANES), F32), jax.ShapeDtypeStruct((B, NSLOT, T, HEAD_DIM), BF16)]
    hm_map = (lambda i: (i // tiles_per_seq, 0, i % tiles_per_seq, 0))
    out_specs = [slab(2 * GW), slab(GW), slab(GW), row(512), row(LANES),
                 pl.BlockSpec((None, NSLOT, tm, HEAD_DIM), hm_map)]
    stacks = tuple(stacks or ())
    n_fixed = 5
    aliases = {n_fixed + k: k for k in range(len(stacks))}
    scratch = []
    if prompt:
        out_shape += [jax.ShapeDtypeStruct((B, 4, T, HEAD_DIM), F32), jax.ShapeDtypeStruct((M, LANES), F32),
                      jax.ShapeDtypeStruct((B, T // tm, 8, tm), F32),
                      jax.ShapeDtypeStruct((B, T // MOBA_BLOCK, 4, HEAD_DIM), F32)]
        out_specs += [pl.BlockSpec((None, 4, tm, HEAD_DIM), hm_map), row(LANES),
                      pl.BlockSpec((None, None, 8, tm), lambda i: (i // tiles_per_seq, i % tiles_per_seq, 0, 0)),
                      pl.BlockSpec((None, None, 4, HEAD_DIM), lambda i: (i // tiles_per_seq, i % tiles_per_seq, 0, 0))]
        scratch = [pltpu.VMEM((1, LANES), F32)]
    return pl.pallas_call(
        functools.partial(_prep_kernel, tm=tm, prompt=prompt, tiles_per_seq=tiles_per_seq, n_alias=len(stacks)),
        grid=(nt,),
        in_specs=[row(PW), full(gf), full(gm), full(gn), full(fb)] + [pl.BlockSpec(memory_space=pl.ANY)] * len(stacks),
        out_specs=out_specs,
        out_shape=out_shape,
        scratch_shapes=scratch,
        input_output_aliases=aliases,
        compiler_params=_cparams(("arbitrary",)),
        name="prep_prompt" if prompt else "prep_sample",
    )(proj, gf, gm, gn, fb, *stacks)


def _lane_pick(x, lane_idx):
    return jnp.sum(jnp.where(_iota(x.shape, 1) == lane_idx, x, 0.0), axis=1, keepdims=True)


def _attend_pieces(pieces):
    m = None
    for s, _ in pieces:
        mi = jnp.max(s, axis=1, keepdims=True)
        m = mi if m is None else jnp.maximum(m, mi)
    l, acc = None, None
    for s, v in pieces:
        e = jnp.exp(s - m)
        li = jnp.sum(e, axis=1, keepdims=True)
        ai = _dot(e.astype(BF16), v)
        l = li if l is None else l + li
        acc = ai if acc is None else acc + ai
    return acc / jnp.maximum(l, 1e-30)


def _per_tile(qi, nq, body):
    for n in range(nq):
        pl.when(qi == n)(functools.partial(body, n))


def _causal_tile(tq):
    return _iota((1, tq), 1) <= _iota((tq, 1), 0)


def _fox_kernel(q_ref, k_ref, v_ref, cc_ref, crow_ref, o_ref, *, tq):
    h, qi = pl.program_id(1), pl.program_id(2)
    nq = k_ref.shape[0] // tq
    q = q_ref[...]
    cq = _lane_pick(cc_ref[...], h)

    def body(n):
        d0 = n * tq
        sd = _dot_nt(q, k_ref[d0:d0 + tq, :]) + (cq - crow_ref[n, pl.ds(h, 1), :])
        pieces = [(jnp.where(_causal_tile(tq), sd, NEG), v_ref[d0:d0 + tq, :])]
        if n > 0:
            ck = jnp.concatenate([crow_ref[j, pl.ds(h, 1), :] for j in range(n)], axis=1)
            pieces.append((_dot_nt(q, k_ref[0:d0, :]) + (cq - ck), v_ref[0:d0, :]))
        o_ref[...] = _attend_pieces(pieces).astype(o_ref.dtype)

    _per_tile(qi, nq, body)


def _fox_prompt(hm, cc, crow, *, B, T):
    tq = 256
    nq = T // tq
    slot = lambda s0: pl.BlockSpec((None, None, T, HEAD_DIM), lambda b, h, q: (b, s0 + h, 0, 0))
    return pl.pallas_call(
        functools.partial(_fox_kernel, tq=tq),
        grid=(B, 8, nq),
        in_specs=[pl.BlockSpec((None, None, tq, HEAD_DIM), lambda b, h, q: (b, _SLOT["fq"] + h, q, 0)),
                  slot(_SLOT["fk"]), slot(_SLOT["fv"]),
                  pl.BlockSpec((tq, LANES), lambda b, h, q: (b * nq + q, 0)),
                  pl.BlockSpec((None, nq, 8, tq), lambda b, h, q: (b, 0, 0, 0))],
        out_specs=pl.BlockSpec((tq, HEAD_DIM), lambda b, h, q: (b * nq + q, h)),
        out_shape=jax.ShapeDtypeStruct((B * T, GW), BF16),
        compiler_params=_cparams(("parallel", "parallel", "arbitrary")),
        name="fox_prompt",
    )(hm, hm, hm, cc, crow)


def _rank_select(score, n_cand, own, n_keep):
    lane = _iota(score.shape, 1)
    rank = jnp.zeros(score.shape, jnp.int32)
    for j in range(n_cand):
        sj = score[:, j:j + 1]
        beats = (sj > score) | ((sj == score) & (j < lane))
        rank = rank + jnp.where(beats & (j < own), 1, 0)
    return ((lane < own) & (rank < n_keep)) | (lane == own)


def _moba_kernel(q_ref, k_ref, v_ref, km_ref, o_ref, *, tq):
    qi = pl.program_id(2)
    T = k_ref.shape[0]
    n_blk = T // MOBA_BLOCK
    q = q_ref[...]
    gate = _dot_nt(q, km_ref[...].astype(BF16))
    pos = qi * tq + _iota((tq, 1), 0)
    own = pos // MOBA_BLOCK
    sel = jnp.where(_rank_select(gate, n_blk, own, MOBA_TOPK), 1.0, 0.0)

    def body(n):
        d0 = n * tq
        sd = _dot_nt(q, k_ref[d0:d0 + tq, :])
        pieces = [(jnp.where(_causal_tile(tq), sd, NEG), v_ref[d0:d0 + tq, :])]
        if n > 0:
            keep = jnp.concatenate([jnp.broadcast_to(sel[:, j:j + 1], (tq, tq)) for j in range(n)], axis=1)
            pieces.append((jnp.where(keep > 0.5, _dot_nt(q, k_ref[0:d0, :]), NEG), v_ref[0:d0, :]))
        o_ref[...] = _attend_pieces(pieces).astype(o_ref.dtype)

    _per_tile(qi, n_blk, body)


def _moba_prompt(hm, kmean, *, B, T):
    tq = MOBA_BLOCK
    nq = T // tq
    n_blk = T // MOBA_BLOCK
    slot = lambda s0: pl.BlockSpec((None, None, T, HEAD_DIM), lambda b, h, q: (b, s0 + h // 2, 0, 0))
    return pl.pallas_call(
        functools.partial(_moba_kernel, tq=tq),
        grid=(B, 8, nq),
        in_specs=[pl.BlockSpec((None, None, tq, HEAD_DIM), lambda b, h, q: (b, _SLOT["mq"] + h, q, 0)),
                  slot(_SLOT["mk"]), slot(_SLOT["mv"]),
                  pl.BlockSpec((None, None, n_blk, HEAD_DIM), lambda b, h, q: (b, h // 2, 0, 0))],
        out_specs=pl.BlockSpec((tq, HEAD_DIM), lambda b, h, q: (b * nq + q, h)),
        out_shape=jax.ShapeDtypeStruct((B * T, GW), BF16),
        compiler_params=_cparams(("parallel", "parallel", "arbitrary")),
        name="moba_prompt",
    )(hm, hm, hm, kmean)


def _sel_matrix(n_cmp_rows, n_cols, n_c):
    n = _iota((n_cmp_rows, n_cols), 0)
    j = _iota((n_cmp_rows, n_cols), 1)
    ratio = NSA_SEL_BLOCK // NSA_CMP_STRIDE
    lo = jnp.clip(ratio * j - NSA_CMP_BLOCK // NSA_CMP_STRIDE + 1, 0, n_c)
    hi = jnp.clip(ratio * (j + 1), 0, n_c)
    return ((n >= lo) & (n < hi)).astype(BF16)


def _nsa_kernel(q_ref, kc_ref, vc_ref, ks_ref, vs_ref, kw_ref, vw_ref, sm_ref, o_ref, *, tq):
    g, qi = pl.program_id(1), pl.program_id(2)
    T = ks_ref.shape[0]
    NC = kc_ref.shape[0]
    n_c = NC - 1
    n_sel = T // NSA_SEL_BLOCK
    R = q_ref.shape[0]
    pos = qi * tq + _iota((tq, 1), 0)

    kc = kc_ref[...].astype(BF16)
    vc = vc_ref[...].astype(BF16)
    ncol = _iota((1, NC), 1)
    mask_c = (ncol * NSA_CMP_STRIDE + (NSA_CMP_BLOCK - 1) <= pos) & (ncol < n_c)
    o_cmp = []
    psum = jnp.zeros((tq, NC), F32)
    for r in range(R):
        e, l = _softmax_unnorm(_dot_nt(q_ref[r], kc), mask_c)
        p = e / l
        psum = psum + p
        o_cmp.append(_dot(p.astype(BF16), vc))
    imp = _dot3(psum, _sel_matrix(NC, LANES, n_c))
    own = pos // NSA_SEL_BLOCK
    sel = jnp.where(_rank_select(imp, n_sel, own, NSA_SEL_COUNT - 1), 1.0, 0.0).astype(BF16)
    gates = sm_ref[...]
    gate_cols = [[_lane_pick(gates, GATE_LANE0 + 8 * k + g * R + r) for k in range(3)] for r in range(R)]
    n_win = (NSA_WINDOW + tq - 1) // tq

    def body(n):
        d0 = n * tq
        causal = _causal_tile(tq)
        blk_of_key = _iota((LANES, d0 + tq), 1) // NSA_SEL_BLOCK
        selk = _dot(sel, (blk_of_key == _iota((LANES, d0 + tq), 0)).astype(BF16))
        mask_d = (selk[:, d0:d0 + tq] > 0.5) & causal
        w_lo = max(n - n_win, 0)
        for r in range(R):
            q = q_ref[r]
            pieces = [(jnp.where(mask_d, _dot_nt(q, ks_ref[d0:d0 + tq, :]), NEG), vs_ref[d0:d0 + tq, :])]
            if n > 0:
                pieces.append((jnp.where(selk[:, 0:d0] > 0.5, _dot_nt(q, ks_ref[0:d0, :]), NEG), vs_ref[0:d0, :]))
            o_sel = _attend_pieces(pieces)
            pieces = [(jnp.where(causal, _dot_nt(q, kw_ref[d0:d0 + tq, :]), NEG), vw_ref[d0:d0 + tq, :])]
            for j in range(w_lo, n):
                s = _dot_nt(q, kw_ref[j * tq:(j + 1) * tq, :])
                if (n - j + 1) * tq - 1 >= NSA_WINDOW:
                    diff = (n - j) * tq + _iota((tq, 1), 0) - _iota((1, tq), 1)
                    s = jnp.where(diff < NSA_WINDOW, s, NEG)
                pieces.append((s, vw_ref[j * tq:(j + 1) * tq, :]))
            o_win = _attend_pieces(pieces)
            g0, g1, g2 = gate_cols[r]
            o = g0 * o_cmp[r] + g1 * o_sel + g2 * o_win
            o_ref[:, r * HEAD_DIM:(r + 1) * HEAD_DIM] = o.astype(o_ref.dtype)

    _per_tile(qi, T // tq, body)


def _nsa_prompt(hm, cmp_kv, sm, *, B, T):
    tq = 256
    nq = T // tq
    G = NSA_KV_HEADS
    R = 8 // G
    NC = cmp_kv.shape[2]
    slot = lambda s0: pl.BlockSpec((None, None, T, HEAD_DIM), lambda b, g, q: (b, s0 + g, 0, 0))
    return pl.pallas_call(
        functools.partial(_nsa_kernel, tq=tq),
        grid=(B, G, nq),
        in_specs=[pl.BlockSpec((None, R, tq, HEAD_DIM), lambda b, g, q: (b, _SLOT["nq"] // R + g, q, 0)),
                  pl.BlockSpec((None, None, NC, HEAD_DIM), lambda b, g, q: (b, g, 0, 0)),
                  pl.BlockSpec((None, None, NC, HEAD_DIM), lambda b, g, q: (b, G + g, 0, 0)),
                  slot(_SLOT["nks"]), slot(_SLOT["nvs"]), slot(_SLOT["nkw"]), slot(_SLOT["nvw"]),
                  pl.BlockSpec((tq, LANES), lambda b, g, q: (b * nq + q, 0))],
        out_specs=pl.BlockSpec((tq, R * HEAD_DIM), lambda b, g, q: (b * nq + q, g)),
        out_shape=jax.ShapeDtypeStruct((B * T, GW), BF16),
        compiler_params=_cparams(("parallel", "parallel", "arbitrary")),
        name="nsa_prompt",
    )(hm, cmp_kv, cmp_kv, hm, hm, hm, hm, sm)


def _compress_kernel(c_ref, pe_ref, w1_ref, w2_ref, g_ref, o_ref):
    c = pl.program_id(1)
    x = c_ref[...]
    NC = x.shape[0]
    a = _dot((x + pe_ref[0]).astype(BF16), w1_ref[0])
    bm = _dot((x + pe_ref[1]).astype(BF16), w1_ref[1])
    pre = a + pltpu.roll(bm, NC - 1, 0)
    hid = pre * (1.0 / (1.0 + jnp.exp(-pre)))
    out = _dot(hid.astype(BF16), w2_ref[...])
    out = jnp.where(c < NSA_KV_HEADS, _head_rms(out, g_ref[...]), out)
    o_ref[...] = jnp.where(_iota(out.shape, 0) < NC - 1, out, 0.0)


def _compress(x, pe, w1, w2, gain):
    B, C, NC, W = x.shape
    G = NSA_KV_HEADS
    return pl.pallas_call(
        _compress_kernel,
        grid=(B, C),
        in_specs=[pl.BlockSpec((None, None, NC, W), lambda b, c: (b, c, 0, 0)),
                  pl.BlockSpec((None, 2, 1, W), lambda b, c: (c // G, 0, 0, 0)),
                  pl.BlockSpec((None, 2, W, HEAD_DIM), lambda b, c: (c // G, 0, 0, 0)),
                  pl.BlockSpec((None, HEAD_DIM, HEAD_DIM), lambda b, c: (c // G, 0, 0)),
                  pl.BlockSpec((1, HEAD_DIM), lambda b, c: (0, 0))],
        out_specs=pl.BlockSpec((None, None, NC, HEAD_DIM), lambda b, c: (b, c, 0, 0)),
        out_shape=jax.ShapeDtypeStruct((B, C, NC, HEAD_DIM), F32),
        compiler_params=_cparams(("parallel", "parallel")),
        name="nsa_compress",
    )(x, pe, w1, w2, gain)


def _pool_kernel(u_ref, halo_ref, w_ref, sc_ref, o_ref, ext_ref, *, tp):
    t = pl.program_id(1)
    HALO = halo_ref.shape[0]
    halo = halo_ref[...]
    ext_ref[0:HALO, :] = jnp.where(t > 0, halo, 0.0)
    ext_ref[HALO:HALO + tp, :] = u_ref[...]
    pos = t * tp + _iota((tp, 1), 0)
    PG = GW // len(POOL_WINDOWS)
    for gi, w in enumerate(POOL_WINDOWS):
        c0 = gi * PG
        u = ext_ref[HALO:HALO + tp, c0:c0 + PG]
        win = u
        for k in range(1, w):
            win = win + ext_ref[HALO - k:HALO - k + tp, c0:c0 + PG]
        cnt = jnp.minimum(w, pos + 1).astype(F32)
        d = win / cnt - u
        o = _dot(d.astype(BF16), w_ref[gi]) * sc_ref[:, c0:c0 + PG]
        o_ref[:, c0:c0 + PG] = o.astype(o_ref.dtype)


def _pool_prompt(proj, pool_w, pool_scale, *, B, T):
    tp = 512
    HALO = 16
    nt = T // tp
    cb = _SEG["pu"] // GW
    return pl.pallas_call(
        functools.partial(_pool_kernel, tp=tp),
        grid=(B, nt),
        in_specs=[pl.BlockSpec((tp, GW), lambda b, t: (b * nt + t, cb)),
                  pl.BlockSpec((HALO, GW), lambda b, t: (jnp.maximum((b * nt + t) * (tp // HALO) - 1, 0), cb)),
                  pl.BlockSpec(pool_w.shape, lambda b, t: (0, 0, 0)),
                  pl.BlockSpec((1, GW), lambda b, t: (0, 0))],
        out_specs=pl.BlockSpec((tp, GW), lambda b, t: (b * nt + t, 0)),
        out_shape=jax.ShapeDtypeStruct((B * T, GW), BF16),
        scratch_shapes=[pltpu.VMEM((HALO + tp, GW), F32)],
        compiler_params=_cparams(("parallel", "arbitrary")),
        name="pool_prompt",
    )(proj, proj, pool_w, pool_scale)


def _diag_col(row, lane0, n):
    b = jnp.broadcast_to(row, (n, row.shape[1]))
    keep = _iota(b.shape, 1) == _iota(b.shape, 0) + lane0
    return jnp.sum(jnp.where(keep, b, 0.0), axis=1, keepdims=True)


def _pad_rows(x, n):
    return jnp.concatenate([x, jnp.zeros((n - x.shape[0],) + x.shape[1:], x.dtype)], axis=0)


def _sfox_kernel(pt_ref, *refs, pps):
    kv_refs, lf_refs = refs[:pps], refs[pps:2 * pps]
    later_ref, q_ref, new_ref, sm_ref, o_ref, m_ref, l_ref, acc_ref, carry_ref = refs[2 * pps:]
    b, i = pl.program_id(0), pl.program_id(1)
    H = 8
    P = kv_refs[0].shape[0]
    q = q_ref[...]

    @pl.when(i == 0)
    def _():
        knew, vnew = new_ref[0:H, :], new_ref[H:2 * H, :]
        s_new = jnp.sum(q * knew, axis=1, keepdims=True)
        m_ref[...] = jnp.broadcast_to(s_new, m_ref.shape)
        l_ref[...] = jnp.ones_like(l_ref)
        acc_ref[...] = vnew
        carry_ref[...] = jnp.broadcast_to(_diag_col(sm_ref[pl.ds(b, 1), :], 0, H), carry_ref.shape)

    ones = jnp.ones((HEAD_DIM, HEAD_DIM), BF16)
    carry = carry_ref[...]
    scores = []
    for kv_ref, lf_ref in zip(kv_refs, lf_refs):
        lf = lf_ref[...]
        z = (kv_ref[:, 0] * q[None] + later_ref[...] * lf[None]).reshape(P * H, HEAD_DIM)
        hi = z.astype(BF16)
        mid = (z - hi.astype(F32)).astype(BF16)
        scores.append((_dot(hi, ones) + _dot(mid, ones)).reshape(P, H, HEAD_DIM) + carry[None])
        carry = carry + jnp.sum(lf, axis=1, keepdims=True)
    m_old = m_ref[...]
    m_new = m_old
    for s in scores:
        m_new = jnp.maximum(m_new, jnp.max(s, axis=0))
    alpha = jnp.exp(m_old - m_new)
    l = alpha * l_ref[...]
    acc = alpha * acc_ref[...]
    for s, kv_ref in zip(scores, kv_refs):
        p = jnp.exp(s - m_new[None])
        l = l + jnp.sum(p, axis=0)
        acc = acc + jnp.sum(p * kv_ref[:, 1], axis=0)
    l_ref[...] = l
    acc_ref[...] = acc
    m_ref[...] = m_new
    carry_ref[...] = carry

    @pl.when(i == pl.num_programs(1) - 1)
    def _():
        o_ref[...] = acc_ref[...] / jnp.maximum(l_ref[...], 1e-30)


def _fox_sample(layer, page_table, cache_kv, cache_lfT, hm_s, fox_new, sm_s, *, Bs):
    NP = page_table.shape[1]
    PS = cache_kv.shape[2]
    H = 8
    r = jnp.arange(PS, dtype=jnp.int32)
    later = jnp.broadcast_to((r[None, :] > r[:, None]).astype(F32)[:, None, :], (PS, H, PS))
    rep = pltpu.VMEM((H, HEAD_DIM), F32)
    pps = 8 if NP % 8 == 0 else 1
    page = lambda k: (lambda b, i, pt: pt[b, NP - 1 - (pps * i + k)])
    kv_spec = lambda k: pl.BlockSpec((None, None, PS, 2, H, HEAD_DIM),
                                     lambda b, i, pt: (layer, page(k)(b, i, pt), 0, 0, 0, 0))
    lf_spec = lambda k: pl.BlockSpec((None, None, H, PS), lambda b, i, pt: (layer, page(k)(b, i, pt), 0, 0))
    grid_spec = pltpu.PrefetchScalarGridSpec(
        num_scalar_prefetch=1,
        grid=(Bs, NP // pps),
        in_specs=[kv_spec(k) for k in range(pps)] + [lf_spec(k) for k in range(pps)]
        + [pl.BlockSpec((PS, H, PS), lambda b, i, pt: (0, 0, 0)),
           pl.BlockSpec((None, None, H, HEAD_DIM), lambda b, i, pt: (b, _SLOT["fq"] // H, 0, 0)),
           pl.BlockSpec((None, 2 * H, HEAD_DIM), lambda b, i, pt: (b, 0, 0)),
           pl.BlockSpec(sm_s.shape, lambda b, i, pt: (0, 0))],
        out_specs=pl.BlockSpec((None, H, HEAD_DIM), lambda b, i, pt: (b, 0, 0)),
        scratch_shapes=[rep, rep, rep, rep],
    )
    assert PS == HEAD_DIM
    return pl.pallas_call(
        functools.partial(_sfox_kernel, pps=pps),
        grid_spec=grid_spec,
        out_shape=jax.ShapeDtypeStruct((Bs, H, HEAD_DIM), F32),
        compiler_params=_cparams(("parallel", "arbitrary")),
        name="fox_sample",
    )(page_table, *([cache_kv] * pps), *([cache_lfT] * pps), later,
      hm_s.reshape(hm_s.shape[0], NSLOT // H, H, HEAD_DIM), fox_new, sm_s)


def _argmax_rounds(score, n_rounds, out_lanes):
    rows, L = score.shape
    lane = _iota(score.shape, 1)
    olane = _iota((rows, out_lanes), 1)
    out = jnp.zeros((rows, out_lanes), jnp.int32)
    for t in range(n_rounds):
        m = jnp.max(score, axis=1, keepdims=True)
        idx = jnp.min(jnp.where(score == m, lane, L), axis=1, keepdims=True)
        ok = jnp.where(m > NEG, 1, 0)
        out = jnp.where(olane == t, idx, out)
        out = jnp.where(olane == n_rounds + t, ok, out)
        score = jnp.where(lane == idx, NEG, score)
    return out


def _smoba_gate_kernel(pt_ref, *refs, ppb, bps):
    k_refs = refs[:bps * ppb]
    q_ref, o_ref, g_ref = refs[bps * ppb:]
    i = pl.program_id(1)
    H = 8

    @pl.when(i == 0)
    def _():
        g_ref[...] = jnp.full_like(g_ref, NEG)

    q = q_ref[...]
    g = g_ref[...]
    for jb in range(bps):
        ksum = jnp.sum(k_refs[jb * ppb][...], axis=0)
        for t in range(1, ppb):
            ksum = ksum + jnp.sum(k_refs[jb * ppb + t][...], axis=0)
        kmean = ksum / float(MOBA_BLOCK)
        kme = jnp.concatenate([kmean[h // 2:h // 2 + 1, :] for h in range(H)], axis=0)
        gate = jnp.sum(q * kme, axis=1, keepdims=True)
        g = jnp.where(_iota(g.shape, 1) == i * bps + jb, gate, g)
    g_ref[...] = g

    @pl.when(i == pl.num_programs(1) - 1)
    def _():
        o_ref[...] = _argmax_rounds(g, MOBA_TOPK, LANES)


def _moba_sample_gate(layer, page_table, cache_kv, hm_s, *, Bs):
    NP = page_table.shape[1]
    PS = cache_kv.shape[2]
    nkv = cache_kv.shape[4]
    ppb = MOBA_BLOCK // PS
    NB = NP // ppb
    bps = 4 if NB % 4 == 0 else 1
    assert NP % ppb == 0
    page = lambda j: pl.BlockSpec((None, None, PS, None, nkv, HEAD_DIM),
                                  lambda b, i, pt: (layer, pt[b, bps * ppb * i + j], 0, 0, 0, 0))
    grid_spec = pltpu.PrefetchScalarGridSpec(
        num_scalar_prefetch=1,
        grid=(Bs, NB // bps),
        in_specs=[page(j) for j in range(bps * ppb)]
        + [pl.BlockSpec((None, None, 8, HEAD_DIM), lambda b, i, pt: (b, _SLOT["mq"] // 8, 0, 0))],
        out_specs=pl.BlockSpec((None, 8, LANES), lambda b, i, pt: (b, 0, 0)),
        scratch_shapes=[pltpu.VMEM((8, max(LANES, NB)), F32)],
    )
    return pl.pallas_call(
        functools.partial(_smoba_gate_kernel, ppb=ppb, bps=bps),
        grid_spec=grid_spec,
        out_shape=jax.ShapeDtypeStruct((Bs, 8, LANES), jnp.int32),
        compiler_params=_cparams(("parallel", "arbitrary")),
        name="moba_sample_gate",
    )(page_table, *([cache_kv] * (bps * ppb)), hm_s.reshape(hm_s.shape[0], NSLOT // 8, 8, HEAD_DIM))


def _smoba_attn_kernel(pg_ref, ok_ref, *refs, n_pages, ppb):
    H = 8
    kv_refs = refs[:H]
    q_ref, new_ref, o_ref, ks_ref, vs_ref = refs[H:]
    b, j = pl.program_id(0), pl.program_id(1)
    for h in range(H):
        ks_ref[h, j] = kv_refs[h][:, 0, h // 2, :]
        vs_ref[h, j] = kv_refs[h][:, 1, h // 2, :]

    @pl.when(j == n_pages - 1)
    def _():
        for h in range(H):
            kvh = h // 2
            q = q_ref[h:h + 1, :]
            knew = new_ref[kvh:kvh + 1, :]
            vnew = new_ref[4 + kvh:5 + kvh, :]
            s_new = jnp.sum(q * knew, axis=1, keepdims=True)
            s, ok = [], []
            for t in range(n_pages):
                s.append(jnp.sum(ks_ref[h, t] * q, axis=1, keepdims=True))
                ok.append(ok_ref[(b * H + h) * (n_pages // ppb) + t // ppb] > 0)
            m = s_new
            for t in range(n_pages):
                m = jnp.maximum(m, jnp.max(jnp.where(ok[t], s[t], NEG), axis=0, keepdims=True))
            l = jnp.exp(s_new - m)
            acc = l * vnew
            for t in range(n_pages):
                e = jnp.where(ok[t], jnp.exp(s[t] - m), 0.0)
                l = l + jnp.sum(e, axis=0, keepdims=True)
                acc = acc + jnp.sum(e * vs_ref[h, t], axis=0, keepdims=True)
            o_ref[h:h + 1, :] = acc / jnp.maximum(l, 1e-30)


def _moba_sample_attn(layer, pages, oks, cache_kv, hm_s, moba_new, *, Bs):
    PS = cache_kv.shape[2]
    nkv = cache_kv.shape[4]
    ppb = MOBA_BLOCK // PS
    n_pages = MOBA_TOPK * ppb
    H = 8
    page = lambda h: pl.BlockSpec((None, None, PS, 2, nkv, HEAD_DIM),
                                  lambda b, j, pg, ok: (layer, pg[(b * H + h) * n_pages + j], 0, 0, 0, 0))
    grid_spec = pltpu.PrefetchScalarGridSpec(
        num_scalar_prefetch=2,
        grid=(Bs, n_pages),
        in_specs=[page(h) for h in range(H)]
        + [pl.BlockSpec((None, None, 8, HEAD_DIM), lambda b, j, pg, ok: (b, _SLOT["mq"] // 8, 0, 0)),
           pl.BlockSpec((None, 8, HEAD_DIM), lambda b, j, pg, ok: (b, 0, 0))],
        out_specs=pl.BlockSpec((None, 8, HEAD_DIM), lambda b, j, pg, ok: (b, 0, 0)),
        scratch_shapes=[pltpu.VMEM((H, n_pages, PS, HEAD_DIM), F32), pltpu.VMEM((H, n_pages, PS, HEAD_DIM), F32)],
    )
    return pl.pallas_call(
        functools.partial(_smoba_attn_kernel, n_pages=n_pages, ppb=ppb),
        grid_spec=grid_spec,
        out_shape=jax.ShapeDtypeStruct((Bs, 8, HEAD_DIM), F32),
        compiler_params=_cparams(("parallel", "arbitrary")),
        name="moba_sample_attn",
    )(pages, oks, *([cache_kv] * H), hm_s.reshape(hm_s.shape[0], NSLOT // 8, 8, HEAD_DIM), moba_new)


def _snsa_gather_kernel(pt_ref, *refs, pps):
    x_refs, o_ref = refs[:pps], refs[pps]
    PS = x_refs[0].shape[0]
    G = x_refs[0].shape[2]
    S = NSA_CMP_STRIDE
    cpp = PS // S
    for k in range(pps):
        for j in range(2):
            for g in range(G):
                for r in range(S):
                    o_ref[j * G + g, k * cpp:(k + 1) * cpp, r * HEAD_DIM:(r + 1) * HEAD_DIM] = (
                        x_refs[k][pl.ds(r, cpp, stride=S), j, g, :])


def _nsa_sample_gather(layer, page_table, cache_kv, *, Bs):
    NP = page_table.shape[1]
    PS = cache_kv.shape[2]
    G = cache_kv.shape[4]
    C = 2 * G
    S = NSA_CMP_STRIDE
    pps = 4 if NP % 4 == 0 else 1
    cpp = PS // S
    page = lambda k: pl.BlockSpec((None, None, PS, 2, G, HEAD_DIM),
                                  lambda b, i, pt: (layer, pt[b, pps * i + k], 0, 0, 0, 0))
    grid_spec = pltpu.PrefetchScalarGridSpec(
        num_scalar_prefetch=1,
        grid=(Bs, NP // pps),
        in_specs=[page(k) for k in range(pps)],
        out_specs=pl.BlockSpec((None, C, pps * cpp, S * HEAD_DIM), lambda b, i, pt: (b, 0, i, 0)),
    )
    return pl.pallas_call(
        functools.partial(_snsa_gather_kernel, pps=pps),
        grid_spec=grid_spec,
        out_shape=jax.ShapeDtypeStruct((Bs, C, NP * cpp, S * HEAD_DIM), F32),
        compiler_params=_cparams(("parallel", "arbitrary")),
        name="nsa_sample_gather",
    )(page_table, *([cache_kv] * pps))


def _snsa_cmp_kernel(q_ref, kc_ref, vc_ref, o_ref, sel_ref, *, n_sel_past):
    R = q_ref.shape[0]
    NC = kc_ref.shape[0]
    n_c = NC - 1
    q = _pad_rows(q_ref[...], 16).astype(BF16)
    s = _dot_nt(q, kc_ref[...].astype(BF16))[0:R, :]
    mask = _iota((1, NC), 1) < n_c
    e, l = _softmax_unnorm(s, mask)
    p = e / l
    o_ref[...] = _dot(_pad_rows(p, 16).astype(BF16), vc_ref[...].astype(BF16))[0:R, :]
    psum = jnp.sum(p, axis=0, keepdims=True)
    imp = _dot3(_pad_rows(psum, 16), _sel_matrix(NC, n_sel_past, n_c))[0:1, :]
    sel_ref[...] = _argmax_rounds(imp, NSA_SEL_COUNT - 1, LANES)


def _nsa_sample_cmp(hm_s, cmp_kv, *, Bs):
    G = NSA_KV_HEADS
    R = 8 // G
    NC = cmp_kv.shape[2]
    n_sel_past = NC * NSA_CMP_STRIDE // NSA_SEL_BLOCK
    return pl.pallas_call(
        functools.partial(_snsa_cmp_kernel, n_sel_past=n_sel_past),
        grid=(Bs, G),
        in_specs=[pl.BlockSpec((None, None, R, HEAD_DIM), lambda b, g: (b, _SLOT["nq"] // R + g, 0, 0)),
                  pl.BlockSpec((None, None, NC, HEAD_DIM), lambda b, g: (b, g, 0, 0)),
                  pl.BlockSpec((None, None, NC, HEAD_DIM), lambda b, g: (b, G + g, 0, 0))],
        out_specs=[pl.BlockSpec((None, None, R, HEAD_DIM), lambda b, g: (b, g, 0, 0)),
                   pl.BlockSpec((None, None, 1, LANES), lambda b, g: (b, g, 0, 0))],
        out_shape=[jax.ShapeDtypeStruct((Bs, G, R, HEAD_DIM), F32), jax.ShapeDtypeStruct((Bs, G, 1, LANES), jnp.int32)],
        compiler_params=_cparams(("parallel", "parallel")),
        name="nsa_sample_cmp",
    )(hm_s.reshape(hm_s.shape[0], NSLOT // R, R, HEAD_DIM), cmp_kv, cmp_kv)


def _snsa_final_kernel(pg_ref, hf_ref, ok_ref, *refs, n_sel):
    G = NSA_KV_HEADS
    blk_refs = refs[:G]
    q_ref, new_ref, win_ref, wnew_ref, ocmp_ref, sm_ref, o_ref, ks_ref, vs_ref = refs[G:]
    b, j = pl.program_id(0), pl.program_id(1)
    for g in range(G):
        ks_ref[g, j] = blk_refs[g][:, 0, g, :]
        vs_ref[g, j] = blk_refs[g][:, 1, g, :]

    @pl.when(j == n_sel - 1)
    def _():
        R = q_ref.shape[0] // G
        gates = sm_ref[pl.ds(b, 1), :]
        for g in range(G):
            qf = q_ref[g * R:(g + 1) * R, :]
            q = _pad_rows(qf, 16).astype(BF16)
            ks_new = new_ref[2 * G + g:2 * G + g + 1, :]
            vs_new = new_ref[3 * G + g:3 * G + g + 1, :]
            s_new = jnp.sum(qf * ks_new, axis=1, keepdims=True)
            s, ok = [], []
            for t in range(n_sel):
                s.append(_dot_nt(q, ks_ref[g, t].astype(BF16))[0:R, :])
                ok.append(ok_ref[(b * G + g) * n_sel + t] > 0)
            m = s_new
            for t in range(n_sel):
                m = jnp.maximum(m, jnp.max(jnp.where(ok[t], s[t], NEG), axis=1, keepdims=True))
            l = jnp.exp(s_new - m)
            acc = l * vs_new
            for t in range(n_sel):
                e = jnp.where(ok[t], jnp.exp(s[t] - m), 0.0)
                l = l + jnp.sum(e, axis=1, keepdims=True)
                acc = acc + _dot(_pad_rows(e, 16).astype(BF16), vs_ref[g, t].astype(BF16))[0:R, :]
            o_sel = acc / jnp.maximum(l, 1e-30)
            WB = win_ref.shape[0]
            kw_new = wnew_ref[g:g + 1, :]
            vw_new = wnew_ref[G + g:G + g + 1, :]
            sw = _dot_nt(q, win_ref[:, 0, g, :].astype(BF16))[0:R, :]
            mask_w = (WB - _iota((1, WB), 1)) < NSA_WINDOW
            sw_new = jnp.sum(qf * kw_new, axis=1, keepdims=True)
            mw = jnp.maximum(sw_new, jnp.max(jnp.where(mask_w, sw, NEG), axis=1, keepdims=True))
            ew = jnp.where(mask_w, jnp.exp(sw - mw), 0.0)
            ew_new = jnp.exp(sw_new - mw)
            lw = ew_new + jnp.sum(ew, axis=1, keepdims=True)
            pv = _dot(_pad_rows(ew, 16).astype(BF16), win_ref[:, 1, g, :].astype(BF16))[0:R, :]
            o_win = (ew_new * vw_new + pv) / jnp.maximum(lw, 1e-30)
            g0 = _diag_col(gates, GATE_LANE0 + g * R, R)
            g1 = _diag_col(gates, GATE_LANE0 + 8 + g * R, R)
            g2 = _diag_col(gates, GATE_LANE0 + 16 + g * R, R)
            o_ref[g * R:(g + 1) * R, :] = g0 * ocmp_ref[g * R:(g + 1) * R, :] + g1 * o_sel + g2 * o_win


def _nsa_sample_final(layer, pages, halves, oks, cache_kv, hm_s, nsa_new, win_state, win_new, o_cmp, sm_s, *, Bs):
    G = NSA_KV_HEADS
    n_sel = NSA_SEL_COUNT - 1
    SB = NSA_SEL_BLOCK
    WB = win_state.shape[2]
    idx = lambda b, g, j: (b * G + g) * n_sel + j
    blk = lambda g: pl.BlockSpec((None, None, SB, 2, G, HEAD_DIM),
                                 lambda b, j, pg, hf, ok: (layer, pg[idx(b, g, j)], hf[idx(b, g, j)], 1, 0, 0))
    grid_spec = pltpu.PrefetchScalarGridSpec(
        num_scalar_prefetch=3,
        grid=(Bs, n_sel),
        in_specs=[blk(g) for g in range(G)]
        + [pl.BlockSpec((None, None, 8, HEAD_DIM), lambda b, j, pg, hf, ok: (b, _SLOT["nq"] // 8, 0, 0)),
           pl.BlockSpec((None, 8, HEAD_DIM), lambda b, j, pg, hf, ok: (b, 0, 0)),
           pl.BlockSpec((None, None, WB, 2, G, HEAD_DIM), lambda b, j, pg, hf, ok: (layer, b, 0, 0, 0, 0)),
           pl.BlockSpec((None, 2 * G, HEAD_DIM), lambda b, j, pg, hf, ok: (b, 0, 0)),
           pl.BlockSpec((None, 8, HEAD_DIM), lambda b, j, pg, hf, ok: (b, 0, 0)),
           pl.BlockSpec(sm_s.shape, lambda b, j, pg, hf, ok: (0, 0))],
        out_specs=pl.BlockSpec((None, 8, HEAD_DIM), lambda b, j, pg, hf, ok: (b, 0, 0)),
        scratch_shapes=[pltpu.VMEM((G, n_sel, SB, HEAD_DIM), F32), pltpu.VMEM((G, n_sel, SB, HEAD_DIM), F32)],
    )
    return pl.pallas_call(
        functools.partial(_snsa_final_kernel, n_sel=n_sel),
        grid_spec=grid_spec,
        out_shape=jax.ShapeDtypeStruct((Bs, 8, HEAD_DIM), F32),
        compiler_params=_cparams(("parallel", "arbitrary")),
        name="nsa_sample_final",
    )(pages, halves, oks, *([cache_kv] * G), hm_s.reshape(hm_s.shape[0], NSLOT // 8, 8, HEAD_DIM), nsa_new, win_state,
      win_new, o_cmp.reshape(Bs, 8, HEAD_DIM), sm_s)


def _spool_kernel(st_ref, u_ref, w_ref, sc_ref, o_ref, *, Bs, pos0):
    u_all = u_ref[...]
    PG = GW // len(POOL_WINDOWS)
    NB = st_ref.shape[1]
    out = []
    for gi, w in enumerate(POOL_WINDOWS):
        c0 = gi * PG
        u = u_all[:, c0:c0 + PG]
        win = u
        for k in range(1, w):
            prev = st_ref[:, NB - k, c0:c0 + PG]
            if prev.shape[0] != u.shape[0]:
                prev = _pad_rows(prev, u.shape[0])
            win = win + prev
        d = win / float(min(w, pos0 + 1)) - u
        out.append(_dot(d.astype(BF16), w_ref[gi]) * sc_ref[:, c0:c0 + PG])
    o_ref[...] = jnp.concatenate(out, axis=1).astype(o_ref.dtype)


def _pool_sample(layer, state_pool, proj_s, pool_w, pool_scale, *, Bs, pos0):
    Ms = proj_s.shape[0]
    NB = state_pool.shape[2]
    assert NB >= max(POOL_WINDOWS) - 1 and pos0 >= NB
    cb = _SEG["pu"] // GW
    return pl.pallas_call(
        functools.partial(_spool_kernel, Bs=Bs, pos0=pos0),
        grid=(1,),
        in_specs=[pl.BlockSpec((None, Bs, NB, GW), lambda i: (layer, 0, 0, 0)),
                  pl.BlockSpec((Ms, GW), lambda i: (0, cb)),
                  pl.BlockSpec(pool_w.shape, lambda i: (0, 0, 0)),
                  pl.BlockSpec((1, GW), lambda i: (0, 0))],
        out_specs=pl.BlockSpec((Ms, GW), lambda i: (0, 0)),
        out_shape=jax.ShapeDtypeStruct((Ms, GW), BF16),
        compiler_params=_cparams(("arbitrary",)),
        name="pool_sample",
    )(state_pool, proj_s, pool_w, pool_scale)


def _layer_weights(l, norm1_g, w_in, fox_f_bias, fox_qk_g, moba_qk_g, nsa_qk_g, pool_w, pool_scale, cmp_pe, cmp_w1,
                   cmp_w2, w_out, norm2_g, w_up, w_down):
    D = w_in.shape[1]
    segs, c = {}, 0
    for name, n in _ORIG:
        segs[name] = w_in[l, :, c:c + n]
        c += n
    cols = [segs[n] for n in _NEW_ORDER]
    used = sum(x.shape[1] for x in cols)
    cols.append(jnp.zeros((D, PW - used), w_in.dtype))
    W = NSA_CMP_STRIDE * HEAD_DIM
    return dict(
        norm1_g=norm1_g[l], norm2_g=norm2_g[l],
        w_in=jnp.concatenate(cols, axis=1).astype(BF16),
        w_out=_cast_bf16(w_out, l), w_up=_cast_bf16(w_up, l), w_down=_cast_bf16(w_down, l),
        fox_f_bias=fox_f_bias[l], fox_qk_g=fox_qk_g[l], moba_qk_g=moba_qk_g[l], nsa_qk_g=nsa_qk_g[l],
        pool_w=pool_w[l].astype(BF16), pool_scale=pool_scale[l].reshape(1, GW),
        cmp_pe=cmp_pe[l].reshape(2, 2, 1, W), cmp_w1=cmp_w1[l].reshape(2, 2, W, HEAD_DIM).astype(BF16),
        cmp_w2=cmp_w2[l].astype(BF16), kc_gain=nsa_qk_g[l][1:2],
    )


def _mlp(x1, lw, *, tm):
    h2 = _rms(x1, lw["norm2_g"], tm=min(tm, 512))
    u = _matmul(h2, lw["w_up"], tm=tm, tn=1024, act="relu2", out_dtype=BF16, name="mm_up")
    return _matmul_down(x1, u, lw["w_down"], tm=tm, tn=1024, tk=4096)


def _prompt_layer(x, lw, *, B, T, layer, depth, stacks):
    tm = min(1024, B * T)
    h = _rms(x, lw["norm1_g"], tm=min(tm, 512))
    proj = _matmul(h, lw["w_in"], tm=tm, tn=1024, name="mm_in")
    fox_kv, moba_kv, nsa_kv, win, sm, hm, cmp_in, cc, crow, kmean = _prep(
        proj, lw["fox_qk_g"], lw["moba_qk_g"], lw["nsa_qk_g"], lw["fox_f_bias"], B=B, T=T, prompt=True,
        layer=layer, depth=depth, stacks=stacks)
    o_fox = _fox_prompt(hm, cc, crow, B=B, T=T)
    o_pool = _pool_prompt(proj, lw["pool_w"], lw["pool_scale"], B=B, T=T)
    o_moba = _moba_prompt(hm, jnp.swapaxes(kmean, 1, 2), B=B, T=T)
    cmp_in = cmp_in.reshape(B, 4, T // NSA_CMP_STRIDE, NSA_CMP_STRIDE * HEAD_DIM)
    cmp_kv = _compress(cmp_in, lw["cmp_pe"], lw["cmp_w1"], lw["cmp_w2"], lw["kc_gain"])
    o_nsa = _nsa_prompt(hm, cmp_kv, sm, B=B, T=T)
    x1 = _matmul_out(x, (o_fox, o_pool, o_moba, o_nsa), lw["w_out"], tm=tm, tn=1024)
    y = _mlp(x1, lw, tm=tm)
    wb = min(NSA_WINDOW, T)
    new = (sm[:, 0:8].reshape(B, T, 8), win.reshape(B, T, 2, NSA_KV_HEADS, HEAD_DIM)[:, T - wb:],
           proj.reshape(B, T, PW)[:, T - POOL_BUF:, _SEG["pu"]:_SEG["pu"] + GW])
    return y, new, (fox_kv, moba_kv, nsa_kv)


def _sample_layer(x, l, lw, caches, page_table, *, Bs):
    cache_fox_kv, cache_fox_lfT, cache_moba_kv, cache_nsa_kv, state_nsa_win, state_pool = caches
    Ms = x.shape[0]
    NP = page_table.shape[1]
    PS = cache_fox_kv.shape[2]
    past = NP * PS
    G = NSA_KV_HEADS
    h = _rms(x, lw["norm1_g"], tm=Ms)
    proj = _matmul(h, lw["w_in"], tm=Ms, tn=1024, name="mm_in_s")
    fox_kv, moba_kv, nsa_kv, win, sm, hm = _prep(
        proj, lw["fox_qk_g"], lw["moba_qk_g"], lw["nsa_qk_g"], lw["fox_f_bias"], B=1, T=Ms, prompt=False)
    fox_kv, moba_kv, nsa_kv = fox_kv[0], moba_kv[0], nsa_kv[0]
    hm_s = jnp.swapaxes(hm[0], 0, 1).astype(F32)
    fox_new = fox_kv.reshape(Ms, 16, HEAD_DIM)
    moba_new = moba_kv.reshape(Ms, 8, HEAD_DIM)
    nsa_new = nsa_kv.reshape(Ms, 8, HEAD_DIM)
    win_new = win.reshape(Ms, 4, HEAD_DIM)

    o_fox = _fox_sample(l, page_table, cache_fox_kv, cache_fox_lfT, hm_s, fox_new, sm, Bs=Bs)

    o_pool = _pool_sample(l, state_pool, proj, lw["pool_w"], lw["pool_scale"], Bs=Bs, pos0=past)

    top = _moba_sample_gate(l, page_table, cache_moba_kv, hm_s, Bs=Bs)
    ppb = MOBA_BLOCK // PS
    blk = top[:, :, 0:MOBA_TOPK]
    oks = top[:, :, MOBA_TOPK:2 * MOBA_TOPK]
    blk = jnp.where(oks > 0, blk, 0)
    pidx = (blk[..., None] * ppb + jnp.arange(ppb, dtype=jnp.int32)).reshape(Bs, 8 * MOBA_TOPK * ppb)
    pages = jnp.take_along_axis(page_table, pidx, axis=1).reshape(-1)
    o_moba = _moba_sample_attn(l, pages, oks.reshape(-1), cache_moba_kv, hm_s, moba_new, Bs=Bs)

    cmp_in = _nsa_sample_gather(l, page_table, cache_nsa_kv, Bs=Bs)
    cmp_kv = _compress(cmp_in, lw["cmp_pe"], lw["cmp_w1"], lw["cmp_w2"], lw["kc_gain"])
    o_cmp, sel = _nsa_sample_cmp(hm_s, cmp_kv, Bs=Bs)
    n_sel = NSA_SEL_COUNT - 1
    sblk = sel[:, :, 0, 0:n_sel]
    soks = sel[:, :, 0, n_sel:2 * n_sel]
    sblk = jnp.where(soks > 0, sblk, 0)
    spp = PS // NSA_SEL_BLOCK
    spages = jnp.take_along_axis(page_table, (sblk // spp).reshape(Bs, G * n_sel), axis=1).reshape(-1)
    o_nsa = _nsa_sample_final(l, spages, (sblk % spp).reshape(-1), soks.reshape(-1), cache_nsa_kv, hm_s, nsa_new,
                              state_nsa_win, win_new, o_cmp, sm, Bs=Bs)

    def rows(o):
        o = o.reshape(Bs, GW).astype(BF16)
        return jnp.concatenate([o, jnp.zeros((Ms - Bs, GW), BF16)], axis=0)

    x1 = _matmul_out(x, (rows(o_fox), o_pool, rows(o_moba), rows(o_nsa)), lw["w_out"], tm=Ms, tn=1024)
    y = _mlp(x1, lw, tm=Ms)
    new = (fox_kv[:Bs].reshape(Bs, 1, 2, 8, HEAD_DIM), sm[:Bs, 0:8].reshape(Bs, 1, 8),
           moba_kv[:Bs].reshape(Bs, 1, 2, 4, HEAD_DIM), nsa_kv[:Bs].reshape(Bs, 1, 4, G, HEAD_DIM),
           win[:Bs].reshape(Bs, 1, 2, G, HEAD_DIM), proj[:Bs, _SEG["pu"]:_SEG["pu"] + GW].reshape(Bs, 1, GW))
    return y, new


def kernel(x_prompt, x_sample, cache_fox_kv, cache_fox_logf, cache_moba_kv, cache_nsa_kv, state_nsa_win, state_pool,
           page_table, norm1_g, w_in, fox_f_bias, fox_qk_g, moba_qk_g, nsa_qk_g, pool_w, pool_scale, cmp_pe, cmp_w1,
           cmp_w2, w_out, norm2_g, w_up, w_down):
    B, T, D = x_prompt.shape
    Bs, Ts, _ = x_sample.shape
    assert Ts == 1 and D == N_MIXERS * GW
    depth = w_in.shape[0]
    n_pool, PS = cache_fox_kv.shape[1:3]
    WB = state_nsa_win.shape[2]
    Ms = 16
    assert Bs <= Ms

    caches = (cache_fox_kv, jnp.swapaxes(cache_fox_logf, 2, 3), cache_moba_kv, cache_nsa_kv, state_nsa_win, state_pool)

    xp = x_prompt.reshape(B * T, D)
    xs = jnp.concatenate([x_sample.reshape(Bs, D), jnp.zeros((Ms - Bs, D), x_sample.dtype)], axis=0)
    new_p, new_s, stacks = [], [], None
    for l in range(depth):
        lw = _layer_weights(l, norm1_g, w_in, fox_f_bias, fox_qk_g, moba_qk_g, nsa_qk_g, pool_w, pool_scale, cmp_pe,
                            cmp_w1, cmp_w2, w_out, norm2_g, w_up, w_down)
        xp, rows_p, stacks = _prompt_layer(xp, lw, B=B, T=T, layer=l, depth=depth, stacks=stacks)
        xs, rows_s = _sample_layer(xs, l, lw, caches, page_table, Bs=Bs)
        new_p.append(rows_p)
        new_s.append(rows_s)

    def stk(rows, i):
        return jnp.stack([r[i] for r in rows], axis=0)

    fox_kv_p, moba_kv_p, nsa_kv_p = stacks
    win_s = jnp.concatenate([state_nsa_win[:, :, 1:], stk(new_s, 4)], axis=2) if WB > 0 else stk(new_s, 4)[:, :, :0]
    pool_s = jnp.concatenate([state_pool[:, :, 1:], stk(new_s, 5)], axis=2)
    return (xp.reshape(B, T, D), xs[:Bs].reshape(Bs, 1, D),
            fox_kv_p.reshape(depth, B, T, 2, 8, HEAD_DIM), stk(new_s, 0), stk(new_p, 0), stk(new_s, 1),
            moba_kv_p.reshape(depth, B, T, 2, 4, HEAD_DIM), stk(new_s, 2),
            nsa_kv_p.reshape(depth, B, T, 4, NSA_KV_HEADS, HEAD_DIM), stk(new_s, 3),
            stk(new_p, 1), win_s, stk(new_p, 2), pool_s)
```

```python
import functools

import jax
import jax.numpy as jnp
from jax import lax
from jax.experimental import pallas as pl
from jax.experimental.pallas import tpu as pltpu

F32 = jnp.float32
BF16 = jnp.bfloat16

HEAD_DIM = 128
N_MIXERS = 4
FOX_FORGET_BIAS = 2.0
POOL_WINDOWS = (2, 4, 8, 16)
POOL_BUF = max(POOL_WINDOWS) - 1
MOBA_BLOCK = 256
MOBA_TOPK = 3
NSA_KV_HEADS = 2
NSA_CMP_BLOCK = 32
NSA_CMP_STRIDE = 16
NSA_SEL_BLOCK = 64
NSA_SEL_COUNT = 16
NSA_WINDOW = 512
RMS_EPS = 1e-6
SCALE = HEAD_DIM ** -0.5
LOG2E = 1.4426950408889634

LANES = 128
VMEM_LIMIT = 56 * 1024 * 1024
NEG = -1e30

GW = 1024
_SEG = dict(fq=0, fk=1024, fv=2048, pu=3072, mq=4096, mk=5120, mv=5632, nq=6144,
            nkc=7168, nvc=7424, nks=7680, nvs=7936, nkw=8192, nvw=8448, small=8704)
PW = 9216
_ORIG = (("fq", 1024), ("fk", 1024), ("fv", 1024), ("ff", 8), ("pu", 1024), ("mq", 1024), ("mk", 512),
         ("mv", 512), ("nq", 1024), ("nkc", 256), ("nvc", 256), ("nks", 256), ("nvs", 256), ("nkw", 256),
         ("nvw", 256), ("ng", 24))
_NEW_ORDER = ("fq", "fk", "fv", "pu", "mq", "mk", "mv", "nq", "nkc", "nvc", "nks", "nvs", "nkw", "nvw", "ff", "ng")
_SLOT = dict(fq=0, fk=8, fv=16, mq=24, mk=32, mv=36, nq=40, nks=48, nvs=50, nkw=52, nvw=54)
NSLOT = 56
GATE_LANE0 = 8


def _cparams(sem):
    return pltpu.CompilerParams(dimension_semantics=sem, vmem_limit_bytes=VMEM_LIMIT)


def _iota(shape, dim):
    return lax.broadcasted_iota(jnp.int32, shape, dim)


def _dot(a, b):
    return jnp.dot(a, b, preferred_element_type=F32)


def _dot_nt(a, b):
    return lax.dot_general(a, b, (((1,), (1,)), ((), ())), preferred_element_type=F32)


def _split3(x):
    hi = x.astype(BF16)
    r1 = x - hi.astype(F32)
    mid = r1.astype(BF16)
    lo = (r1 - mid.astype(F32)).astype(BF16)
    return hi, mid, lo


def _dot3(x, m):
    hi, mid, lo = _split3(x)
    return _dot(hi, m) + _dot(mid, m) + _dot(lo, m)


def _softmax_unnorm(s, mask, base2=False):
    sm = jnp.where(mask, s, NEG)
    m = jnp.max(sm, axis=-1, keepdims=True)
    e = jnp.where(mask, jnp.exp2(sm - m) if base2 else jnp.exp(sm - m), 0.0)
    l = jnp.maximum(jnp.sum(e, axis=-1, keepdims=True), 1e-30)
    return e, l


def _rms_kernel(x_ref, g_ref, o_ref):
    x = x_ref[...]
    y = x * lax.rsqrt(jnp.mean(x * x, axis=-1, keepdims=True) + RMS_EPS)
    o_ref[...] = (y * g_ref[...]).astype(o_ref.dtype)


def _rms(x, g, tm):
    M, D = x.shape
    return pl.pallas_call(
        _rms_kernel,
        grid=(M // tm,),
        in_specs=[pl.BlockSpec((tm, D), lambda i: (i, 0)), pl.BlockSpec((1, D), lambda i: (0, 0))],
        out_specs=pl.BlockSpec((tm, D), lambda i: (i, 0)),
        out_shape=jax.ShapeDtypeStruct((M, D), BF16),
        compiler_params=_cparams(("parallel",)),
        name="rms",
    )(x, g.reshape(1, D))


def _cast_kernel(x_ref, o_ref):
    o_ref[...] = x_ref[...].astype(o_ref.dtype)


def _cast_bf16(w_stack, l):
    _, K, N = w_stack.shape
    tk = max(8, min(K, (8 * 1024 * 1024) // (4 * N)))
    assert K % tk == 0
    return pl.pallas_call(
        _cast_kernel,
        grid=(K // tk,),
        in_specs=[pl.BlockSpec((None, tk, N), lambda i: (l, i, 0))],
        out_specs=pl.BlockSpec((tk, N), lambda i: (i, 0)),
        out_shape=jax.ShapeDtypeStruct((K, N), BF16),
        compiler_params=_cparams(("parallel",)),
        name="cast_bf16",
    )(w_stack)


def _mm_kernel(a_ref, w_ref, o_ref, *, act):
    acc = _dot(a_ref[...], w_ref[...])
    if act == "relu2":
        acc = jnp.maximum(acc, 0.0)
        acc = acc * acc
    o_ref[...] = acc.astype(o_ref.dtype)


def _matmul(a, w, *, tm, tn, act=None, out_dtype=F32, name="mm"):
    M, K = a.shape
    N = w.shape[1]
    return pl.pallas_call(
        functools.partial(_mm_kernel, act=act),
        grid=(M // tm, N // tn),
        in_specs=[pl.BlockSpec((tm, K), lambda i, j: (i, 0)), pl.BlockSpec((K, tn), lambda i, j: (0, j))],
        out_specs=pl.BlockSpec((tm, tn), lambda i, j: (i, j)),
        out_shape=jax.ShapeDtypeStruct((M, N), out_dtype),
        compiler_params=_cparams(("parallel", "parallel")),
        name=name,
    )(a, w)


def _mm_out_kernel(x_ref, a0, a1, a2, a3, w_ref, o_ref):
    acc = x_ref[...]
    for i, a in enumerate((a0, a1, a2, a3)):
        acc = acc + _dot(a[...], w_ref[i * GW:(i + 1) * GW, :])
    o_ref[...] = acc


def _matmul_out(x, parts, w, *, tm, tn):
    M, D = x.shape
    a_spec = pl.BlockSpec((tm, GW), lambda i, j: (i, 0))
    return pl.pallas_call(
        _mm_out_kernel,
        grid=(M // tm, D // tn),
        in_specs=[pl.BlockSpec((tm, tn), lambda i, j: (i, j)), a_spec, a_spec, a_spec, a_spec,
                  pl.BlockSpec((N_MIXERS * GW, tn), lambda i, j: (0, j))],
        out_specs=pl.BlockSpec((tm, tn), lambda i, j: (i, j)),
        out_shape=jax.ShapeDtypeStruct((M, D), F32),
        compiler_params=_cparams(("parallel", "parallel")),
        name="mm_out",
    )(x, *parts, w)


def _mm_down_kernel(x_ref, a_ref, w_ref, o_ref):
    @pl.when(pl.program_id(2) == 0)
    def _():
        o_ref[...] = x_ref[...]

    o_ref[...] += _dot(a_ref[...], w_ref[...])


def _matmul_down(x, a, w, *, tm, tn, tk):
    M, K = a.shape
    N = w.shape[1]
    return pl.pallas_call(
        _mm_down_kernel,
        grid=(M // tm, N // tn, K // tk),
        in_specs=[pl.BlockSpec((tm, tn), lambda i, j, k: (i, j)), pl.BlockSpec((tm, tk), lambda i, j, k: (i, k)),
                  pl.BlockSpec((tk, tn), lambda i, j, k: (k, j))],
        out_specs=pl.BlockSpec((tm, tn), lambda i, j, k: (i, j)),
        out_shape=jax.ShapeDtypeStruct((M, N), F32),
        compiler_params=_cparams(("parallel", "parallel", "arbitrary")),
        name="mm_down",
    )(x, a, w)


def _log_sigmoid(x):
    return jnp.minimum(x, 0.0) - jnp.log1p(jnp.exp(-jnp.abs(x)))


def _head_rms(x, g):
    return x * lax.rsqrt(jnp.mean(x * x, axis=-1, keepdims=True) + RMS_EPS) * g


def _prep_kernel(p_ref, gf_ref, gm_ref, gn_ref, fb_ref, *refs, tm, prompt, tiles_per_seq, n_alias):
    refs = refs[n_alias:]
    if prompt:
        (fox_ref, moba_ref, nsa_ref, win_ref, sm_ref, hm_ref, cmpin_ref, cc_ref, crow_ref, km_ref, carry_ref) = refs
    else:
        (fox_ref, moba_ref, nsa_ref, win_ref, sm_ref, hm_ref) = refs

    def seg(name, h):
        c0 = _SEG[name] + h * HEAD_DIM
        return p_ref[:, c0:c0 + HEAD_DIM]

    gfq, gfk = gf_ref[0:1, :], gf_ref[1:2, :]
    gmq, gmk = gm_ref[0:1, :], gm_ref[1:2, :]
    gnq, gnks, gnkw = gn_ref[0:1, :], gn_ref[2:3, :], gn_ref[3:4, :]
    qscale = SCALE * LOG2E if prompt else SCALE

    for h in range(8):
        hm_ref[_SLOT["fq"] + h] = (_head_rms(seg("fq", h), gfq) * qscale).astype(BF16)
        k = _head_rms(seg("fk", h), gfk)
        fox_ref[:, h * 128:(h + 1) * 128] = k
        hm_ref[_SLOT["fk"] + h] = k.astype(BF16)
        v = seg("fv", h)
        fox_ref[:, GW + h * 128:GW + (h + 1) * 128] = v
        hm_ref[_SLOT["fv"] + h] = v.astype(BF16)
        hm_ref[_SLOT["mq"] + h] = (_head_rms(seg("mq", h), gmq) * qscale).astype(BF16)
        hm_ref[_SLOT["nq"] + h] = (_head_rms(seg("nq", h), gnq) * qscale).astype(BF16)
    for h in range(4):
        k = _head_rms(seg("mk", h), gmk)
        moba_ref[:, h * 128:(h + 1) * 128] = k
        hm_ref[_SLOT["mk"] + h] = k.astype(BF16)
        if prompt:
            km_ref[h:h + 1, :] = jnp.mean(k, axis=0, keepdims=True)
        v = seg("mv", h)
        moba_ref[:, 512 + h * 128:512 + (h + 1) * 128] = v
        hm_ref[_SLOT["mv"] + h] = v.astype(BF16)
    for g in range(2):
        kc, vc = seg("nkc", g), seg("nvc", g)
        nsa_ref[:, g * 128:(g + 1) * 128] = kc
        nsa_ref[:, 256 + g * 128:256 + (g + 1) * 128] = vc
        if prompt:
            cmpin_ref[g] = kc
            cmpin_ref[2 + g] = vc
        ks = _head_rms(seg("nks", g), gnks)
        nsa_ref[:, 512 + g * 128:512 + (g + 1) * 128] = ks
        hm_ref[_SLOT["nks"] + g] = ks.astype(BF16)
        vs = seg("nvs", g)
        nsa_ref[:, 768 + g * 128:768 + (g + 1) * 128] = vs
        hm_ref[_SLOT["nvs"] + g] = vs.astype(BF16)
        kw = _head_rms(seg("nkw", g), gnkw)
        win_ref[:, g * 128:(g + 1) * 128] = kw
        hm_ref[_SLOT["nkw"] + g] = kw.astype(BF16)
        vw = seg("nvw", g)
        win_ref[:, 256 + g * 128:256 + (g + 1) * 128] = vw
        hm_ref[_SLOT["nvw"] + g] = vw.astype(BF16)

    sblk = p_ref[:, _SEG["small"]:_SEG["small"] + LANES]
    lane = _iota(sblk.shape, 1)
    logf = _log_sigmoid(sblk + fb_ref[...])
    gates = 1.0 / (1.0 + jnp.exp(-sblk))
    logf = jnp.where(lane < 8, logf, 0.0)
    sm_ref[...] = jnp.where(lane < 8, logf, jnp.where(lane < 32, gates, 0.0))

    if prompt:
        t = pl.program_id(0) % tiles_per_seq

        @pl.when(t == 0)
        def _():
            carry_ref[...] = jnp.zeros_like(carry_ref)

        tri = (_iota((tm, tm), 1) <= _iota((tm, tm), 0)).astype(BF16)
        c = _dot3_left(tri, logf) + carry_ref[...]
        carry_ref[...] = c[tm - 1:tm, :]
        c2 = c * LOG2E
        cc_ref[...] = c2
        crow_ref[...] = c2.T[0:8, :]


def _dot3_left(m, x):
    hi, mid, lo = _split3(x)
    return _dot(m, hi) + _dot(m, mid) + _dot(m, lo)


def _prep(proj, gf, gm, gn, fbias, *, B, T, prompt, layer=0, depth=1, stacks=None):
    M = B * T
    tm = 256 if prompt else M
    nt = M // tm
    row = lambda w: pl.BlockSpec((tm, w), lambda i: (i, 0))
    slab = lambda w: pl.BlockSpec((None, tm, w), lambda i: (layer, i, 0))
    full = lambda a: pl.BlockSpec(a.shape, lambda i: (0,) * a.ndim)
    fb = jnp.zeros((1, LANES), F32).at[0, :8].set(fbias)
    tiles_per_seq = T // tm if prompt else 1
    out_shape = [jax.ShapeDtypeStruct((depth, M, 2 * GW), F32), jax.ShapeDtypeStruct((depth, M, GW), F32),
                 jax.ShapeDtypeStruct((depth, M, GW), F32), jax.ShapeDtypeStruct((M, 512), F32),
                 jax.ShapeDtypeStruct((M, LANES), F32), jax.ShapeDtypeStruct((B, NSLOT, T, HEAD_DIM), BF16)]
    hm_map = (lambda i: (i // tiles_per_seq, 0, i % tiles_per_seq, 0))
    out_specs = [slab(2 * GW), slab(GW), slab(GW), row(512), row(LANES),
                 pl.BlockSpec((None, NSLOT, tm, HEAD_DIM), hm_map)]
    stacks = tuple(stacks or ())
    n_fixed = 5
    aliases = {n_fixed + k: k for k in range(len(stacks))}
    scratch = []
    if prompt:
        out_shape += [jax.ShapeDtypeStruct((B, 4, T, HEAD_DIM), F32), jax.ShapeDtypeStruct((M, LANES), F32),
                      jax.ShapeDtypeStruct((B, T // tm, 8, tm), F32),
                      jax.ShapeDtypeStruct((B, T // MOBA_BLOCK, 4, HEAD_DIM), F32)]
        out_specs += [pl.BlockSpec((None, 4, tm, HEAD_DIM), hm_map), row(LANES),
                      pl.BlockSpec((None, None, 8, tm), lambda i: (i // tiles_per_seq, i % tiles_per_seq, 0, 0)),
                      pl.BlockSpec((None, None, 4, HEAD_DIM), lambda i: (i // tiles_per_seq, i % tiles_per_seq, 0, 0))]
        scratch = [pltpu.VMEM((1, LANES), F32)]
    return pl.pallas_call(
        functools.partial(_prep_kernel, tm=tm, prompt=prompt, tiles_per_seq=tiles_per_seq, n_alias=len(stacks)),
        grid=(nt,),
        in_specs=[row(PW), full(gf), full(gm), full(gn), full(fb)] + [pl.BlockSpec(memory_space=pl.ANY)] * len(stacks),
        out_specs=out_specs,
        out_shape=out_shape,
        scratch_shapes=scratch,
        input_output_aliases=aliases,
        compiler_params=_cparams(("arbitrary",)),
        name="prep_prompt" if prompt else "prep_sample",
    )(proj, gf, gm, gn, fb, *stacks)


def _lane_pick(x, lane_idx):
    return jnp.sum(jnp.where(_iota(x.shape, 1) == lane_idx, x, 0.0), axis=1, keepdims=True)


def _attend_pieces(pieces):
    m = None
    for s, _ in pieces:
        mi = jnp.max(s, axis=1, keepdims=True)
        m = mi if m is None else jnp.maximum(m, mi)
    l, acc = None, None
    for s, v in pieces:
        e = jnp.exp2(s - m)
        li = jnp.sum(e, axis=1, keepdims=True)
        ai = _dot(e.astype(BF16), v)
        l = li if l is None else l + li
        acc = ai if acc is None else acc + ai
    return acc / jnp.maximum(l, 1e-30)


def _per_tile(qi, nq, body):
    for n in range(nq):
        pl.when(qi == n)(functools.partial(body, n))


def _causal_tile(tq):
    return _iota((1, tq), 1) <= _iota((tq, 1), 0)


HPS = 2


def _fox_kernel(q_ref, k_ref, v_ref, cc_ref, crow_ref, o_ref, *, tq):
    hp, qi = pl.program_id(1), pl.program_id(2)
    nq = k_ref.shape[1] // tq
    cc = cc_ref[...]

    def body(n):
        d0 = n * tq
        for i in range(HPS):
            h = hp * HPS + i
            q = q_ref[i]
            cq = _lane_pick(cc, h)
            sd = _dot_nt(q, k_ref[i, d0:d0 + tq, :]) + (cq - crow_ref[n, pl.ds(h, 1), :])
            pieces = [(jnp.where(_causal_tile(tq), sd, NEG), v_ref[i, d0:d0 + tq, :])]
            if n > 0:
                ck = jnp.concatenate([crow_ref[j, pl.ds(h, 1), :] for j in range(n)], axis=1)
                pieces.append((_dot_nt(q, k_ref[i, 0:d0, :]) + (cq - ck), v_ref[i, 0:d0, :]))
            o_ref[:, i * HEAD_DIM:(i + 1) * HEAD_DIM] = _attend_pieces(pieces).astype(o_ref.dtype)

    _per_tile(qi, nq, body)


def _fox_prompt(hm, cc, crow, *, B, T):
    tq = 256
    nq = T // tq
    slot = lambda s0: pl.BlockSpec((None, HPS, T, HEAD_DIM), lambda b, h, q: (b, s0 // HPS + h, 0, 0))
    return pl.pallas_call(
        functools.partial(_fox_kernel, tq=tq),
        grid=(B, 8 // HPS, nq),
        in_specs=[pl.BlockSpec((None, HPS, tq, HEAD_DIM), lambda b, h, q: (b, _SLOT["fq"] // HPS + h, q, 0)),
                  slot(_SLOT["fk"]), slot(_SLOT["fv"]),
                  pl.BlockSpec((tq, LANES), lambda b, h, q: (b * nq + q, 0)),
                  pl.BlockSpec((None, nq, 8, tq), lambda b, h, q: (b, 0, 0, 0))],
        out_specs=pl.BlockSpec((tq, HPS * HEAD_DIM), lambda b, h, q: (b * nq + q, h)),
        out_shape=jax.ShapeDtypeStruct((B * T, GW), BF16),
        compiler_params=_cparams(("parallel", "parallel", "arbitrary")),
        name="fox_prompt",
    )(hm, hm, hm, cc, crow)


def _rank_select(score, n_cand, own, n_keep):
    lane = _iota(score.shape, 1)
    rank = jnp.zeros(score.shape, jnp.int32)
    for j in range(n_cand):
        sj = score[:, j:j + 1]
        beats = (sj > score) | ((sj == score) & (j < lane))
        rank = rank + jnp.where(beats & (j < own), 1, 0)
    return ((lane < own) & (rank < n_keep)) | (lane == own)


def _moba_kernel(q_ref, k_ref, v_ref, km_ref, o_ref, *, tq):
    qi = pl.program_id(2)
    T = k_ref.shape[0]
    n_blk = T // MOBA_BLOCK
    km = km_ref[...].astype(BF16)
    pos = qi * tq + _iota((tq, 1), 0)
    own = pos // MOBA_BLOCK
    rep = q_ref.shape[0]
    sels = []
    for i in range(rep):
        gate = _dot_nt(q_ref[i], km)
        sels.append(jnp.where(_rank_select(gate, n_blk, own, MOBA_TOPK), 1.0, 0.0))

    def body(n):
        d0 = n * tq
        for i in range(rep):
            q, sel = q_ref[i], sels[i]
            sd = _dot_nt(q, k_ref[d0:d0 + tq, :])
            pieces = [(jnp.where(_causal_tile(tq), sd, NEG), v_ref[d0:d0 + tq, :])]
            if n > 0:
                keep = jnp.concatenate([jnp.broadcast_to(sel[:, j:j + 1], (tq, tq)) for j in range(n)], axis=1)
                pieces.append((jnp.where(keep > 0.5, _dot_nt(q, k_ref[0:d0, :]), NEG), v_ref[0:d0, :]))
            o_ref[:, i * HEAD_DIM:(i + 1) * HEAD_DIM] = _attend_pieces(pieces).astype(o_ref.dtype)

    _per_tile(qi, n_blk, body)


def _moba_prompt(hm, kmean, *, B, T):
    tq = MOBA_BLOCK
    nq = T // tq
    n_blk = T // MOBA_BLOCK
    n_kv = kmean.shape[1]
    rep = 8 // n_kv
    slot = lambda s0: pl.BlockSpec((None, None, T, HEAD_DIM), lambda b, h, q: (b, s0 + h, 0, 0))
    return pl.pallas_call(
        functools.partial(_moba_kernel, tq=tq),
        grid=(B, n_kv, nq),
        in_specs=[pl.BlockSpec((None, rep, tq, HEAD_DIM), lambda b, h, q: (b, _SLOT["mq"] // rep + h, q, 0)),
                  slot(_SLOT["mk"]), slot(_SLOT["mv"]),
                  pl.BlockSpec((None, None, n_blk, HEAD_DIM), lambda b, h, q: (b, h, 0, 0))],
        out_specs=pl.BlockSpec((tq, rep * HEAD_DIM), lambda b, h, q: (b * nq + q, h)),
        out_shape=jax.ShapeDtypeStruct((B * T, GW), BF16),
        compiler_params=_cparams(("parallel", "parallel", "arbitrary")),
        name="moba_prompt",
    )(hm, hm, hm, kmean)


def _sel_matrix(n_cmp_rows, n_cols, n_c):
    n = _iota((n_cmp_rows, n_cols), 0)
    j = _iota((n_cmp_rows, n_cols), 1)
    ratio = NSA_SEL_BLOCK // NSA_CMP_STRIDE
    lo = jnp.clip(ratio * j - NSA_CMP_BLOCK // NSA_CMP_STRIDE + 1, 0, n_c)
    hi = jnp.clip(ratio * (j + 1), 0, n_c)
    return ((n >= lo) & (n < hi)).astype(BF16)


def _nsa_kernel(q_ref, kc_ref, vc_ref, ks_ref, vs_ref, kw_ref, vw_ref, sm_ref, o_ref, *, tq):
    g, qi = pl.program_id(1), pl.program_id(2)
    T = ks_ref.shape[0]
    NC = kc_ref.shape[0]
    n_c = NC - 1
    n_sel = T // NSA_SEL_BLOCK
    R = q_ref.shape[0]
    pos = qi * tq + _iota((tq, 1), 0)

    kc = kc_ref[...].astype(BF16)
    vc = vc_ref[...].astype(BF16)
    ncol = _iota((1, NC), 1)
    mask_c = (ncol * NSA_CMP_STRIDE + (NSA_CMP_BLOCK - 1) <= pos) & (ncol < n_c)
    o_cmp = []
    psum = jnp.zeros((tq, NC), F32)
    for r in range(R):
        e, l = _softmax_unnorm(_dot_nt(q_ref[r], kc), mask_c, base2=True)
        p = e / l
        psum = psum + p
        o_cmp.append(_dot(p.astype(BF16), vc))
    imp = _dot3(psum, _sel_matrix(NC, LANES, n_c))
    own = pos // NSA_SEL_BLOCK
    sel = jnp.where(_rank_select(imp, n_sel, own, NSA_SEL_COUNT - 1), 1.0, 0.0).astype(BF16)
    gates = sm_ref[...]
    gate_cols = [[_lane_pick(gates, GATE_LANE0 + 8 * k + g * R + r) for k in range(3)] for r in range(R)]
    n_win = (NSA_WINDOW + tq - 1) // tq

    def body(n):
        d0 = n * tq
        causal = _causal_tile(tq)
        blk_of_key = _iota((LANES, d0 + tq), 1) // NSA_SEL_BLOCK
        selk = _dot(sel, (blk_of_key == _iota((LANES, d0 + tq), 0)).astype(BF16))
        mask_d = (selk[:, d0:d0 + tq] > 0.5) & causal
        w_lo = max(n - n_win, 0)
        bias_d = jnp.where(mask_d, 0.0, NEG)
        bias_a = jnp.where(selk[:, 0:d0] > 0.5, 0.0, NEG) if n > 0 else None
        bias_c = jnp.where(causal, 0.0, NEG)
        bias_w = {}
        for j in range(w_lo, n):
            if (n - j + 1) * tq - 1 >= NSA_WINDOW:
                diff = (n - j) * tq + _iota((tq, 1), 0) - _iota((1, tq), 1)
                bias_w[j] = jnp.where(diff < NSA_WINDOW, 0.0, NEG)
        for r in range(R):
            q = q_ref[r]
            pieces = [(_dot_nt(q, ks_ref[d0:d0 + tq, :]) + bias_d, vs_ref[d0:d0 + tq, :])]
            if n > 0:
                pieces.append((_dot_nt(q, ks_ref[0:d0, :]) + bias_a, vs_ref[0:d0, :]))
            o_sel = _attend_pieces(pieces)
            pieces = [(_dot_nt(q, kw_ref[d0:d0 + tq, :]) + bias_c, vw_ref[d0:d0 + tq, :])]
            for j in range(w_lo, n):
                s = _dot_nt(q, kw_ref[j * tq:(j + 1) * tq, :])
                if j in bias_w:
                    s = s + bias_w[j]
                pieces.append((s, vw_ref[j * tq:(j + 1) * tq, :]))
            o_win = _attend_pieces(pieces)
            g0, g1, g2 = gate_cols[r]
            o = g0 * o_cmp[r] + g1 * o_sel + g2 * o_win
            o_ref[:, r * HEAD_DIM:(r + 1) * HEAD_DIM] = o.astype(o_ref.dtype)

    _per_tile(qi, T // tq, body)


def _nsa_prompt(hm, cmp_kv, sm, *, B, T):
    tq = 256
    nq = T // tq
    G = NSA_KV_HEADS
    R = 8 // G
    NC = cmp_kv.shape[2]
    slot = lambda s0: pl.BlockSpec((None, None, T, HEAD_DIM), lambda b, g, q: (b, s0 + g, 0, 0))
    return pl.pallas_call(
        functools.partial(_nsa_kernel, tq=tq),
        grid=(B, G, nq),
        in_specs=[pl.BlockSpec((None, R, tq, HEAD_DIM), lambda b, g, q: (b, _SLOT["nq"] // R + g, q, 0)),
                  pl.BlockSpec((None, None, NC, HEAD_DIM), lambda b, g, q: (b, g, 0, 0)),
                  pl.BlockSpec((None, None, NC, HEAD_DIM), lambda b, g, q: (b, G + g, 0, 0)),
                  slot(_SLOT["nks"]), slot(_SLOT["nvs"]), slot(_SLOT["nkw"]), slot(_SLOT["nvw"]),
                  pl.BlockSpec((tq, LANES), lambda b, g, q: (b * nq + q, 0))],
        out_specs=pl.BlockSpec((tq, R * HEAD_DIM), lambda b, g, q: (b * nq + q, g)),
        out_shape=jax.ShapeDtypeStruct((B * T, GW), BF16),
        compiler_params=_cparams(("parallel", "parallel", "arbitrary")),
        name="nsa_prompt",
    )(hm, cmp_kv, cmp_kv, hm, hm, hm, hm, sm)


def _compress_kernel(c_ref, pe_ref, w1_ref, w2_ref, g_ref, o_ref):
    c = pl.program_id(1)
    x = c_ref[...]
    NC = x.shape[0]
    a = _dot((x + pe_ref[0]).astype(BF16), w1_ref[0])
    bm = _dot((x + pe_ref[1]).astype(BF16), w1_ref[1])
    pre = a + pltpu.roll(bm, NC - 1, 0)
    hid = pre * (1.0 / (1.0 + jnp.exp(-pre)))
    out = _dot(hid.astype(BF16), w2_ref[...])
    out = jnp.where(c < NSA_KV_HEADS, _head_rms(out, g_ref[...]), out)
    o_ref[...] = jnp.where(_iota(out.shape, 0) < NC - 1, out, 0.0)


def _compress(x, pe, w1, w2, gain):
    B, C, NC, W = x.shape
    G = NSA_KV_HEADS
    return pl.pallas_call(
        _compress_kernel,
        grid=(B, C),
        in_specs=[pl.BlockSpec((None, None, NC, W), lambda b, c: (b, c, 0, 0)),
                  pl.BlockSpec((None, 2, 1, W), lambda b, c: (c // G, 0, 0, 0)),
                  pl.BlockSpec((None, 2, W, HEAD_DIM), lambda b, c: (c // G, 0, 0, 0)),
                  pl.BlockSpec((None, HEAD_DIM, HEAD_DIM), lambda b, c: (c // G, 0, 0)),
                  pl.BlockSpec((1, HEAD_DIM), lambda b, c: (0, 0))],
        out_specs=pl.BlockSpec((None, None, NC, HEAD_DIM), lambda b, c: (b, c, 0, 0)),
        out_shape=jax.ShapeDtypeStruct((B, C, NC, HEAD_DIM), F32),
        compiler_params=_cparams(("parallel", "parallel")),
        name="nsa_compress",
    )(x, pe, w1, w2, gain)


def _pool_kernel(u_ref, halo_ref, w_ref, sc_ref, o_ref, ext_ref, *, tp):
    t = pl.program_id(1)
    HALO = halo_ref.shape[0]
    halo = halo_ref[...]
    ext_ref[0:HALO, :] = jnp.where(t > 0, halo, 0.0)
    ext_ref[HALO:HALO + tp, :] = u_ref[...]
    pos = t * tp + _iota((tp, 1), 0)
    PG = GW // len(POOL_WINDOWS)
    for gi, w in enumerate(POOL_WINDOWS):
        c0 = gi * PG
        u = ext_ref[HALO:HALO + tp, c0:c0 + PG]
        win = u
        for k in range(1, w):
            win = win + ext_ref[HALO - k:HALO - k + tp, c0:c0 + PG]
        cnt = jnp.minimum(w, pos + 1).astype(F32)
        d = win / cnt - u
        o = _dot(d.astype(BF16), w_ref[gi]) * sc_ref[:, c0:c0 + PG]
        o_ref[:, c0:c0 + PG] = o.astype(o_ref.dtype)


def _pool_prompt(proj, pool_w, pool_scale, *, B, T):
    tp = 512
    HALO = 16
    nt = T // tp
    cb = _SEG["pu"] // GW
    return pl.pallas_call(
        functools.partial(_pool_kernel, tp=tp),
        grid=(B, nt),
        in_specs=[pl.BlockSpec((tp, GW), lambda b, t: (b * nt + t, cb)),
                  pl.BlockSpec((HALO, GW), lambda b, t: (jnp.maximum((b * nt + t) * (tp // HALO) - 1, 0), cb)),
                  pl.BlockSpec(pool_w.shape, lambda b, t: (0, 0, 0)),
                  pl.BlockSpec((1, GW), lambda b, t: (0, 0))],
        out_specs=pl.BlockSpec((tp, GW), lambda b, t: (b * nt + t, 0)),
        out_shape=jax.ShapeDtypeStruct((B * T, GW), BF16),
        scratch_shapes=[pltpu.VMEM((HALO + tp, GW), F32)],
        compiler_params=_cparams(("parallel", "arbitrary")),
        name="pool_prompt",
    )(proj, proj, pool_w, pool_scale)


def _diag_col(row, lane0, n):
    b = jnp.broadcast_to(row, (n, row.shape[1]))
    keep = _iota(b.shape, 1) == _iota(b.shape, 0) + lane0
    return jnp.sum(jnp.where(keep, b, 0.0), axis=1, keepdims=True)


def _pad_rows(x, n):
    return jnp.concatenate([x, jnp.zeros((n - x.shape[0],) + x.shape[1:], x.dtype)], axis=0)


def _sfox_kernel(pt_ref, *refs, pps):
    kv_refs, lf_refs = refs[:pps], refs[pps:2 * pps]
    later_ref, q_ref, new_ref, sm_ref, o_ref, m_ref, l_ref, acc_ref, carry_ref = refs[2 * pps:]
    b, i = pl.program_id(0), pl.program_id(1)
    H = 8
    P = kv_refs[0].shape[0]
    q = q_ref[...]

    @pl.when(i == 0)
    def _():
        knew, vnew = new_ref[0:H, :], new_ref[H:2 * H, :]
        s_new = jnp.sum(q * knew, axis=1, keepdims=True)
        m_ref[...] = jnp.broadcast_to(s_new, m_ref.shape)
        l_ref[...] = jnp.ones_like(l_ref)
        acc_ref[...] = vnew
        carry_ref[...] = jnp.broadcast_to(_diag_col(sm_ref[pl.ds(b, 1), :], 0, H), carry_ref.shape)

    ones = jnp.ones((HEAD_DIM, HEAD_DIM), BF16)
    carry = carry_ref[...]
    scores = []
    for kv_ref, lf_ref in zip(kv_refs, lf_refs):
        lf = lf_ref[...]
        z = (kv_ref[:, 0] * q[None] + later_ref[...] * lf[None]).reshape(P * H, HEAD_DIM)
        hi = z.astype(BF16)
        mid = (z - hi.astype(F32)).astype(BF16)
        scores.append((_dot(hi, ones) + _dot(mid, ones)).reshape(P, H, HEAD_DIM) + carry[None])
        carry = carry + jnp.sum(lf, axis=1, keepdims=True)
    m_old = m_ref[...]
    m_new = m_old
    for s in scores:
        m_new = jnp.maximum(m_new, jnp.max(s, axis=0))
    alpha = jnp.exp(m_old - m_new)
    l = alpha * l_ref[...]
    acc = alpha * acc_ref[...]
    for s, kv_ref in zip(scores, kv_refs):
        p = jnp.exp(s - m_new[None])
        l = l + jnp.sum(p, axis=0)
        acc = acc + jnp.sum(p * kv_ref[:, 1], axis=0)
    l_ref[...] = l
    acc_ref[...] = acc
    m_ref[...] = m_new
    carry_ref[...] = carry

    @pl.when(i == pl.num_programs(1) - 1)
    def _():
        o_ref[...] = acc_ref[...] / jnp.maximum(l_ref[...], 1e-30)


def _fox_sample(layer, page_table, cache_kv, cache_lfT, hm_s, fox_new, sm_s, *, Bs):
    NP = page_table.shape[1]
    PS = cache_kv.shape[2]
    H = 8
    r = jnp.arange(PS, dtype=jnp.int32)
    later = jnp.broadcast_to((r[None, :] > r[:, None]).astype(F32)[:, None, :], (PS, H, PS))
    rep = pltpu.VMEM((H, HEAD_DIM), F32)
    pps = 8 if NP % 8 == 0 else 1
    page = lambda k: (lambda b, i, pt: pt[b, NP - 1 - (pps * i + k)])
    kv_spec = lambda k: pl.BlockSpec((None, None, PS, 2, H, HEAD_DIM),
                                     lambda b, i, pt: (layer, page(k)(b, i, pt), 0, 0, 0, 0))
    lf_spec = lambda k: pl.BlockSpec((None, None, H, PS), lambda b, i, pt: (layer, page(k)(b, i, pt), 0, 0))
    grid_spec = pltpu.PrefetchScalarGridSpec(
        num_scalar_prefetch=1,
        grid=(Bs, NP // pps),
        in_specs=[kv_spec(k) for k in range(pps)] + [lf_spec(k) for k in range(pps)]
        + [pl.BlockSpec((PS, H, PS), lambda b, i, pt: (0, 0, 0)),
           pl.BlockSpec((None, None, H, HEAD_DIM), lambda b, i, pt: (b, _SLOT["fq"] // H, 0, 0)),
           pl.BlockSpec((None, 2 * H, HEAD_DIM), lambda b, i, pt: (b, 0, 0)),
           pl.BlockSpec(sm_s.shape, lambda b, i, pt: (0, 0))],
        out_specs=pl.BlockSpec((None, H, HEAD_DIM), lambda b, i, pt: (b, 0, 0)),
        scratch_shapes=[rep, rep, rep, rep],
    )
    assert PS == HEAD_DIM
    return pl.pallas_call(
        functools.partial(_sfox_kernel, pps=pps),
        grid_spec=grid_spec,
        out_shape=jax.ShapeDtypeStruct((Bs, H, HEAD_DIM), F32),
        compiler_params=_cparams(("parallel", "arbitrary")),
        name="fox_sample",
    )(page_table, *([cache_kv] * pps), *([cache_lfT] * pps), later,
      hm_s.reshape(hm_s.shape[0], NSLOT // H, H, HEAD_DIM), fox_new, sm_s)


def _argmax_rounds(score, n_rounds, out_lanes):
    rows, L = score.shape
    lane = _iota(score.shape, 1)
    olane = _iota((rows, out_lanes), 1)
    out = jnp.zeros((rows, out_lanes), jnp.int32)
    for t in range(n_rounds):
        m = jnp.max(score, axis=1, keepdims=True)
        idx = jnp.min(jnp.where(score == m, lane, L), axis=1, keepdims=True)
        ok = jnp.where(m > NEG, 1, 0)
        out = jnp.where(olane == t, idx, out)
        out = jnp.where(olane == n_rounds + t, ok, out)
        score = jnp.where(lane == idx, NEG, score)
    return out


def _smoba_gate_kernel(pt_ref, *refs, ppb, bps):
    k_refs = refs[:bps * ppb]
    q_ref, o_ref, g_ref = refs[bps * ppb:]
    i = pl.program_id(1)
    H = 8

    @pl.when(i == 0)
    def _():
        g_ref[...] = jnp.full_like(g_ref, NEG)

    q = q_ref[...]
    g = g_ref[...]
    for jb in range(bps):
        ksum = jnp.sum(k_refs[jb * ppb][...], axis=0)
        for t in range(1, ppb):
            ksum = ksum + jnp.sum(k_refs[jb * ppb + t][...], axis=0)
        kmean = ksum / float(MOBA_BLOCK)
        kme = jnp.concatenate([kmean[h // 2:h // 2 + 1, :] for h in range(H)], axis=0)
        gate = jnp.sum(q * kme, axis=1, keepdims=True)
        g = jnp.where(_iota(g.shape, 1) == i * bps + jb, gate, g)
    g_ref[...] = g

    @pl.when(i == pl.num_programs(1) - 1)
    def _():
        o_ref[...] = _argmax_rounds(g, MOBA_TOPK, LANES)


def _moba_sample_gate(layer, page_table, cache_kv, hm_s, *, Bs):
    NP = page_table.shape[1]
    PS = cache_kv.shape[2]
    nkv = cache_kv.shape[4]
    ppb = MOBA_BLOCK // PS
    NB = NP // ppb
    bps = 4 if NB % 4 == 0 else 1
    assert NP % ppb == 0
    page = lambda j: pl.BlockSpec((None, None, PS, None, nkv, HEAD_DIM),
                                  lambda b, i, pt: (layer, pt[b, bps * ppb * i + j], 0, 0, 0, 0))
    grid_spec = pltpu.PrefetchScalarGridSpec(
        num_scalar_prefetch=1,
        grid=(Bs, NB // bps),
        in_specs=[page(j) for j in range(bps * ppb)]
        + [pl.BlockSpec((None, None, 8, HEAD_DIM), lambda b, i, pt: (b, _SLOT["mq"] // 8, 0, 0))],
        out_specs=pl.BlockSpec((None, 8, LANES), lambda b, i, pt: (b, 0, 0)),
        scratch_shapes=[pltpu.VMEM((8, max(LANES, NB)), F32)],
    )
    return pl.pallas_call(
        functools.partial(_smoba_gate_kernel, ppb=ppb, bps=bps),
        grid_spec=grid_spec,
        out_shape=jax.ShapeDtypeStruct((Bs, 8, LANES), jnp.int32),
        compiler_params=_cparams(("parallel", "arbitrary")),
        name="moba_sample_gate",
    )(page_table, *([cache_kv] * (bps * ppb)), hm_s.reshape(hm_s.shape[0], NSLOT // 8, 8, HEAD_DIM))


def _smoba_attn_kernel(pg_ref, ok_ref, *refs, n_pages, ppb):
    H = 8
    kv_refs = refs[:H]
    q_ref, new_ref, o_ref, ks_ref, vs_ref = refs[H:]
    b, j = pl.program_id(0), pl.program_id(1)
    for h in range(H):
        ks_ref[h, j] = kv_refs[h][:, 0, h // 2, :]
        vs_ref[h, j] = kv_refs[h][:, 1, h // 2, :]

    @pl.when(j == n_pages - 1)
    def _():
        for h in range(H):
            kvh = h // 2
            q = q_ref[h:h + 1, :]
            knew = new_ref[kvh:kvh + 1, :]
            vnew = new_ref[4 + kvh:5 + kvh, :]
            s_new = jnp.sum(q * knew, axis=1, keepdims=True)
            s, ok = [], []
            for t in range(n_pages):
                s.append(jnp.sum(ks_ref[h, t] * q, axis=1, keepdims=True))
                ok.append(ok_ref[(b * H + h) * (n_pages // ppb) + t // ppb] > 0)
            m = s_new
            for t in range(n_pages):
                m = jnp.maximum(m, jnp.max(jnp.where(ok[t], s[t], NEG), axis=0, keepdims=True))
            l = jnp.exp(s_new - m)
            acc = l * vnew
            for t in range(n_pages):
                e = jnp.where(ok[t], jnp.exp(s[t] - m), 0.0)
                l = l + jnp.sum(e, axis=0, keepdims=True)
                acc = acc + jnp.sum(e * vs_ref[h, t], axis=0, keepdims=True)
            o_ref[h:h + 1, :] = acc / jnp.maximum(l, 1e-30)


def _moba_sample_attn(layer, pages, oks, cache_kv, hm_s, moba_new, *, Bs):
    PS = cache_kv.shape[2]
    nkv = cache_kv.shape[4]
    ppb = MOBA_BLOCK // PS
    n_pages = MOBA_TOPK * ppb
    H = 8
    page = lambda h: pl.BlockSpec((None, None, PS, 2, nkv, HEAD_DIM),
                                  lambda b, j, pg, ok: (layer, pg[(b * H + h) * n_pages + j], 0, 0, 0, 0))
    grid_spec = pltpu.PrefetchScalarGridSpec(
        num_scalar_prefetch=2,
        grid=(Bs, n_pages),
        in_specs=[page(h) for h in range(H)]
        + [pl.BlockSpec((None, None, 8, HEAD_DIM), lambda b, j, pg, ok: (b, _SLOT["mq"] // 8, 0, 0)),
           pl.BlockSpec((None, 8, HEAD_DIM), lambda b, j, pg, ok: (b, 0, 0))],
        out_specs=pl.BlockSpec((None, 8, HEAD_DIM), lambda b, j, pg, ok: (b, 0, 0)),
        scratch_shapes=[pltpu.VMEM((H, n_pages, PS, HEAD_DIM), F32), pltpu.VMEM((H, n_pages, PS, HEAD_DIM), F32)],
    )
    return pl.pallas_call(
        functools.partial(_smoba_attn_kernel, n_pages=n_pages, ppb=ppb),
        grid_spec=grid_spec,
        out_shape=jax.ShapeDtypeStruct((Bs, 8, HEAD_DIM), F32),
        compiler_params=_cparams(("parallel", "arbitrary")),
        name="moba_sample_attn",
    )(pages, oks, *([cache_kv] * H), hm_s.reshape(hm_s.shape[0], NSLOT // 8, 8, HEAD_DIM), moba_new)


def _snsa_gather_kernel(pt_ref, *refs, pps):
    x_refs, o_ref = refs[:pps], refs[pps]
    PS = x_refs[0].shape[0]
    G = x_refs[0].shape[2]
    S = NSA_CMP_STRIDE
    cpp = PS // S
    for k in range(pps):
        for j in range(2):
            for g in range(G):
                for r in range(S):
                    o_ref[j * G + g, k * cpp:(k + 1) * cpp, r * HEAD_DIM:(r + 1) * HEAD_DIM] = (
                        x_refs[k][pl.ds(r, cpp, stride=S), j, g, :])


def _nsa_sample_gather(layer, page_table, cache_kv, *, Bs):
    NP = page_table.shape[1]
    PS = cache_kv.shape[2]
    G = cache_kv.shape[4]
    C = 2 * G
    S = NSA_CMP_STRIDE
    pps = 4 if NP % 4 == 0 else 1
    cpp = PS // S
    page = lambda k: pl.BlockSpec((None, None, PS, 2, G, HEAD_DIM),
                                  lambda b, i, pt: (layer, pt[b, pps * i + k], 0, 0, 0, 0))
    grid_spec = pltpu.PrefetchScalarGridSpec(
        num_scalar_prefetch=1,
        grid=(Bs, NP // pps),
        in_specs=[page(k) for k in range(pps)],
        out_specs=pl.BlockSpec((None, C, pps * cpp, S * HEAD_DIM), lambda b, i, pt: (b, 0, i, 0)),
    )
    return pl.pallas_call(
        functools.partial(_snsa_gather_kernel, pps=pps),
        grid_spec=grid_spec,
        out_shape=jax.ShapeDtypeStruct((Bs, C, NP * cpp, S * HEAD_DIM), F32),
        compiler_params=_cparams(("parallel", "arbitrary")),
        name="nsa_sample_gather",
    )(page_table, *([cache_kv] * pps))


def _snsa_cmp_kernel(q_ref, kc_ref, vc_ref, o_ref, sel_ref, *, n_sel_past):
    R = q_ref.shape[0]
    NC = kc_ref.shape[0]
    n_c = NC - 1
    q = _pad_rows(q_ref[...], 16).astype(BF16)
    s = _dot_nt(q, kc_ref[...].astype(BF16))[0:R, :]
    mask = _iota((1, NC), 1) < n_c
    e, l = _softmax_unnorm(s, mask)
    p = e / l
    o_ref[...] = _dot(_pad_rows(p, 16).astype(BF16), vc_ref[...].astype(BF16))[0:R, :]
    psum = jnp.sum(p, axis=0, keepdims=True)
    imp = _dot3(_pad_rows(psum, 16), _sel_matrix(NC, n_sel_past, n_c))[0:1, :]
    sel_ref[...] = _argmax_rounds(imp, NSA_SEL_COUNT - 1, LANES)


def _nsa_sample_cmp(hm_s, cmp_kv, *, Bs):
    G = NSA_KV_HEADS
    R = 8 // G
    NC = cmp_kv.shape[2]
    n_sel_past = NC * NSA_CMP_STRIDE // NSA_SEL_BLOCK
    return pl.pallas_call(
        functools.partial(_snsa_cmp_kernel, n_sel_past=n_sel_past),
        grid=(Bs, G),
        in_specs=[pl.BlockSpec((None, None, R, HEAD_DIM), lambda b, g: (b, _SLOT["nq"] // R + g, 0, 0)),
                  pl.BlockSpec((None, None, NC, HEAD_DIM), lambda b, g: (b, g, 0, 0)),
                  pl.BlockSpec((None, None, NC, HEAD_DIM), lambda b, g: (b, G + g, 0, 0))],
        out_specs=[pl.BlockSpec((None, None, R, HEAD_DIM), lambda b, g: (b, g, 0, 0)),
                   pl.BlockSpec((None, None, 1, LANES), lambda b, g: (b, g, 0, 0))],
        out_shape=[jax.ShapeDtypeStruct((Bs, G, R, HEAD_DIM), F32), jax.ShapeDtypeStruct((Bs, G, 1, LANES), jnp.int32)],
        compiler_params=_cparams(("parallel", "parallel")),
        name="nsa_sample_cmp",
    )(hm_s.reshape(hm_s.shape[0], NSLOT // R, R, HEAD_DIM), cmp_kv, cmp_kv)


def _snsa_final_kernel(pg_ref, hf_ref, ok_ref, *refs, n_sel):
    G = NSA_KV_HEADS
    blk_refs = refs[:G]
    q_ref, new_ref, win_ref, wnew_ref, ocmp_ref, sm_ref, o_ref, ks_ref, vs_ref = refs[G:]
    b, j = pl.program_id(0), pl.program_id(1)
    for g in range(G):
        ks_ref[g, j] = blk_refs[g][:, 0, g, :]
        vs_ref[g, j] = blk_refs[g][:, 1, g, :]

    @pl.when(j == n_sel - 1)
    def _():
        R = q_ref.shape[0] // G
        gates = sm_ref[pl.ds(b, 1), :]
        for g in range(G):
            qf = q_ref[g * R:(g + 1) * R, :]
            q = _pad_rows(qf, 16).astype(BF16)
            ks_new = new_ref[2 * G + g:2 * G + g + 1, :]
            vs_new = new_ref[3 * G + g:3 * G + g + 1, :]
            s_new = jnp.sum(qf * ks_new, axis=1, keepdims=True)
            s, ok = [], []
            for t in range(n_sel):
                s.append(_dot_nt(q, ks_ref[g, t].astype(BF16))[0:R, :])
                ok.append(ok_ref[(b * G + g) * n_sel + t] > 0)
            m = s_new
            for t in range(n_sel):
                m = jnp.maximum(m, jnp.max(jnp.where(ok[t], s[t], NEG), axis=1, keepdims=True))
            l = jnp.exp(s_new - m)
            acc = l * vs_new
            for t in range(n_sel):
                e = jnp.where(ok[t], jnp.exp(s[t] - m), 0.0)
                l = l + jnp.sum(e, axis=1, keepdims=True)
                acc = acc + _dot(_pad_rows(e, 16).astype(BF16), vs_ref[g, t].astype(BF16))[0:R, :]
            o_sel = acc / jnp.maximum(l, 1e-30)
            WB = win_ref.shape[0]
            kw_new = wnew_ref[g:g + 1, :]
            vw_new = wnew_ref[G + g:G + g + 1, :]
            sw = _dot_nt(q, win_ref[:, 0, g, :].astype(BF16))[0:R, :]
            mask_w = (WB - _iota((1, WB), 1)) < NSA_WINDOW
            sw_new = jnp.sum(qf * kw_new, axis=1, keepdims=True)
            mw = jnp.maximum(sw_new, jnp.max(jnp.where(mask_w, sw, NEG), axis=1, keepdims=True))
            ew = jnp.where(mask_w, jnp.exp(sw - mw), 0.0)
            ew_new = jnp.exp(sw_new - mw)
            lw = ew_new + jnp.sum(ew, axis=1, keepdims=True)
            pv = _dot(_pad_rows(ew, 16).astype(BF16), win_ref[:, 1, g, :].astype(BF16))[0:R, :]
            o_win = (ew_new * vw_new + pv) / jnp.maximum(lw, 1e-30)
            g0 = _diag_col(gates, GATE_LANE0 + g * R, R)
            g1 = _diag_col(gates, GATE_LANE0 + 8 + g * R, R)
            g2 = _diag_col(gates, GATE_LANE0 + 16 + g * R, R)
            o_ref[g * R:(g + 1) * R, :] = g0 * ocmp_ref[g * R:(g + 1) * R, :] + g1 * o_sel + g2 * o_win


def _nsa_sample_final(layer, pages, halves, oks, cache_kv, hm_s, nsa_new, win_state, win_new, o_cmp, sm_s, *, Bs):
    G = NSA_KV_HEADS
    n_sel = NSA_SEL_COUNT - 1
    SB = NSA_SEL_BLOCK
    WB = win_state.shape[2]
    idx = lambda b, g, j: (b * G + g) * n_sel + j
    blk = lambda g: pl.BlockSpec((None, None, SB, 2, G, HEAD_DIM),
                                 lambda b, j, pg, hf, ok: (layer, pg[idx(b, g, j)], hf[idx(b, g, j)], 1, 0, 0))
    grid_spec = pltpu.PrefetchScalarGridSpec(
        num_scalar_prefetch=3,
        grid=(Bs, n_sel),
        in_specs=[blk(g) for g in range(G)]
        + [pl.BlockSpec((None, None, 8, HEAD_DIM), lambda b, j, pg, hf, ok: (b, _SLOT["nq"] // 8, 0, 0)),
           pl.BlockSpec((None, 8, HEAD_DIM), lambda b, j, pg, hf, ok: (b, 0, 0)),
           pl.BlockSpec((None, None, WB, 2, G, HEAD_DIM), lambda b, j, pg, hf, ok: (layer, b, 0, 0, 0, 0)),
           pl.BlockSpec((None, 2 * G, HEAD_DIM), lambda b, j, pg, hf, ok: (b, 0, 0)),
           pl.BlockSpec((None, 8, HEAD_DIM), lambda b, j, pg, hf, ok: (b, 0, 0)),
           pl.BlockSpec(sm_s.shape, lambda b, j, pg, hf, ok: (0, 0))],
        out_specs=pl.BlockSpec((None, 8, HEAD_DIM), lambda b, j, pg, hf, ok: (b, 0, 0)),
        scratch_shapes=[pltpu.VMEM((G, n_sel, SB, HEAD_DIM), F32), pltpu.VMEM((G, n_sel, SB, HEAD_DIM), F32)],
    )
    return pl.pallas_call(
        functools.partial(_snsa_final_kernel, n_sel=n_sel),
        grid_spec=grid_spec,
        out_shape=jax.ShapeDtypeStruct((Bs, 8, HEAD_DIM), F32),
        compiler_params=_cparams(("parallel", "arbitrary")),
        name="nsa_sample_final",
    )(pages, halves, oks, *([cache_kv] * G), hm_s.reshape(hm_s.shape[0], NSLOT // 8, 8, HEAD_DIM), nsa_new, win_state,
      win_new, o_cmp.reshape(Bs, 8, HEAD_DIM), sm_s)


def _spool_kernel(st_ref, u_ref, w_ref, sc_ref, o_ref, *, Bs, pos0):
    u_all = u_ref[...]
    PG = GW // len(POOL_WINDOWS)
    NB = st_ref.shape[1]
    out = []
    for gi, w in enumerate(POOL_WINDOWS):
        c0 = gi * PG
        u = u_all[:, c0:c0 + PG]
        win = u
        for k in range(1, w):
            prev = st_ref[:, NB - k, c0:c0 + PG]
            if prev.shape[0] != u.shape[0]:
                prev = _pad_rows(prev, u.shape[0])
            win = win + prev
        d = win / float(min(w, pos0 + 1)) - u
        out.append(_dot(d.astype(BF16), w_ref[gi]) * sc_ref[:, c0:c0 + PG])
    o_ref[...] = jnp.concatenate(out, axis=1).astype(o_ref.dtype)


def _pool_sample(layer, state_pool, proj_s, pool_w, pool_scale, *, Bs, pos0):
    Ms = proj_s.shape[0]
    NB = state_pool.shape[2]
    assert NB >= max(POOL_WINDOWS) - 1 and pos0 >= NB
    cb = _SEG["pu"] // GW
    return pl.pallas_call(
        functools.partial(_spool_kernel, Bs=Bs, pos0=pos0),
        grid=(1,),
        in_specs=[pl.BlockSpec((None, Bs, NB, GW), lambda i: (layer, 0, 0, 0)),
                  pl.BlockSpec((Ms, GW), lambda i: (0, cb)),
                  pl.BlockSpec(pool_w.shape, lambda i: (0, 0, 0)),
                  pl.BlockSpec((1, GW), lambda i: (0, 0))],
        out_specs=pl.BlockSpec((Ms, GW), lambda i: (0, 0)),
        out_shape=jax.ShapeDtypeStruct((Ms, GW), BF16),
        compiler_params=_cparams(("arbitrary",)),
        name="pool_sample",
    )(state_pool, proj_s, pool_w, pool_scale)


def _layer_weights(l, norm1_g, w_in, fox_f_bias, fox_qk_g, moba_qk_g, nsa_qk_g, pool_w, pool_scale, cmp_pe, cmp_w1,
                   cmp_w2, w_out, norm2_g, w_up, w_down):
    D = w_in.shape[1]
    segs, c = {}, 0
    for name, n in _ORIG:
        segs[name] = w_in[l, :, c:c + n]
        c += n
    cols = [segs[n] for n in _NEW_ORDER]
    used = sum(x.shape[1] for x in cols)
    cols.append(jnp.zeros((D, PW - used), w_in.dtype))
    W = NSA_CMP_STRIDE * HEAD_DIM
    return dict(
        norm1_g=norm1_g[l], norm2_g=norm2_g[l],
        w_in=jnp.concatenate(cols, axis=1).astype(BF16),
        w_out=_cast_bf16(w_out, l), w_up=_cast_bf16(w_up, l), w_down=_cast_bf16(w_down, l),
        fox_f_bias=fox_f_bias[l], fox_qk_g=fox_qk_g[l], moba_qk_g=moba_qk_g[l], nsa_qk_g=nsa_qk_g[l],
        pool_w=pool_w[l].astype(BF16), pool_scale=pool_scale[l].reshape(1, GW),
        cmp_pe=cmp_pe[l].reshape(2, 2, 1, W), cmp_w1=cmp_w1[l].reshape(2, 2, W, HEAD_DIM).astype(BF16),
        cmp_w2=cmp_w2[l].astype(BF16), kc_gain=nsa_qk_g[l][1:2],
    )


def _mlp(x1, lw, *, tm):
    h2 = _rms(x1, lw["norm2_g"], tm=min(tm, 512))
    u = _matmul(h2, lw["w_up"], tm=tm, tn=1024, act="relu2", out_dtype=BF16, name="mm_up")
    return _matmul_down(x1, u, lw["w_down"], tm=tm, tn=1024, tk=4096)


def _prompt_layer(x, lw, *, B, T, layer, depth, stacks):
    tm = min(1024, B * T)
    h = _rms(x, lw["norm1_g"], tm=min(tm, 512))
    proj = _matmul(h, lw["w_in"], tm=tm, tn=1024, name="mm_in")
    fox_kv, moba_kv, nsa_kv, win, sm, hm, cmp_in, cc, crow, kmean = _prep(
        proj, lw["fox_qk_g"], lw["moba_qk_g"], lw["nsa_qk_g"], lw["fox_f_bias"], B=B, T=T, prompt=True,
        layer=layer, depth=depth, stacks=stacks)
    o_fox = _fox_prompt(hm, cc, crow, B=B, T=T)
    o_pool = _pool_prompt(proj, lw["pool_w"], lw["pool_scale"], B=B, T=T)
    o_moba = _moba_prompt(hm, jnp.swapaxes(kmean, 1, 2), B=B, T=T)
    cmp_in = cmp_in.reshape(B, 4, T // NSA_CMP_STRIDE, NSA_CMP_STRIDE * HEAD_DIM)
    cmp_kv = _compress(cmp_in, lw["cmp_pe"], lw["cmp_w1"], lw["cmp_w2"], lw["kc_gain"])
    o_nsa = _nsa_prompt(hm, cmp_kv, sm, B=B, T=T)
    x1 = _matmul_out(x, (o_fox, o_pool, o_moba, o_nsa), lw["w_out"], tm=tm, tn=1024)
    y = _mlp(x1, lw, tm=tm)
    wb = min(NSA_WINDOW, T)
    new = (sm[:, 0:8].reshape(B, T, 8), win.reshape(B, T, 2, NSA_KV_HEADS, HEAD_DIM)[:, T - wb:],
           proj.reshape(B, T, PW)[:, T - POOL_BUF:, _SEG["pu"]:_SEG["pu"] + GW])
    return y, new, (fox_kv, moba_kv, nsa_kv)


def _sample_layer(x, l, lw, caches, page_table, *, Bs):
    cache_fox_kv, cache_fox_lfT, cache_moba_kv, cache_nsa_kv, state_nsa_win, state_pool = caches
    Ms = x.shape[0]
    NP = page_table.shape[1]
    PS = cache_fox_kv.shape[2]
    past = NP * PS
    G = NSA_KV_HEADS
    h = _rms(x, lw["norm1_g"], tm=Ms)
    proj = _matmul(h, lw["w_in"], tm=Ms, tn=1024, name="mm_in_s")
    fox_kv, moba_kv, nsa_kv, win, sm, hm = _prep(
        proj, lw["fox_qk_g"], lw["moba_qk_g"], lw["nsa_qk_g"], lw["fox_f_bias"], B=1, T=Ms, prompt=False)
    fox_kv, moba_kv, nsa_kv = fox_kv[0], moba_kv[0], nsa_kv[0]
    hm_s = jnp.swapaxes(hm[0], 0, 1).astype(F32)
    fox_new = fox_kv.reshape(Ms, 16, HEAD_DIM)
    moba_new = moba_kv.reshape(Ms, 8, HEAD_DIM)
    nsa_new = nsa_kv.reshape(Ms, 8, HEAD_DIM)
    win_new = win.reshape(Ms, 4, HEAD_DIM)

    o_fox = _fox_sample(l, page_table, cache_fox_kv, cache_fox_lfT, hm_s, fox_new, sm, Bs=Bs)

    o_pool = _pool_sample(l, state_pool, proj, lw["pool_w"], lw["pool_scale"], Bs=Bs, pos0=past)

    top = _moba_sample_gate(l, page_table, cache_moba_kv, hm_s, Bs=Bs)
    ppb = MOBA_BLOCK // PS
    blk = top[:, :, 0:MOBA_TOPK]
    oks = top[:, :, MOBA_TOPK:2 * MOBA_TOPK]
    blk = jnp.where(oks > 0, blk, 0)
    pidx = (blk[..., None] * ppb + jnp.arange(ppb, dtype=jnp.int32)).reshape(Bs, 8 * MOBA_TOPK * ppb)
    pages = jnp.take_along_axis(page_table, pidx, axis=1).reshape(-1)
    o_moba = _moba_sample_attn(l, pages, oks.reshape(-1), cache_moba_kv, hm_s, moba_new, Bs=Bs)

    cmp_in = _nsa_sample_gather(l, page_table, cache_nsa_kv, Bs=Bs)
    cmp_kv = _compress(cmp_in, lw["cmp_pe"], lw["cmp_w1"], lw["cmp_w2"], lw["kc_gain"])
    o_cmp, sel = _nsa_sample_cmp(hm_s, cmp_kv, Bs=Bs)
    n_sel = NSA_SEL_COUNT - 1
    sblk = sel[:, :, 0, 0:n_sel]
    soks = sel[:, :, 0, n_sel:2 * n_sel]
    sblk = jnp.where(soks > 0, sblk, 0)
    spp = PS // NSA_SEL_BLOCK
    spages = jnp.take_along_axis(page_table, (sblk // spp).reshape(Bs, G * n_sel), axis=1).reshape(-1)
    o_nsa = _nsa_sample_final(l, spages, (sblk % spp).reshape(-1), soks.reshape(-1), cache_nsa_kv, hm_s, nsa_new,
                              state_nsa_win, win_new, o_cmp, sm, Bs=Bs)

    def rows(o):
        o = o.reshape(Bs, GW).astype(BF16)
        return jnp.concatenate([o, jnp.zeros((Ms - Bs, GW), BF16)], axis=0)

    x1 = _matmul_out(x, (rows(o_fox), o_pool, rows(o_moba), rows(o_nsa)), lw["w_out"], tm=Ms, tn=1024)
    y = _mlp(x1, lw, tm=Ms)
    new = (fox_kv[:Bs].reshape(Bs, 1, 2, 8, HEAD_DIM), sm[:Bs, 0:8].reshape(Bs, 1, 8),
           moba_kv[:Bs].reshape(Bs, 1, 2, 4, HEAD_DIM), nsa_kv[:Bs].reshape(Bs, 1, 4, G, HEAD_DIM),
           win[:Bs].reshape(Bs, 1, 2, G, HEAD_DIM), proj[:Bs, _SEG["pu"]:_SEG["pu"] + GW].reshape(Bs, 1, GW))
    return y, new


def kernel(x_prompt, x_sample, cache_fox_kv, cache_fox_logf, cache_moba_kv, cache_nsa_kv, state_nsa_win, state_pool,
           page_table, norm1_g, w_in, fox_f_bias, fox_qk_g, moba_qk_g, nsa_qk_g, pool_w, pool_scale, cmp_pe, cmp_w1,
           cmp_w2, w_out, norm2_g, w_up, w_down):
    B, T, D = x_prompt.shape
    Bs, Ts, _ = x_sample.shape
    assert Ts == 1 and D == N_MIXERS * GW
    depth = w_in.shape[0]
    n_pool, PS = cache_fox_kv.shape[1:3]
    WB = state_nsa_win.shape[2]
    Ms = 16
    assert Bs <= Ms

    caches = (cache_fox_kv, jnp.swapaxes(cache_fox_logf, 2, 3), cache_moba_kv, cache_nsa_kv, state_nsa_win, state_pool)

    xp = x_prompt.reshape(B * T, D)
    xs = jnp.concatenate([x_sample.reshape(Bs, D), jnp.zeros((Ms - Bs, D), x_sample.dtype)], axis=0)
    new_p, new_s, stacks = [], [], None
    for l in range(depth):
        lw = _layer_weights(l, norm1_g, w_in, fox_f_bias, fox_qk_g, moba_qk_g, nsa_qk_g, pool_w, pool_scale, cmp_pe,
                            cmp_w1, cmp_w2, w_out, norm2_g, w_up, w_down)
        xp, rows_p, stacks = _prompt_layer(xp, lw, B=B, T=T, layer=l, depth=depth, stacks=stacks)
        xs, rows_s = _sample_layer(xs, l, lw, caches, page_table, Bs=Bs)
        new_p.append(rows_p)
        new_s.append(rows_s)

    def stk(rows, i):
        return jnp.stack([r[i] for r in rows], axis=0)

    fox_kv_p, moba_kv_p, nsa_kv_p = stacks
    win_s = jnp.concatenate([state_nsa_win[:, :, 1:], stk(new_s, 4)], axis=2) if WB > 0 else stk(new_s, 4)[:, :, :0]
    pool_s = jnp.concatenate([state_pool[:, :, 1:], stk(new_s, 5)], axis=2)
    return (xp.reshape(B, T, D), xs[:Bs].reshape(Bs, 1, D),
            fox_kv_p.reshape(depth, B, T, 2, 8, HEAD_DIM), stk(new_s, 0), stk(new_p, 0), stk(new_s, 1),
            moba_kv_p.reshape(depth, B, T, 2, 4, HEAD_DIM), stk(new_s, 2),
            nsa_kv_p.reshape(depth, B, T, 4, NSA_KV_HEADS, HEAD_DIM), stk(new_s, 3),
            stk(new_p, 1), win_s, stk(new_p, 2), pool_s)
```

```python
import functools

import jax
import jax.numpy as jnp
from jax import lax
from jax.experimental import pallas as pl
from jax.experimental.pallas import tpu as pltpu

F32 = jnp.float32
BF16 = jnp.bfloat16

HEAD_DIM = 128
N_MIXERS = 4
FOX_FORGET_BIAS = 2.0
POOL_WINDOWS = (2, 4, 8, 16)
POOL_BUF = max(POOL_WINDOWS) - 1
MOBA_BLOCK = 256
MOBA_TOPK = 3
NSA_KV_HEADS = 2
NSA_CMP_BLOCK = 32
NSA_CMP_STRIDE = 16
NSA_SEL_BLOCK = 64
NSA_SEL_COUNT = 16
NSA_WINDOW = 512
RMS_EPS = 1e-6
SCALE = HEAD_DIM ** -0.5
LOG2E = 1.4426950408889634

LANES = 128
VMEM_LIMIT = 56 * 1024 * 1024
NEG = -1e30

GW = 1024
_SEG = dict(fq=0, fk=1024, fv=2048, pu=3072, mq=4096, mk=5120, mv=5632, nq=6144,
            nkc=7168, nvc=7424, nks=7680, nvs=7936, nkw=8192, nvw=8448, small=8704)
PW = 9216
_ORIG = (("fq", 1024), ("fk", 1024), ("fv", 1024), ("ff", 8), ("pu", 1024), ("mq", 1024), ("mk", 512),
         ("mv", 512), ("nq", 1024), ("nkc", 256), ("nvc", 256), ("nks", 256), ("nvs", 256), ("nkw", 256),
         ("nvw", 256), ("ng", 24))
_NEW_ORDER = ("fq", "fk", "fv", "pu", "mq", "mk", "mv", "nq", "nkc", "nvc", "nks", "nvs", "nkw", "nvw", "ff", "ng")
_SLOT = dict(fq=0, fk=8, fv=16, mq=24, mk=32, mv=36, nq=40, nks=48, nvs=50, nkw=52, nvw=54)
NSLOT = 56
GATE_LANE0 = 8


def _cparams(sem):
    return pltpu.CompilerParams(dimension_semantics=sem, vmem_limit_bytes=VMEM_LIMIT)


def _iota(shape, dim):
    return lax.broadcasted_iota(jnp.int32, shape, dim)


def _dot(a, b):
    return jnp.dot(a, b, preferred_element_type=F32)


def _dot_nt(a, b):
    return lax.dot_general(a, b, (((1,), (1,)), ((), ())), preferred_element_type=F32)


def _split3(x):
    hi = x.astype(BF16)
    r1 = x - hi.astype(F32)
    mid = r1.astype(BF16)
    lo = (r1 - mid.astype(F32)).astype(BF16)
    return hi, mid, lo


def _dot3(x, m):
    hi, mid, lo = _split3(x)
    return _dot(hi, m) + _dot(mid, m) + _dot(lo, m)


def _softmax_unnorm(s, mask, base2=False):
    sm = jnp.where(mask, s, NEG)
    m = jnp.max(sm, axis=-1, keepdims=True)
    e = jnp.where(mask, jnp.exp2(sm - m) if base2 else jnp.exp(sm - m), 0.0)
    l = jnp.maximum(jnp.sum(e, axis=-1, keepdims=True), 1e-30)
    return e, l


def _rms_kernel(x_ref, g_ref, o_ref):
    x = x_ref[...]
    y = x * lax.rsqrt(jnp.mean(x * x, axis=-1, keepdims=True) + RMS_EPS)
    o_ref[...] = (y * g_ref[...]).astype(o_ref.dtype)


def _rms(x, g, tm):
    M, D = x.shape
    return pl.pallas_call(
        _rms_kernel,
        grid=(M // tm,),
        in_specs=[pl.BlockSpec((tm, D), lambda i: (i, 0)), pl.BlockSpec((1, D), lambda i: (0, 0))],
        out_specs=pl.BlockSpec((tm, D), lambda i: (i, 0)),
        out_shape=jax.ShapeDtypeStruct((M, D), BF16),
        compiler_params=_cparams(("parallel",)),
        name="rms",
    )(x, g.reshape(1, D))


def _cast_kernel(x_ref, o_ref):
    o_ref[...] = x_ref[...].astype(o_ref.dtype)


def _cast_bf16(w_stack, l):
    _, K, N = w_stack.shape
    tk = max(8, min(K, (8 * 1024 * 1024) // (4 * N)))
    assert K % tk == 0
    return pl.pallas_call(
        _cast_kernel,
        grid=(K // tk,),
        in_specs=[pl.BlockSpec((None, tk, N), lambda i: (l, i, 0))],
        out_specs=pl.BlockSpec((tk, N), lambda i: (i, 0)),
        out_shape=jax.ShapeDtypeStruct((K, N), BF16),
        compiler_params=_cparams(("parallel",)),
        name="cast_bf16",
    )(w_stack)


def _mm_kernel(a_ref, w_ref, o_ref, *, act):
    acc = _dot(a_ref[...], w_ref[...])
    if act == "relu2":
        acc = jnp.maximum(acc, 0.0)
        acc = acc * acc
    o_ref[...] = acc.astype(o_ref.dtype)


def _matmul(a, w, *, tm, tn, act=None, out_dtype=F32, name="mm"):
    M, K = a.shape
    N = w.shape[1]
    return pl.pallas_call(
        functools.partial(_mm_kernel, act=act),
        grid=(M // tm, N // tn),
        in_specs=[pl.BlockSpec((tm, K), lambda i, j: (i, 0)), pl.BlockSpec((K, tn), lambda i, j: (0, j))],
        out_specs=pl.BlockSpec((tm, tn), lambda i, j: (i, j)),
        out_shape=jax.ShapeDtypeStruct((M, N), out_dtype),
        compiler_params=_cparams(("parallel", "parallel")),
        name=name,
    )(a, w)


def _mm_out_kernel(x_ref, a0, a1, a2, a3, w_ref, o_ref):
    acc = x_ref[...]
    for i, a in enumerate((a0, a1, a2, a3)):
        acc = acc + _dot(a[...], w_ref[i * GW:(i + 1) * GW, :])
    o_ref[...] = acc


def _matmul_out(x, parts, w, *, tm, tn):
    M, D = x.shape
    a_spec = pl.BlockSpec((tm, GW), lambda i, j: (i, 0))
    return pl.pallas_call(
        _mm_out_kernel,
        grid=(M // tm, D // tn),
        in_specs=[pl.BlockSpec((tm, tn), lambda i, j: (i, j)), a_spec, a_spec, a_spec, a_spec,
                  pl.BlockSpec((N_MIXERS * GW, tn), lambda i, j: (0, j))],
        out_specs=pl.BlockSpec((tm, tn), lambda i, j: (i, j)),
        out_shape=jax.ShapeDtypeStruct((M, D), F32),
        compiler_params=_cparams(("parallel", "parallel")),
        name="mm_out",
    )(x, *parts, w)


def _mm_down_kernel(x_ref, a_ref, w_ref, o_ref):
    @pl.when(pl.program_id(2) == 0)
    def _():
        o_ref[...] = x_ref[...]

    o_ref[...] += _dot(a_ref[...], w_ref[...])


def _matmul_down(x, a, w, *, tm, tn, tk):
    M, K = a.shape
    N = w.shape[1]
    return pl.pallas_call(
        _mm_down_kernel,
        grid=(M // tm, N // tn, K // tk),
        in_specs=[pl.BlockSpec((tm, tn), lambda i, j, k: (i, j)), pl.BlockSpec((tm, tk), lambda i, j, k: (i, k)),
                  pl.BlockSpec((tk, tn), lambda i, j, k: (k, j))],
        out_specs=pl.BlockSpec((tm, tn), lambda i, j, k: (i, j)),
        out_shape=jax.ShapeDtypeStruct((M, N), F32),
        compiler_params=_cparams(("parallel", "parallel", "arbitrary")),
        name="mm_down",
    )(x, a, w)


def _log_sigmoid(x):
    return jnp.minimum(x, 0.0) - jnp.log1p(jnp.exp(-jnp.abs(x)))


def _head_rms(x, g):
    return x * lax.rsqrt(jnp.mean(x * x, axis=-1, keepdims=True) + RMS_EPS) * g


def _prep_kernel(p_ref, gf_ref, gm_ref, gn_ref, fb_ref, *refs, tm, prompt, tiles_per_seq, n_alias):
    refs = refs[n_alias:]
    if prompt:
        (fox_ref, moba_ref, nsa_ref, win_ref, sm_ref, hm_ref, cmpin_ref, cc_ref, crow_ref, km_ref, carry_ref) = refs
    else:
        (fox_ref, moba_ref, nsa_ref, win_ref, sm_ref, hm_ref) = refs

    def seg(name, h):
        c0 = _SEG[name] + h * HEAD_DIM
        return p_ref[:, c0:c0 + HEAD_DIM]

    gfq, gfk = gf_ref[0:1, :], gf_ref[1:2, :]
    gmq, gmk = gm_ref[0:1, :], gm_ref[1:2, :]
    gnq, gnks, gnkw = gn_ref[0:1, :], gn_ref[2:3, :], gn_ref[3:4, :]
    qscale = SCALE * LOG2E if prompt else SCALE

    for h in range(8):
        hm_ref[_SLOT["fq"] + h] = (_head_rms(seg("fq", h), gfq) * qscale).astype(BF16)
        k = _head_rms(seg("fk", h), gfk)
        fox_ref[:, h * 128:(h + 1) * 128] = k
        hm_ref[_SLOT["fk"] + h] = k.astype(BF16)
        v = seg("fv", h)
        fox_ref[:, GW + h * 128:GW + (h + 1) * 128] = v
        hm_ref[_SLOT["fv"] + h] = v.astype(BF16)
        hm_ref[_SLOT["mq"] + h] = (_head_rms(seg("mq", h), gmq) * qscale).astype(BF16)
        hm_ref[_SLOT["nq"] + h] = (_head_rms(seg("nq", h), gnq) * qscale).astype(BF16)
    for h in range(4):
        k = _head_rms(seg("mk", h), gmk)
        moba_ref[:, h * 128:(h + 1) * 128] = k
        hm_ref[_SLOT["mk"] + h] = k.astype(BF16)
        if prompt:
            km_ref[h:h + 1, :] = jnp.mean(k, axis=0, keepdims=True)
        v = seg("mv", h)
        moba_ref[:, 512 + h * 128:512 + (h + 1) * 128] = v
        hm_ref[_SLOT["mv"] + h] = v.astype(BF16)
    for g in range(2):
        kc, vc = seg("nkc", g), seg("nvc", g)
        nsa_ref[:, g * 128:(g + 1) * 128] = kc
        nsa_ref[:, 256 + g * 128:256 + (g + 1) * 128] = vc
        if prompt:
            cmpin_ref[g] = kc
            cmpin_ref[2 + g] = vc
        ks = _head_rms(seg("nks", g), gnks)
        nsa_ref[:, 512 + g * 128:512 + (g + 1) * 128] = ks
        hm_ref[_SLOT["nks"] + g] = ks.astype(BF16)
        vs = seg("nvs", g)
        nsa_ref[:, 768 + g * 128:768 + (g + 1) * 128] = vs
        hm_ref[_SLOT["nvs"] + g] = vs.astype(BF16)
        kw = _head_rms(seg("nkw", g), gnkw)
        win_ref[:, g * 128:(g + 1) * 128] = kw
        hm_ref[_SLOT["nkw"] + g] = kw.astype(BF16)
        vw = seg("nvw", g)
        win_ref[:, 256 + g * 128:256 + (g + 1) * 128] = vw
        hm_ref[_SLOT["nvw"] + g] = vw.astype(BF16)

    sblk = p_ref[:, _SEG["small"]:_SEG["small"] + LANES]
    lane = _iota(sblk.shape, 1)
    logf = _log_sigmoid(sblk + fb_ref[...])
    gates = 1.0 / (1.0 + jnp.exp(-sblk))
    logf = jnp.where(lane < 8, logf, 0.0)
    sm_ref[...] = jnp.where(lane < 8, logf, jnp.where(lane < 32, gates, 0.0))

    if prompt:
        t = pl.program_id(0) % tiles_per_seq

        @pl.when(t == 0)
        def _():
            carry_ref[...] = jnp.zeros_like(carry_ref)

        tri = (_iota((tm, tm), 1) <= _iota((tm, tm), 0)).astype(BF16)
        c = _dot3_left(tri, logf) + carry_ref[...]
        carry_ref[...] = c[tm - 1:tm, :]
        c2 = c * LOG2E
        cc_ref[...] = c2
        crow_ref[...] = c2.T[0:8, :]


def _dot3_left(m, x):
    hi, mid, lo = _split3(x)
    return _dot(m, hi) + _dot(m, mid) + _dot(m, lo)


def _prep(proj, gf, gm, gn, fbias, *, B, T, prompt, layer=0, depth=1, stacks=None):
    M = B * T
    tm = 256 if prompt else M
    nt = M // tm
    row = lambda w: pl.BlockSpec((tm, w), lambda i: (i, 0))
    slab = lambda w: pl.BlockSpec((None, tm, w), lambda i: (layer, i, 0))
    full = lambda a: pl.BlockSpec(a.shape, lambda i: (0,) * a.ndim)
    fb = jnp.zeros((1, LANES), F32).at[0, :8].set(fbias)
    tiles_per_seq = T // tm if prompt else 1
    out_shape = [jax.ShapeDtypeStruct((depth, M, 2 * GW), F32), jax.ShapeDtypeStruct((depth, M, GW), F32),
                 jax.ShapeDtypeStruct((depth, M, GW), F32), jax.ShapeDtypeStruct((M, 512), F32),
                 jax.ShapeDtypeStruct((M, LANES), F32), jax.ShapeDtypeStruct((B, NSLOT, T, HEAD_DIM), BF16)]
    hm_map = (lambda i: (i // tiles_per_seq, 0, i % tiles_per_seq, 0))
    out_specs = [slab(2 * GW), slab(GW), slab(GW), row(512), row(LANES),
                 pl.BlockSpec((None, NSLOT, tm, HEAD_DIM), hm_map)]
    stacks = tuple(stacks or ())
    n_fixed = 5
    aliases = {n_fixed + k: k for k in range(len(stacks))}
    scratch = []
    if prompt:
        out_shape += [jax.ShapeDtypeStruct((B, 4, T, HEAD_DIM), F32), jax.ShapeDtypeStruct((M, LANES), F32),
                      jax.ShapeDtypeStruct((B, T // tm, 8, tm), F32),
                      jax.ShapeDtypeStruct((B, T // MOBA_BLOCK, 4, HEAD_DIM), F32)]
        out_specs += [pl.BlockSpec((None, 4, tm, HEAD_DIM), hm_map), row(LANES),
                      pl.BlockSpec((None, None, 8, tm), lambda i: (i // tiles_per_seq, i % tiles_per_seq, 0, 0)),
                      pl.BlockSpec((None, None, 4, HEAD_DIM), lambda i: (i // tiles_per_seq, i % tiles_per_seq, 0, 0))]
        scratch = [pltpu.VMEM((1, LANES), F32)]
    return pl.pallas_call(
        functools.partial(_prep_kernel, tm=tm, prompt=prompt, tiles_per_seq=tiles_per_seq, n_alias=len(stacks)),
        grid=(nt,),
        in_specs=[row(PW), full(gf), full(gm), full(gn), full(fb)] + [pl.BlockSpec(memory_space=pl.ANY)] * len(stacks),
        out_specs=out_specs,
        out_shape=out_shape,
        scratch_shapes=scratch,
        input_output_aliases=aliases,
        compiler_params=_cparams(("arbitrary",)),
        name="prep_prompt" if prompt else "prep_sample",
    )(proj, gf, gm, gn, fb, *stacks)


def _lane_pick(x, lane_idx):
    return jnp.sum(jnp.where(_iota(x.shape, 1) == lane_idx, x, 0.0), axis=1, keepdims=True)


def _attend_pieces(pieces):
    m = None
    for s, _ in pieces:
        mi = jnp.max(s, axis=1, keepdims=True)
        m = mi if m is None else jnp.maximum(m, mi)
    l, acc = None, None
    for s, v in pieces:
        e = jnp.exp2(s - m)
        li = jnp.sum(e, axis=1, keepdims=True)
        ai = _dot(e.astype(BF16), v)
        l = li if l is None else l + li
        acc = ai if acc is None else acc + ai
    return acc / jnp.maximum(l, 1e-30)


def _per_tile(qi, nq, body):
    for n in range(nq):
        pl.when(qi == n)(functools.partial(body, n))


def _causal_tile(tq):
    return _iota((1, tq), 1) <= _iota((tq, 1), 0)


HPS = 4


def _fox_kernel(q_ref, k_ref, v_ref, cc_ref, crow_ref, o_ref, *, tq):
    hp, qi = pl.program_id(1), pl.program_id(2)
    nq = k_ref.shape[1] // tq
    cc = cc_ref[...]

    def body(n):
        d0 = n * tq
        for i in range(HPS):
            h = hp * HPS + i
            q = q_ref[i]
            cq = _lane_pick(cc, h)
            sd = _dot_nt(q, k_ref[i, d0:d0 + tq, :]) + (cq - crow_ref[n, pl.ds(h, 1), :])
            pieces = [(jnp.where(_causal_tile(tq), sd, NEG), v_ref[i, d0:d0 + tq, :])]
            if n > 0:
                ck = jnp.concatenate([crow_ref[j, pl.ds(h, 1), :] for j in range(n)], axis=1)
                pieces.append((_dot_nt(q, k_ref[i, 0:d0, :]) + (cq - ck), v_ref[i, 0:d0, :]))
            o_ref[:, i * HEAD_DIM:(i + 1) * HEAD_DIM] = _attend_pieces(pieces).astype(o_ref.dtype)

    _per_tile(qi, nq, body)


def _fox_prompt(hm, cc, crow, *, B, T):
    tq = 256
    nq = T // tq
    slot = lambda s0: pl.BlockSpec((None, HPS, T, HEAD_DIM), lambda b, h, q: (b, s0 // HPS + h, 0, 0))
    return pl.pallas_call(
        functools.partial(_fox_kernel, tq=tq),
        grid=(B, 8 // HPS, nq),
        in_specs=[pl.BlockSpec((None, HPS, tq, HEAD_DIM), lambda b, h, q: (b, _SLOT["fq"] // HPS + h, q, 0)),
                  slot(_SLOT["fk"]), slot(_SLOT["fv"]),
                  pl.BlockSpec((tq, LANES), lambda b, h, q: (b * nq + q, 0)),
                  pl.BlockSpec((None, nq, 8, tq), lambda b, h, q: (b, 0, 0, 0))],
        out_specs=pl.BlockSpec((tq, HPS * HEAD_DIM), lambda b, h, q: (b * nq + q, h)),
        out_shape=jax.ShapeDtypeStruct((B * T, GW), BF16),
        compiler_params=_cparams(("parallel", "parallel", "arbitrary")),
        name="fox_prompt",
    )(hm, hm, hm, cc, crow)


def _rank_select(score, n_cand, own, n_keep):
    lane = _iota(score.shape, 1)
    rank = jnp.zeros(score.shape, jnp.int32)
    for j in range(n_cand):
        sj = score[:, j:j + 1]
        beats = (sj > score) | ((sj == score) & (j < lane))
        rank = rank + jnp.where(beats & (j < own), 1, 0)
    return ((lane < own) & (rank < n_keep)) | (lane == own)


def _moba_kernel(q_ref, k_ref, v_ref, km_ref, o_ref, *, tq):
    qi = pl.program_id(2)
    kps, T = k_ref.shape[0], k_ref.shape[1]
    n_blk = T // MOBA_BLOCK
    pos = qi * tq + _iota((tq, 1), 0)
    own = pos // MOBA_BLOCK
    rep = q_ref.shape[0] // kps
    sels = []
    for i in range(kps * rep):
        gate = _dot_nt(q_ref[i], km_ref[i // rep].astype(BF16))
        sels.append(jnp.where(_rank_select(gate, n_blk, own, MOBA_TOPK), 1.0, 0.0))

    def body(n):
        d0 = n * tq
        for i in range(kps * rep):
            q, sel, kv = q_ref[i], sels[i], i // rep
            sd = _dot_nt(q, k_ref[kv, d0:d0 + tq, :])
            pieces = [(jnp.where(_causal_tile(tq), sd, NEG), v_ref[kv, d0:d0 + tq, :])]
            if n > 0:
                keep = jnp.concatenate([jnp.broadcast_to(sel[:, j:j + 1], (tq, tq)) for j in range(n)], axis=1)
                pieces.append((jnp.where(keep > 0.5, _dot_nt(q, k_ref[kv, 0:d0, :]), NEG), v_ref[kv, 0:d0, :]))
            o_ref[:, i * HEAD_DIM:(i + 1) * HEAD_DIM] = _attend_pieces(pieces).astype(o_ref.dtype)

    _per_tile(qi, n_blk, body)


def _moba_prompt(hm, kmean, *, B, T):
    tq = MOBA_BLOCK
    nq = T // tq
    n_blk = T // MOBA_BLOCK
    n_kv = kmean.shape[1]
    rep = 8 // n_kv
    kps = 2
    slot = lambda s0: pl.BlockSpec((None, kps, T, HEAD_DIM), lambda b, h, q: (b, s0 // kps + h, 0, 0))
    return pl.pallas_call(
        functools.partial(_moba_kernel, tq=tq),
        grid=(B, n_kv // kps, nq),
        in_specs=[pl.BlockSpec((None, kps * rep, tq, HEAD_DIM), lambda b, h, q: (b, _SLOT["mq"] // (kps * rep) + h, q, 0)),
                  slot(_SLOT["mk"]), slot(_SLOT["mv"]),
                  pl.BlockSpec((None, kps, n_blk, HEAD_DIM), lambda b, h, q: (b, h, 0, 0))],
        out_specs=pl.BlockSpec((tq, kps * rep * HEAD_DIM), lambda b, h, q: (b * nq + q, h)),
        out_shape=jax.ShapeDtypeStruct((B * T, GW), BF16),
        compiler_params=_cparams(("parallel", "parallel", "arbitrary")),
        name="moba_prompt",
    )(hm, hm, hm, kmean)


def _sel_matrix(n_cmp_rows, n_cols, n_c):
    n = _iota((n_cmp_rows, n_cols), 0)
    j = _iota((n_cmp_rows, n_cols), 1)
    ratio = NSA_SEL_BLOCK // NSA_CMP_STRIDE
    lo = jnp.clip(ratio * j - NSA_CMP_BLOCK // NSA_CMP_STRIDE + 1, 0, n_c)
    hi = jnp.clip(ratio * (j + 1), 0, n_c)
    return ((n >= lo) & (n < hi)).astype(BF16)


def _nsa_kernel(q_ref, kc_ref, vc_ref, ks_ref, vs_ref, kw_ref, vw_ref, sm_ref, o_ref, *, tq):
    g, qi = pl.program_id(1), pl.program_id(2)
    T = ks_ref.shape[0]
    NC = kc_ref.shape[0]
    n_c = NC - 1
    n_sel = T // NSA_SEL_BLOCK
    R = q_ref.shape[0]
    pos = qi * tq + _iota((tq, 1), 0)

    kc = kc_ref[...].astype(BF16)
    vc = vc_ref[...].astype(BF16)
    ncol = _iota((1, NC), 1)
    mask_c = (ncol * NSA_CMP_STRIDE + (NSA_CMP_BLOCK - 1) <= pos) & (ncol < n_c)
    o_cmp = []
    psum = jnp.zeros((tq, NC), F32)
    for r in range(R):
        e, l = _softmax_unnorm(_dot_nt(q_ref[r], kc), mask_c, base2=True)
        p = e / l
        psum = psum + p
        o_cmp.append(_dot(p.astype(BF16), vc))
    imp = _dot3(psum, _sel_matrix(NC, LANES, n_c))
    own = pos // NSA_SEL_BLOCK
    sel = jnp.where(_rank_select(imp, n_sel, own, NSA_SEL_COUNT - 1), 1.0, 0.0).astype(BF16)
    gates = sm_ref[...]
    gate_cols = [[_lane_pick(gates, GATE_LANE0 + 8 * k + g * R + r) for k in range(3)] for r in range(R)]
    n_win = (NSA_WINDOW + tq - 1) // tq

    def body(n):
        d0 = n * tq
        causal = _causal_tile(tq)
        blk_of_key = _iota((LANES, d0 + tq), 1) // NSA_SEL_BLOCK
        selk = _dot(sel, (blk_of_key == _iota((LANES, d0 + tq), 0)).astype(BF16))
        mask_d = (selk[:, d0:d0 + tq] > 0.5) & causal
        w_lo = max(n - n_win, 0)
        bias_d = jnp.where(mask_d, 0.0, NEG)
        bias_a = jnp.where(selk[:, 0:d0] > 0.5, 0.0, NEG) if n > 0 else None
        bias_c = jnp.where(causal, 0.0, NEG)
        bias_w = {}
        for j in range(w_lo, n):
            if (n - j + 1) * tq - 1 >= NSA_WINDOW:
                diff = (n - j) * tq + _iota((tq, 1), 0) - _iota((1, tq), 1)
                bias_w[j] = jnp.where(diff < NSA_WINDOW, 0.0, NEG)
        for r in range(R):
            q = q_ref[r]
            pieces = [(_dot_nt(q, ks_ref[d0:d0 + tq, :]) + bias_d, vs_ref[d0:d0 + tq, :])]
            if n > 0:
                pieces.append((_dot_nt(q, ks_ref[0:d0, :]) + bias_a, vs_ref[0:d0, :]))
            o_sel = _attend_pieces(pieces)
            pieces = [(_dot_nt(q, kw_ref[d0:d0 + tq, :]) + bias_c, vw_ref[d0:d0 + tq, :])]
            for j in range(w_lo, n):
                s = _dot_nt(q, kw_ref[j * tq:(j + 1) * tq, :])
                if j in bias_w:
                    s = s + bias_w[j]
                pieces.append((s, vw_ref[j * tq:(j + 1) * tq, :]))
            o_win = _attend_pieces(pieces)
            g0, g1, g2 = gate_cols[r]
            o = g0 * o_cmp[r] + g1 * o_sel + g2 * o_win
            o_ref[:, r * HEAD_DIM:(r + 1) * HEAD_DIM] = o.astype(o_ref.dtype)

    _per_tile(qi, T // tq, body)


def _nsa_prompt(hm, cmp_kv, sm, *, B, T):
    tq = 256
    nq = T // tq
    G = NSA_KV_HEADS
    R = 8 // G
    NC = cmp_kv.shape[2]
    slot = lambda s0: pl.BlockSpec((None, None, T, HEAD_DIM), lambda b, g, q: (b, s0 + g, 0, 0))
    return pl.pallas_call(
        functools.partial(_nsa_kernel, tq=tq),
        grid=(B, G, nq),
        in_specs=[pl.BlockSpec((None, R, tq, HEAD_DIM), lambda b, g, q: (b, _SLOT["nq"] // R + g, q, 0)),
                  pl.BlockSpec((None, None, NC, HEAD_DIM), lambda b, g, q: (b, g, 0, 0)),
                  pl.BlockSpec((None, None, NC, HEAD_DIM), lambda b, g, q: (b, G + g, 0, 0)),
                  slot(_SLOT["nks"]), slot(_SLOT["nvs"]), slot(_SLOT["nkw"]), slot(_SLOT["nvw"]),
                  pl.BlockSpec((tq, LANES), lambda b, g, q: (b * nq + q, 0))],
        out_specs=pl.BlockSpec((tq, R * HEAD_DIM), lambda b, g, q: (b * nq + q, g)),
        out_shape=jax.ShapeDtypeStruct((B * T, GW), BF16),
        compiler_params=_cparams(("parallel", "parallel", "arbitrary")),
        name="nsa_prompt",
    )(hm, cmp_kv, cmp_kv, hm, hm, hm, hm, sm)


def _compress_kernel(c_ref, pe_ref, w1_ref, w2_ref, g_ref, o_ref):
    c = pl.program_id(1)
    x = c_ref[...]
    NC = x.shape[0]
    a = _dot((x + pe_ref[0]).astype(BF16), w1_ref[0])
    bm = _dot((x + pe_ref[1]).astype(BF16), w1_ref[1])
    pre = a + pltpu.roll(bm, NC - 1, 0)
    hid = pre * (1.0 / (1.0 + jnp.exp(-pre)))
    out = _dot(hid.astype(BF16), w2_ref[...])
    out = jnp.where(c < NSA_KV_HEADS, _head_rms(out, g_ref[...]), out)
    o_ref[...] = jnp.where(_iota(out.shape, 0) < NC - 1, out, 0.0)


def _compress(x, pe, w1, w2, gain):
    B, C, NC, W = x.shape
    G = NSA_KV_HEADS
    return pl.pallas_call(
        _compress_kernel,
        grid=(B, C),
        in_specs=[pl.BlockSpec((None, None, NC, W), lambda b, c: (b, c, 0, 0)),
                  pl.BlockSpec((None, 2, 1, W), lambda b, c: (c // G, 0, 0, 0)),
                  pl.BlockSpec((None, 2, W, HEAD_DIM), lambda b, c: (c // G, 0, 0, 0)),
                  pl.BlockSpec((None, HEAD_DIM, HEAD_DIM), lambda b, c: (c // G, 0, 0)),
                  pl.BlockSpec((1, HEAD_DIM), lambda b, c: (0, 0))],
        out_specs=pl.BlockSpec((None, None, NC, HEAD_DIM), lambda b, c: (b, c, 0, 0)),
        out_shape=jax.ShapeDtypeStruct((B, C, NC, HEAD_DIM), F32),
        compiler_params=_cparams(("parallel", "parallel")),
        name="nsa_compress",
    )(x, pe, w1, w2, gain)


def _pool_kernel(u_ref, halo_ref, w_ref, sc_ref, o_ref, ext_ref, *, tp):
    t = pl.program_id(1)
    HALO = halo_ref.shape[0]
    halo = halo_ref[...]
    ext_ref[0:HALO, :] = jnp.where(t > 0, halo, 0.0)
    ext_ref[HALO:HALO + tp, :] = u_ref[...]
    pos = t * tp + _iota((tp, 1), 0)
    PG = GW // len(POOL_WINDOWS)
    for gi, w in enumerate(POOL_WINDOWS):
        c0 = gi * PG
        u = ext_ref[HALO:HALO + tp, c0:c0 + PG]
        win = u
        for k in range(1, w):
            win = win + ext_ref[HALO - k:HALO - k + tp, c0:c0 + PG]
        cnt = jnp.minimum(w, pos + 1).astype(F32)
        d = win / cnt - u
        o = _dot(d.astype(BF16), w_ref[gi]) * sc_ref[:, c0:c0 + PG]
        o_ref[:, c0:c0 + PG] = o.astype(o_ref.dtype)


def _pool_prompt(proj, pool_w, pool_scale, *, B, T):
    tp = 512
    HALO = 16
    nt = T // tp
    cb = _SEG["pu"] // GW
    return pl.pallas_call(
        functools.partial(_pool_kernel, tp=tp),
        grid=(B, nt),
        in_specs=[pl.BlockSpec((tp, GW), lambda b, t: (b * nt + t, cb)),
                  pl.BlockSpec((HALO, GW), lambda b, t: (jnp.maximum((b * nt + t) * (tp // HALO) - 1, 0), cb)),
                  pl.BlockSpec(pool_w.shape, lambda b, t: (0, 0, 0)),
                  pl.BlockSpec((1, GW), lambda b, t: (0, 0))],
        out_specs=pl.BlockSpec((tp, GW), lambda b, t: (b * nt + t, 0)),
        out_shape=jax.ShapeDtypeStruct((B * T, GW), BF16),
        scratch_shapes=[pltpu.VMEM((HALO + tp, GW), F32)],
        compiler_params=_cparams(("parallel", "arbitrary")),
        name="pool_prompt",
    )(proj, proj, pool_w, pool_scale)


def _diag_col(row, lane0, n):
    b = jnp.broadcast_to(row, (n, row.shape[1]))
    keep = _iota(b.shape, 1) == _iota(b.shape, 0) + lane0
    return jnp.sum(jnp.where(keep, b, 0.0), axis=1, keepdims=True)


def _pad_rows(x, n):
    return jnp.concatenate([x, jnp.zeros((n - x.shape[0],) + x.shape[1:], x.dtype)], axis=0)


def _sfox_kernel(pt_ref, *refs, pps):
    kv_refs, lf_refs = refs[:pps], refs[pps:2 * pps]
    later_ref, q_ref, new_ref, sm_ref, o_ref, m_ref, l_ref, acc_ref, carry_ref = refs[2 * pps:]
    b, i = pl.program_id(0), pl.program_id(1)
    H = 8
    P = kv_refs[0].shape[0]
    q = q_ref[...]

    @pl.when(i == 0)
    def _():
        knew, vnew = new_ref[0:H, :], new_ref[H:2 * H, :]
        s_new = jnp.sum(q * knew, axis=1, keepdims=True)
        m_ref[...] = jnp.broadcast_to(s_new, m_ref.shape)
        l_ref[...] = jnp.ones_like(l_ref)
        acc_ref[...] = vnew
        carry_ref[...] = jnp.broadcast_to(_diag_col(sm_ref[pl.ds(b, 1), :], 0, H), carry_ref.shape)

    ones = jnp.ones((HEAD_DIM, HEAD_DIM), BF16)
    carry = carry_ref[...]
    scores = []
    for kv_ref, lf_ref in zip(kv_refs, lf_refs):
        lf = lf_ref[...]
        z = (kv_ref[:, 0] * q[None] + later_ref[...] * lf[None]).reshape(P * H, HEAD_DIM)
        hi = z.astype(BF16)
        mid = (z - hi.astype(F32)).astype(BF16)
        scores.append((_dot(hi, ones) + _dot(mid, ones)).reshape(P, H, HEAD_DIM) + carry[None])
        carry = carry + jnp.sum(lf, axis=1, keepdims=True)
    m_old = m_ref[...]
    m_new = m_old
    for s in scores:
        m_new = jnp.maximum(m_new, jnp.max(s, axis=0))
    alpha = jnp.exp(m_old - m_new)
    l = alpha * l_ref[...]
    acc = alpha * acc_ref[...]
    for s, kv_ref in zip(scores, kv_refs):
        p = jnp.exp(s - m_new[None])
        l = l + jnp.sum(p, axis=0)
        acc = acc + jnp.sum(p * kv_ref[:, 1], axis=0)
    l_ref[...] = l
    acc_ref[...] = acc
    m_ref[...] = m_new
    carry_ref[...] = carry

    @pl.when(i == pl.num_programs(1) - 1)
    def _():
        o_ref[...] = acc_ref[...] / jnp.maximum(l_ref[...], 1e-30)


def _fox_sample(layer, page_table, cache_kv, cache_lfT, hm_s, fox_new, sm_s, *, Bs):
    NP = page_table.shape[1]
    PS = cache_kv.shape[2]
    H = 8
    r = jnp.arange(PS, dtype=jnp.int32)
    later = jnp.broadcast_to((r[None, :] > r[:, None]).astype(F32)[:, None, :], (PS, H, PS))
    rep = pltpu.VMEM((H, HEAD_DIM), F32)
    pps = 8 if NP % 8 == 0 else 1
    page = lambda k: (lambda b, i, pt: pt[b, NP - 1 - (pps * i + k)])
    kv_spec = lambda k: pl.BlockSpec((None, None, PS, 2, H, HEAD_DIM),
                                     lambda b, i, pt: (layer, page(k)(b, i, pt), 0, 0, 0, 0))
    lf_spec = lambda k: pl.BlockSpec((None, None, H, PS), lambda b, i, pt: (layer, page(k)(b, i, pt), 0, 0))
    grid_spec = pltpu.PrefetchScalarGridSpec(
        num_scalar_prefetch=1,
        grid=(Bs, NP // pps),
        in_specs=[kv_spec(k) for k in range(pps)] + [lf_spec(k) for k in range(pps)]
        + [pl.BlockSpec((PS, H, PS), lambda b, i, pt: (0, 0, 0)),
           pl.BlockSpec((None, None, H, HEAD_DIM), lambda b, i, pt: (b, _SLOT["fq"] // H, 0, 0)),
           pl.BlockSpec((None, 2 * H, HEAD_DIM), lambda b, i, pt: (b, 0, 0)),
           pl.BlockSpec(sm_s.shape, lambda b, i, pt: (0, 0))],
        out_specs=pl.BlockSpec((None, H, HEAD_DIM), lambda b, i, pt: (b, 0, 0)),
        scratch_shapes=[rep, rep, rep, rep],
    )
    assert PS == HEAD_DIM
    return pl.pallas_call(
        functools.partial(_sfox_kernel, pps=pps),
        grid_spec=grid_spec,
        out_shape=jax.ShapeDtypeStruct((Bs, H, HEAD_DIM), F32),
        compiler_params=_cparams(("parallel", "arbitrary")),
        name="fox_sample",
    )(page_table, *([cache_kv] * pps), *([cache_lfT] * pps), later,
      hm_s.reshape(hm_s.shape[0], NSLOT // H, H, HEAD_DIM), fox_new, sm_s)


def _argmax_rounds(score, n_rounds, out_lanes):
    rows, L = score.shape
    lane = _iota(score.shape, 1)
    olane = _iota((rows, out_lanes), 1)
    out = jnp.zeros((rows, out_lanes), jnp.int32)
    for t in range(n_rounds):
        m = jnp.max(score, axis=1, keepdims=True)
        idx = jnp.min(jnp.where(score == m, lane, L), axis=1, keepdims=True)
        ok = jnp.where(m > NEG, 1, 0)
        out = jnp.where(olane == t, idx, out)
        out = jnp.where(olane == n_rounds + t, ok, out)
        score = jnp.where(lane == idx, NEG, score)
    return out


def _smoba_gate_kernel(pt_ref, *refs, ppb, bps):
    k_refs = refs[:bps * ppb]
    q_ref, o_ref, g_ref = refs[bps * ppb:]
    i = pl.program_id(1)
    H = 8

    @pl.when(i == 0)
    def _():
        g_ref[...] = jnp.full_like(g_ref, NEG)

    q = q_ref[...]
    g = g_ref[...]
    for jb in range(bps):
        ksum = jnp.sum(k_refs[jb * ppb][...], axis=0)
        for t in range(1, ppb):
            ksum = ksum + jnp.sum(k_refs[jb * ppb + t][...], axis=0)
        kmean = ksum / float(MOBA_BLOCK)
        kme = jnp.concatenate([kmean[h // 2:h // 2 + 1, :] for h in range(H)], axis=0)
        gate = jnp.sum(q * kme, axis=1, keepdims=True)
        g = jnp.where(_iota(g.shape, 1) == i * bps + jb, gate, g)
    g_ref[...] = g

    @pl.when(i == pl.num_programs(1) - 1)
    def _():
        o_ref[...] = _argmax_rounds(g, MOBA_TOPK, LANES)


def _moba_sample_gate(layer, page_table, cache_kv, hm_s, *, Bs):
    NP = page_table.shape[1]
    PS = cache_kv.shape[2]
    nkv = cache_kv.shape[4]
    ppb = MOBA_BLOCK // PS
    NB = NP // ppb
    bps = 8 if NB % 8 == 0 else (4 if NB % 4 == 0 else 1)
    assert NP % ppb == 0
    page = lambda j: pl.BlockSpec((None, None, PS, None, nkv, HEAD_DIM),
                                  lambda b, i, pt: (layer, pt[b, bps * ppb * i + j], 0, 0, 0, 0))
    grid_spec = pltpu.PrefetchScalarGridSpec(
        num_scalar_prefetch=1,
        grid=(Bs, NB // bps),
        in_specs=[page(j) for j in range(bps * ppb)]
        + [pl.BlockSpec((None, None, 8, HEAD_DIM), lambda b, i, pt: (b, _SLOT["mq"] // 8, 0, 0))],
        out_specs=pl.BlockSpec((None, 8, LANES), lambda b, i, pt: (b, 0, 0)),
        scratch_shapes=[pltpu.VMEM((8, max(LANES, NB)), F32)],
    )
    return pl.pallas_call(
        functools.partial(_smoba_gate_kernel, ppb=ppb, bps=bps),
        grid_spec=grid_spec,
        out_shape=jax.ShapeDtypeStruct((Bs, 8, LANES), jnp.int32),
        compiler_params=_cparams(("parallel", "arbitrary")),
        name="moba_sample_gate",
    )(page_table, *([cache_kv] * (bps * ppb)), hm_s.reshape(hm_s.shape[0], NSLOT // 8, 8, HEAD_DIM))


def _smoba_attn_kernel(pg_ref, ok_ref, *refs, n_pages, ppb):
    H = 8
    kv_refs = refs[:H]
    q_ref, new_ref, o_ref, ks_ref, vs_ref = refs[H:]
    b, j = pl.program_id(0), pl.program_id(1)
    for h in range(H):
        ks_ref[h, j] = kv_refs[h][:, 0, h // 2, :]
        vs_ref[h, j] = kv_refs[h][:, 1, h // 2, :]

    @pl.when(j == n_pages - 1)
    def _():
        for h in range(H):
            kvh = h // 2
            q = q_ref[h:h + 1, :]
            knew = new_ref[kvh:kvh + 1, :]
            vnew = new_ref[4 + kvh:5 + kvh, :]
            s_new = jnp.sum(q * knew, axis=1, keepdims=True)
            s, ok = [], []
            for t in range(n_pages):
                s.append(jnp.sum(ks_ref[h, t] * q, axis=1, keepdims=True))
                ok.append(ok_ref[(b * H + h) * (n_pages // ppb) + t // ppb] > 0)
            m = s_new
            for t in range(n_pages):
                m = jnp.maximum(m, jnp.max(jnp.where(ok[t], s[t], NEG), axis=0, keepdims=True))
            l = jnp.exp(s_new - m)
            acc = l * vnew
            for t in range(n_pages):
                e = jnp.where(ok[t], jnp.exp(s[t] - m), 0.0)
                l = l + jnp.sum(e, axis=0, keepdims=True)
                acc = acc + jnp.sum(e * vs_ref[h, t], axis=0, keepdims=True)
            o_ref[h:h + 1, :] = acc / jnp.maximum(l, 1e-30)


def _moba_sample_attn(layer, pages, oks, cache_kv, hm_s, moba_new, *, Bs):
    PS = cache_kv.shape[2]
    nkv = cache_kv.shape[4]
    ppb = MOBA_BLOCK // PS
    n_pages = MOBA_TOPK * ppb
    H = 8
    page = lambda h: pl.BlockSpec((None, None, PS, 2, nkv, HEAD_DIM),
                                  lambda b, j, pg, ok: (layer, pg[(b * H + h) * n_pages + j], 0, 0, 0, 0))
    grid_spec = pltpu.PrefetchScalarGridSpec(
        num_scalar_prefetch=2,
        grid=(Bs, n_pages),
        in_specs=[page(h) for h in range(H)]
        + [pl.BlockSpec((None, None, 8, HEAD_DIM), lambda b, j, pg, ok: (b, _SLOT["mq"] // 8, 0, 0)),
           pl.BlockSpec((None, 8, HEAD_DIM), lambda b, j, pg, ok: (b, 0, 0))],
        out_specs=pl.BlockSpec((None, 8, HEAD_DIM), lambda b, j, pg, ok: (b, 0, 0)),
        scratch_shapes=[pltpu.VMEM((H, n_pages, PS, HEAD_DIM), F32), pltpu.VMEM((H, n_pages, PS, HEAD_DIM), F32)],
    )
    return pl.pallas_call(
        functools.partial(_smoba_attn_kernel, n_pages=n_pages, ppb=ppb),
        grid_spec=grid_spec,
        out_shape=jax.ShapeDtypeStruct((Bs, 8, HEAD_DIM), F32),
        compiler_params=_cparams(("parallel", "arbitrary")),
        name="moba_sample_attn",
    )(pages, oks, *([cache_kv] * H), hm_s.reshape(hm_s.shape[0], NSLOT // 8, 8, HEAD_DIM), moba_new)


def _snsa_gather_kernel(pt_ref, *refs, pps):
    x_refs, o_ref = refs[:pps], refs[pps]
    PS = x_refs[0].shape[0]
    G = x_refs[0].shape[2]
    S = NSA_CMP_STRIDE
    cpp = PS // S
    for k in range(pps):
        for j in range(2):
            for g in range(G):
                for r in range(S):
                    o_ref[j * G + g, k * cpp:(k + 1) * cpp, r * HEAD_DIM:(r + 1) * HEAD_DIM] = (
                        x_refs[k][pl.ds(r, cpp, stride=S), j, g, :])


def _nsa_sample_gather(layer, page_table, cache_kv, *, Bs):
    NP = page_table.shape[1]
    PS = cache_kv.shape[2]
    G = cache_kv.shape[4]
    C = 2 * G
    S = NSA_CMP_STRIDE
    pps = 8 if NP % 8 == 0 else (4 if NP % 4 == 0 else 1)
    cpp = PS // S
    page = lambda k: pl.BlockSpec((None, None, PS, 2, G, HEAD_DIM),
                                  lambda b, i, pt: (layer, pt[b, pps * i + k], 0, 0, 0, 0))
    grid_spec = pltpu.PrefetchScalarGridSpec(
        num_scalar_prefetch=1,
        grid=(Bs, NP // pps),
        in_specs=[page(k) for k in range(pps)],
        out_specs=pl.BlockSpec((None, C, pps * cpp, S * HEAD_DIM), lambda b, i, pt: (b, 0, i, 0)),
    )
    return pl.pallas_call(
        functools.partial(_snsa_gather_kernel, pps=pps),
        grid_spec=grid_spec,
        out_shape=jax.ShapeDtypeStruct((Bs, C, NP * cpp, S * HEAD_DIM), F32),
        compiler_params=_cparams(("parallel", "arbitrary")),
        name="nsa_sample_gather",
    )(page_table, *([cache_kv] * pps))


def _snsa_cmp_kernel(q_ref, kc_ref, vc_ref, o_ref, sel_ref, *, n_sel_past):
    R = q_ref.shape[0]
    NC = kc_ref.shape[0]
    n_c = NC - 1
    q = _pad_rows(q_ref[...], 16).astype(BF16)
    s = _dot_nt(q, kc_ref[...].astype(BF16))[0:R, :]
    mask = _iota((1, NC), 1) < n_c
    e, l = _softmax_unnorm(s, mask)
    p = e / l
    o_ref[...] = _dot(_pad_rows(p, 16).astype(BF16), vc_ref[...].astype(BF16))[0:R, :]
    psum = jnp.sum(p, axis=0, keepdims=True)
    imp = _dot3(_pad_rows(psum, 16), _sel_matrix(NC, n_sel_past, n_c))[0:1, :]
    sel_ref[...] = _argmax_rounds(imp, NSA_SEL_COUNT - 1, LANES)


def _nsa_sample_cmp(hm_s, cmp_kv, *, Bs):
    G = NSA_KV_HEADS
    R = 8 // G
    NC = cmp_kv.shape[2]
    n_sel_past = NC * NSA_CMP_STRIDE // NSA_SEL_BLOCK
    return pl.pallas_call(
        functools.partial(_snsa_cmp_kernel, n_sel_past=n_sel_past),
        grid=(Bs, G),
        in_specs=[pl.BlockSpec((None, None, R, HEAD_DIM), lambda b, g: (b, _SLOT["nq"] // R + g, 0, 0)),
                  pl.BlockSpec((None, None, NC, HEAD_DIM), lambda b, g: (b, g, 0, 0)),
                  pl.BlockSpec((None, None, NC, HEAD_DIM), lambda b, g: (b, G + g, 0, 0))],
        out_specs=[pl.BlockSpec((None, None, R, HEAD_DIM), lambda b, g: (b, g, 0, 0)),
                   pl.BlockSpec((None, None, 1, LANES), lambda b, g: (b, g, 0, 0))],
        out_shape=[jax.ShapeDtypeStruct((Bs, G, R, HEAD_DIM), F32), jax.ShapeDtypeStruct((Bs, G, 1, LANES), jnp.int32)],
        compiler_params=_cparams(("parallel", "parallel")),
        name="nsa_sample_cmp",
    )(hm_s.reshape(hm_s.shape[0], NSLOT // R, R, HEAD_DIM), cmp_kv, cmp_kv)


def _snsa_final_kernel(pg_ref, hf_ref, ok_ref, *refs, n_sel):
    G = NSA_KV_HEADS
    blk_refs = refs[:G]
    q_ref, new_ref, win_ref, wnew_ref, ocmp_ref, sm_ref, o_ref, ks_ref, vs_ref = refs[G:]
    b, j = pl.program_id(0), pl.program_id(1)
    for g in range(G):
        ks_ref[g, j] = blk_refs[g][:, 0, g, :]
        vs_ref[g, j] = blk_refs[g][:, 1, g, :]

    @pl.when(j == n_sel - 1)
    def _():
        R = q_ref.shape[0] // G
        gates = sm_ref[pl.ds(b, 1), :]
        for g in range(G):
            qf = q_ref[g * R:(g + 1) * R, :]
            q = _pad_rows(qf, 16).astype(BF16)
            ks_new = new_ref[2 * G + g:2 * G + g + 1, :]
            vs_new = new_ref[3 * G + g:3 * G + g + 1, :]
            s_new = jnp.sum(qf * ks_new, axis=1, keepdims=True)
            s, ok = [], []
            for t in range(n_sel):
                s.append(_dot_nt(q, ks_ref[g, t].astype(BF16))[0:R, :])
                ok.append(ok_ref[(b * G + g) * n_sel + t] > 0)
            m = s_new
            for t in range(n_sel):
                m = jnp.maximum(m, jnp.max(jnp.where(ok[t], s[t], NEG), axis=1, keepdims=True))
            l = jnp.exp(s_new - m)
            acc = l * vs_new
            for t in range(n_sel):
                e = jnp.where(ok[t], jnp.exp(s[t] - m), 0.0)
                l = l + jnp.sum(e, axis=1, keepdims=True)
                acc = acc + _dot(_pad_rows(e, 16).astype(BF16), vs_ref[g, t].astype(BF16))[0:R, :]
            o_sel = acc / jnp.maximum(l, 1e-30)
            WB = win_ref.shape[0]
            kw_new = wnew_ref[g:g + 1, :]
            vw_new = wnew_ref[G + g:G + g + 1, :]
            sw = _dot_nt(q, win_ref[:, 0, g, :].astype(BF16))[0:R, :]
            mask_w = (WB - _iota((1, WB), 1)) < NSA_WINDOW
            sw_new = jnp.sum(qf * kw_new, axis=1, keepdims=True)
            mw = jnp.maximum(sw_new, jnp.max(jnp.where(mask_w, sw, NEG), axis=1, keepdims=True))
            ew = jnp.where(mask_w, jnp.exp(sw - mw), 0.0)
            ew_new = jnp.exp(sw_new - mw)
            lw = ew_new + jnp.sum(ew, axis=1, keepdims=True)
            pv = _dot(_pad_rows(ew, 16).astype(BF16), win_ref[:, 1, g, :].astype(BF16))[0:R, :]
            o_win = (ew_new * vw_new + pv) / jnp.maximum(lw, 1e-30)
            g0 = _diag_col(gates, GATE_LANE0 + g * R, R)
            g1 = _diag_col(gates, GATE_LANE0 + 8 + g * R, R)
            g2 = _diag_col(gates, GATE_LANE0 + 16 + g * R, R)
            o_ref[g * R:(g + 1) * R, :] = g0 * ocmp_ref[g * R:(g + 1) * R, :] + g1 * o_sel + g2 * o_win


def _nsa_sample_final(layer, pages, halves, oks, cache_kv, hm_s, nsa_new, win_state, win_new, o_cmp, sm_s, *, Bs):
    G = NSA_KV_HEADS
    n_sel = NSA_SEL_COUNT - 1
    SB = NSA_SEL_BLOCK
    WB = win_state.shape[2]
    idx = lambda b, g, j: (b * G + g) * n_sel + j
    blk = lambda g: pl.BlockSpec((None, None, SB, 2, G, HEAD_DIM),
                                 lambda b, j, pg, hf, ok: (layer, pg[idx(b, g, j)], hf[idx(b, g, j)], 1, 0, 0))
    grid_spec = pltpu.PrefetchScalarGridSpec(
        num_scalar_prefetch=3,
        grid=(Bs, n_sel),
        in_specs=[blk(g) for g in range(G)]
        + [pl.BlockSpec((None, None, 8, HEAD_DIM), lambda b, j, pg, hf, ok: (b, _SLOT["nq"] // 8, 0, 0)),
           pl.BlockSpec((None, 8, HEAD_DIM), lambda b, j, pg, hf, ok: (b, 0, 0)),
           pl.BlockSpec((None, None, WB, 2, G, HEAD_DIM), lambda b, j, pg, hf, ok: (layer, b, 0, 0, 0, 0)),
           pl.BlockSpec((None, 2 * G, HEAD_DIM), lambda b, j, pg, hf, ok: (b, 0, 0)),
           pl.BlockSpec((None, 8, HEAD_DIM), lambda b, j, pg, hf, ok: (b, 0, 0)),
           pl.BlockSpec(sm_s.shape, lambda b, j, pg, hf, ok: (0, 0))],
        out_specs=pl.BlockSpec((None, 8, HEAD_DIM), lambda b, j, pg, hf, ok: (b, 0, 0)),
        scratch_shapes=[pltpu.VMEM((G, n_sel, SB, HEAD_DIM), F32), pltpu.VMEM((G, n_sel, SB, HEAD_DIM), F32)],
    )
    return pl.pallas_call(
        functools.partial(_snsa_final_kernel, n_sel=n_sel),
        grid_spec=grid_spec,
        out_shape=jax.ShapeDtypeStruct((Bs, 8, HEAD_DIM), F32),
        compiler_params=_cparams(("parallel", "arbitrary")),
        name="nsa_sample_final",
    )(pages, halves, oks, *([cache_kv] * G), hm_s.reshape(hm_s.shape[0], NSLOT // 8, 8, HEAD_DIM), nsa_new, win_state,
      win_new, o_cmp.reshape(Bs, 8, HEAD_DIM), sm_s)


def _spool_kernel(st_ref, u_ref, w_ref, sc_ref, o_ref, *, Bs, pos0):
    u_all = u_ref[...]
    PG = GW // len(POOL_WINDOWS)
    NB = st_ref.shape[1]
    out = []
    for gi, w in enumerate(POOL_WINDOWS):
        c0 = gi * PG
        u = u_all[:, c0:c0 + PG]
        win = u
        for k in range(1, w):
            prev = st_ref[:, NB - k, c0:c0 + PG]
            if prev.shape[0] != u.shape[0]:
                prev = _pad_rows(prev, u.shape[0])
            win = win + prev
        d = win / float(min(w, pos0 + 1)) - u
        out.append(_dot(d.astype(BF16), w_ref[gi]) * sc_ref[:, c0:c0 + PG])
    o_ref[...] = jnp.concatenate(out, axis=1).astype(o_ref.dtype)


def _pool_sample(layer, state_pool, proj_s, pool_w, pool_scale, *, Bs, pos0):
    Ms = proj_s.shape[0]
    NB = state_pool.shape[2]
    assert NB >= max(POOL_WINDOWS) - 1 and pos0 >= NB
    cb = _SEG["pu"] // GW
    return pl.pallas_call(
        functools.partial(_spool_kernel, Bs=Bs, pos0=pos0),
        grid=(1,),
        in_specs=[pl.BlockSpec((None, Bs, NB, GW), lambda i: (layer, 0, 0, 0)),
                  pl.BlockSpec((Ms, GW), lambda i: (0, cb)),
                  pl.BlockSpec(pool_w.shape, lambda i: (0, 0, 0)),
                  pl.BlockSpec((1, GW), lambda i: (0, 0))],
        out_specs=pl.BlockSpec((Ms, GW), lambda i: (0, 0)),
        out_shape=jax.ShapeDtypeStruct((Ms, GW), BF16),
        compiler_params=_cparams(("arbitrary",)),
        name="pool_sample",
    )(state_pool, proj_s, pool_w, pool_scale)


def _layer_weights(l, norm1_g, w_in, fox_f_bias, fox_qk_g, moba_qk_g, nsa_qk_g, pool_w, pool_scale, cmp_pe, cmp_w1,
                   cmp_w2, w_out, norm2_g, w_up, w_down):
    D = w_in.shape[1]
    segs, c = {}, 0
    for name, n in _ORIG:
        segs[name] = w_in[l, :, c:c + n]
        c += n
    cols = [segs[n] for n in _NEW_ORDER]
    used = sum(x.shape[1] for x in cols)
    cols.append(jnp.zeros((D, PW - used), w_in.dtype))
    W = NSA_CMP_STRIDE * HEAD_DIM
    return dict(
        norm1_g=norm1_g[l], norm2_g=norm2_g[l],
        w_in=jnp.concatenate(cols, axis=1).astype(BF16),
        w_out=_cast_bf16(w_out, l), w_up=_cast_bf16(w_up, l), w_down=_cast_bf16(w_down, l),
        fox_f_bias=fox_f_bias[l], fox_qk_g=fox_qk_g[l], moba_qk_g=moba_qk_g[l], nsa_qk_g=nsa_qk_g[l],
        pool_w=pool_w[l].astype(BF16), pool_scale=pool_scale[l].reshape(1, GW),
        cmp_pe=cmp_pe[l].reshape(2, 2, 1, W), cmp_w1=cmp_w1[l].reshape(2, 2, W, HEAD_DIM).astype(BF16),
        cmp_w2=cmp_w2[l].astype(BF16), kc_gain=nsa_qk_g[l][1:2],
    )


def _mlp(x1, lw, *, tm):
    h2 = _rms(x1, lw["norm2_g"], tm=min(tm, 512))
    u = _matmul(h2, lw["w_up"], tm=tm, tn=1024, act="relu2", out_dtype=BF16, name="mm_up")
    return _matmul_down(x1, u, lw["w_down"], tm=tm, tn=1024, tk=4096)


def _prompt_layer(x, lw, *, B, T, layer, depth, stacks):
    tm = min(1024, B * T)
    h = _rms(x, lw["norm1_g"], tm=min(tm, 512))
    proj = _matmul(h, lw["w_in"], tm=tm, tn=1024, name="mm_in")
    fox_kv, moba_kv, nsa_kv, win, sm, hm, cmp_in, cc, crow, kmean = _prep(
        proj, lw["fox_qk_g"], lw["moba_qk_g"], lw["nsa_qk_g"], lw["fox_f_bias"], B=B, T=T, prompt=True,
        layer=layer, depth=depth, stacks=stacks)
    o_fox = _fox_prompt(hm, cc, crow, B=B, T=T)
    o_pool = _pool_prompt(proj, lw["pool_w"], lw["pool_scale"], B=B, T=T)
    o_moba = _moba_prompt(hm, jnp.swapaxes(kmean, 1, 2), B=B, T=T)
    cmp_in = cmp_in.reshape(B, 4, T // NSA_CMP_STRIDE, NSA_CMP_STRIDE * HEAD_DIM)
    cmp_kv = _compress(cmp_in, lw["cmp_pe"], lw["cmp_w1"], lw["cmp_w2"], lw["kc_gain"])
    o_nsa = _nsa_prompt(hm, cmp_kv, sm, B=B, T=T)
    x1 = _matmul_out(x, (o_fox, o_pool, o_moba, o_nsa), lw["w_out"], tm=tm, tn=1024)
    y = _mlp(x1, lw, tm=tm)
    wb = min(NSA_WINDOW, T)
    new = (sm[:, 0:8].reshape(B, T, 8), win.reshape(B, T, 2, NSA_KV_HEADS, HEAD_DIM)[:, T - wb:],
           proj.reshape(B, T, PW)[:, T - POOL_BUF:, _SEG["pu"]:_SEG["pu"] + GW])
    return y, new, (fox_kv, moba_kv, nsa_kv)


def _sample_layer(x, l, lw, caches, page_table, *, Bs):
    cache_fox_kv, cache_fox_lfT, cache_moba_kv, cache_nsa_kv, state_nsa_win, state_pool = caches
    Ms = x.shape[0]
    NP = page_table.shape[1]
    PS = cache_fox_kv.shape[2]
    past = NP * PS
    G = NSA_KV_HEADS
    h = _rms(x, lw["norm1_g"], tm=Ms)
    proj = _matmul(h, lw["w_in"], tm=Ms, tn=1024, name="mm_in_s")
    fox_kv, moba_kv, nsa_kv, win, sm, hm = _prep(
        proj, lw["fox_qk_g"], lw["moba_qk_g"], lw["nsa_qk_g"], lw["fox_f_bias"], B=1, T=Ms, prompt=False)
    fox_kv, moba_kv, nsa_kv = fox_kv[0], moba_kv[0], nsa_kv[0]
    hm_s = jnp.swapaxes(hm[0], 0, 1).astype(F32)
    fox_new = fox_kv.reshape(Ms, 16, HEAD_DIM)
    moba_new = moba_kv.reshape(Ms, 8, HEAD_DIM)
    nsa_new = nsa_kv.reshape(Ms, 8, HEAD_DIM)
    win_new = win.reshape(Ms, 4, HEAD_DIM)

    o_fox = _fox_sample(l, page_table, cache_fox_kv, cache_fox_lfT, hm_s, fox_new, sm, Bs=Bs)

    o_pool = _pool_sample(l, state_pool, proj, lw["pool_w"], lw["pool_scale"], Bs=Bs, pos0=past)

    top = _moba_sample_gate(l, page_table, cache_moba_kv, hm_s, Bs=Bs)
    ppb = MOBA_BLOCK // PS
    blk = top[:, :, 0:MOBA_TOPK]
    oks = top[:, :, MOBA_TOPK:2 * MOBA_TOPK]
    blk = jnp.where(oks > 0, blk, 0)
    pidx = (blk[..., None] * ppb + jnp.arange(ppb, dtype=jnp.int32)).reshape(Bs, 8 * MOBA_TOPK * ppb)
    pages = jnp.take_along_axis(page_table, pidx, axis=1).reshape(-1)
    o_moba = _moba_sample_attn(l, pages, oks.reshape(-1), cache_moba_kv, hm_s, moba_new, Bs=Bs)

    cmp_in = _nsa_sample_gather(l, page_table, cache_nsa_kv, Bs=Bs)
    cmp_kv = _compress(cmp_in, lw["cmp_pe"], lw["cmp_w1"], lw["cmp_w2"], lw["kc_gain"])
    o_cmp, sel = _nsa_sample_cmp(hm_s, cmp_kv, Bs=Bs)
    n_sel = NSA_SEL_COUNT - 1
    sblk = sel[:, :, 0, 0:n_sel]
    soks = sel[:, :, 0, n_sel:2 * n_sel]
    sblk = jnp.where(soks > 0, sblk, 0)
    spp = PS // NSA_SEL_BLOCK
    spages = jnp.take_along_axis(page_table, (sblk // spp).reshape(Bs, G * n_sel), axis=1).reshape(-1)
    o_nsa = _nsa_sample_final(l, spages, (sblk % spp).reshape(-1), soks.reshape(-1), cache_nsa_kv, hm_s, nsa_new,
                              state_nsa_win, win_new, o_cmp, sm, Bs=Bs)

    def rows(o):
        o = o.reshape(Bs, GW).astype(BF16)
        return jnp.concatenate([o, jnp.zeros((Ms - Bs, GW), BF16)], axis=0)

    x1 = _matmul_out(x, (rows(o_fox), o_pool, rows(o_moba), rows(o_nsa)), lw["w_out"], tm=Ms, tn=1024)
    y = _mlp(x1, lw, tm=Ms)
    new = (fox_kv[:Bs].reshape(Bs, 1, 2, 8, HEAD_DIM), sm[:Bs, 0:8].reshape(Bs, 1, 8),
           moba_kv[:Bs].reshape(Bs, 1, 2, 4, HEAD_DIM), nsa_kv[:Bs].reshape(Bs, 1, 4, G, HEAD_DIM),
           win[:Bs].reshape(Bs, 1, 2, G, HEAD_DIM), proj[:Bs, _SEG["pu"]:_SEG["pu"] + GW].reshape(Bs, 1, GW))
    return y, new


def kernel(x_prompt, x_sample, cache_fox_kv, cache_fox_logf, cache_moba_kv, cache_nsa_kv, state_nsa_win, state_pool,
           page_table, norm1_g, w_in, fox_f_bias, fox_qk_g, moba_qk_g, nsa_qk_g, pool_w, pool_scale, cmp_pe, cmp_w1,
           cmp_w2, w_out, norm2_g, w_up, w_down):
    B, T, D = x_prompt.shape
    Bs, Ts, _ = x_sample.shape
    assert Ts == 1 and D == N_MIXERS * GW
    depth = w_in.shape[0]
    n_pool, PS = cache_fox_kv.shape[1:3]
    WB = state_nsa_win.shape[2]
    Ms = 16
    assert Bs <= Ms

    caches = (cache_fox_kv, jnp.swapaxes(cache_fox_logf, 2, 3), cache_moba_kv, cache_nsa_kv, state_nsa_win, state_pool)

    xp = x_prompt.reshape(B * T, D)
    xs = jnp.concatenate([x_sample.reshape(Bs, D), jnp.zeros((Ms - Bs, D), x_sample.dtype)], axis=0)
    new_p, new_s, stacks = [], [], None
    for l in range(depth):
        lw = _layer_weights(l, norm1_g, w_in, fox_f_bias, fox_qk_g, moba_qk_g, nsa_qk_g, pool_w, pool_scale, cmp_pe,
                            cmp_w1, cmp_w2, w_out, norm2_g, w_up, w_down)
        xp, rows_p, stacks = _prompt_layer(xp, lw, B=B, T=T, layer=l, depth=depth, stacks=stacks)
        xs, rows_s = _sample_layer(xs, l, lw, caches, page_table, Bs=Bs)
        new_p.append(rows_p)
        new_s.append(rows_s)

    def stk(rows, i):
        return jnp.stack([r[i] for r in rows], axis=0)

    fox_kv_p, moba_kv_p, nsa_kv_p = stacks
    win_s = jnp.concatenate([state_nsa_win[:, :, 1:], stk(new_s, 4)], axis=2) if WB > 0 else stk(new_s, 4)[:, :, :0]
    pool_s = jnp.concatenate([state_pool[:, :, 1:], stk(new_s, 5)], axis=2)
    return (xp.reshape(B, T, D), xs[:Bs].reshape(Bs, 1, D),
            fox_kv_p.reshape(depth, B, T, 2, 8, HEAD_DIM), stk(new_s, 0), stk(new_p, 0), stk(new_s, 1),
            moba_kv_p.reshape(depth, B, T, 2, 4, HEAD_DIM), stk(new_s, 2),
            nsa_kv_p.reshape(depth, B, T, 4, NSA_KV_HEADS, HEAD_DIM), stk(new_s, 3),
            stk(new_p, 1), win_s, stk(new_p, 2), pool_s)
```

```python
import functools

import jax
import jax.numpy as jnp
from jax import lax
from jax.experimental import pallas as pl
from jax.experimental.pallas import tpu as pltpu

F32 = jnp.float32
BF16 = jnp.bfloat16

HEAD_DIM = 128
N_MIXERS = 4
FOX_FORGET_BIAS = 2.0
POOL_WINDOWS = (2, 4, 8, 16)
POOL_BUF = max(POOL_WINDOWS) - 1
MOBA_BLOCK = 256
MOBA_TOPK = 3
NSA_KV_HEADS = 2
NSA_CMP_BLOCK = 32
NSA_CMP_STRIDE = 16
NSA_SEL_BLOCK = 64
NSA_SEL_COUNT = 16
NSA_WINDOW = 512
RMS_EPS = 1e-6
SCALE = HEAD_DIM ** -0.5
LOG2E = 1.4426950408889634

LANES = 128
VMEM_LIMIT = 56 * 1024 * 1024
NEG = -1e30

GW = 1024
_SEG = dict(fq=0, fk=1024, fv=2048, pu=3072, mq=4096, mk=5120, mv=5632, nq=6144,
            nkc=7168, nvc=7424, nks=7680, nvs=7936, nkw=8192, nvw=8448, small=8704)
PW = 9216
_ORIG = (("fq", 1024), ("fk", 1024), ("fv", 1024), ("ff", 8), ("pu", 1024), ("mq", 1024), ("mk", 512),
         ("mv", 512), ("nq", 1024), ("nkc", 256), ("nvc", 256), ("nks", 256), ("nvs", 256), ("nkw", 256),
         ("nvw", 256), ("ng", 24))
_NEW_ORDER = ("fq", "fk", "fv", "pu", "mq", "mk", "mv", "nq", "nkc", "nvc", "nks", "nvs", "nkw", "nvw", "ff", "ng")
_SLOT = dict(fq=0, fk=8, fv=16, mq=24, mk=32, mv=36, nq=40, nks=48, nvs=50, nkw=52, nvw=54)
NSLOT = 56
GATE_LANE0 = 8


def _cparams(sem):
    return pltpu.CompilerParams(dimension_semantics=sem, vmem_limit_bytes=VMEM_LIMIT)


def _iota(shape, dim):
    return lax.broadcasted_iota(jnp.int32, shape, dim)


def _dot(a, b):
    return jnp.dot(a, b, preferred_element_type=F32)


def _dot_nt(a, b):
    return lax.dot_general(a, b, (((1,), (1,)), ((), ())), preferred_element_type=F32)


def _split3(x):
    hi = x.astype(BF16)
    r1 = x - hi.astype(F32)
    mid = r1.astype(BF16)
    lo = (r1 - mid.astype(F32)).astype(BF16)
    return hi, mid, lo


def _dot3(x, m):
    hi, mid, lo = _split3(x)
    return _dot(hi, m) + _dot(mid, m) + _dot(lo, m)


def _softmax_unnorm(s, mask, base2=False):
    sm = jnp.where(mask, s, NEG)
    m = jnp.max(sm, axis=-1, keepdims=True)
    e = jnp.where(mask, jnp.exp2(sm - m) if base2 else jnp.exp(sm - m), 0.0)
    l = jnp.maximum(jnp.sum(e, axis=-1, keepdims=True), 1e-30)
    return e, l


def _rms_kernel(x_ref, g_ref, o_ref):
    x = x_ref[...]
    y = x * lax.rsqrt(jnp.mean(x * x, axis=-1, keepdims=True) + RMS_EPS)
    o_ref[...] = (y * g_ref[...]).astype(o_ref.dtype)


def _rms(x, g, tm):
    M, D = x.shape
    return pl.pallas_call(
        _rms_kernel,
        grid=(M // tm,),
        in_specs=[pl.BlockSpec((tm, D), lambda i: (i, 0)), pl.BlockSpec((1, D), lambda i: (0, 0))],
        out_specs=pl.BlockSpec((tm, D), lambda i: (i, 0)),
        out_shape=jax.ShapeDtypeStruct((M, D), BF16),
        compiler_params=_cparams(("parallel",)),
        name="rms",
    )(x, g.reshape(1, D))


def _cast_kernel(x_ref, o_ref):
    o_ref[...] = x_ref[...].astype(o_ref.dtype)


def _cast_bf16(w_stack, l):
    _, K, N = w_stack.shape
    tk = max(8, min(K, (8 * 1024 * 1024) // (4 * N)))
    assert K % tk == 0
    return pl.pallas_call(
        _cast_kernel,
        grid=(K // tk,),
        in_specs=[pl.BlockSpec((None, tk, N), lambda i: (l, i, 0))],
        out_specs=pl.BlockSpec((tk, N), lambda i: (i, 0)),
        out_shape=jax.ShapeDtypeStruct((K, N), BF16),
        compiler_params=_cparams(("parallel",)),
        name="cast_bf16",
    )(w_stack)


def _mm_kernel(a_ref, w_ref, o_ref, *, act):
    acc = _dot(a_ref[...], w_ref[...])
    if act == "relu2":
        acc = jnp.maximum(acc, 0.0)
        acc = acc * acc
    o_ref[...] = acc.astype(o_ref.dtype)


def _matmul(a, w, *, tm, tn, act=None, out_dtype=F32, name="mm"):
    M, K = a.shape
    N = w.shape[1]
    return pl.pallas_call(
        functools.partial(_mm_kernel, act=act),
        grid=(M // tm, N // tn),
        in_specs=[pl.BlockSpec((tm, K), lambda i, j: (i, 0)), pl.BlockSpec((K, tn), lambda i, j: (0, j))],
        out_specs=pl.BlockSpec((tm, tn), lambda i, j: (i, j)),
        out_shape=jax.ShapeDtypeStruct((M, N), out_dtype),
        compiler_params=_cparams(("parallel", "parallel")),
        name=name,
    )(a, w)


def _mm_out_kernel(x_ref, a0, a1, a2, a3, w_ref, o_ref):
    acc = x_ref[...]
    for i, a in enumerate((a0, a1, a2, a3)):
        acc = acc + _dot(a[...], w_ref[i * GW:(i + 1) * GW, :])
    o_ref[...] = acc


def _matmul_out(x, parts, w, *, tm, tn):
    M, D = x.shape
    a_spec = pl.BlockSpec((tm, GW), lambda i, j: (i, 0))
    return pl.pallas_call(
        _mm_out_kernel,
        grid=(M // tm, D // tn),
        in_specs=[pl.BlockSpec((tm, tn), lambda i, j: (i, j)), a_spec, a_spec, a_spec, a_spec,
                  pl.BlockSpec((N_MIXERS * GW, tn), lambda i, j: (0, j))],
        out_specs=pl.BlockSpec((tm, tn), lambda i, j: (i, j)),
        out_shape=jax.ShapeDtypeStruct((M, D), F32),
        compiler_params=_cparams(("parallel", "parallel")),
        name="mm_out",
    )(x, *parts, w)


def _mm_down_kernel(x_ref, a_ref, w_ref, o_ref):
    @pl.when(pl.program_id(2) == 0)
    def _():
        o_ref[...] = x_ref[...]

    o_ref[...] += _dot(a_ref[...], w_ref[...])


def _matmul_down(x, a, w, *, tm, tn, tk):
    M, K = a.shape
    N = w.shape[1]
    return pl.pallas_call(
        _mm_down_kernel,
        grid=(M // tm, N // tn, K // tk),
        in_specs=[pl.BlockSpec((tm, tn), lambda i, j, k: (i, j)), pl.BlockSpec((tm, tk), lambda i, j, k: (i, k)),
                  pl.BlockSpec((tk, tn), lambda i, j, k: (k, j))],
        out_specs=pl.BlockSpec((tm, tn), lambda i, j, k: (i, j)),
        out_shape=jax.ShapeDtypeStruct((M, N), F32),
        compiler_params=_cparams(("parallel", "parallel", "arbitrary")),
        name="mm_down",
    )(x, a, w)


def _log_sigmoid(x):
    return jnp.minimum(x, 0.0) - jnp.log1p(jnp.exp(-jnp.abs(x)))


def _head_rms(x, g):
    return x * lax.rsqrt(jnp.mean(x * x, axis=-1, keepdims=True) + RMS_EPS) * g


def _prep_kernel(p_ref, gf_ref, gm_ref, gn_ref, fb_ref, *refs, tm, prompt, tiles_per_seq, n_alias):
    refs = refs[n_alias:]
    if prompt:
        (fox_ref, moba_ref, nsa_ref, win_ref, sm_ref, hm_ref, cmpin_ref, cc_ref, crow_ref, km_ref, carry_ref) = refs
    else:
        (fox_ref, moba_ref, nsa_ref, win_ref, sm_ref, hm_ref) = refs

    def seg(name, h):
        c0 = _SEG[name] + h * HEAD_DIM
        return p_ref[:, c0:c0 + HEAD_DIM]

    gfq, gfk = gf_ref[0:1, :], gf_ref[1:2, :]
    gmq, gmk = gm_ref[0:1, :], gm_ref[1:2, :]
    gnq, gnks, gnkw = gn_ref[0:1, :], gn_ref[2:3, :], gn_ref[3:4, :]
    qscale = SCALE * LOG2E if prompt else SCALE

    for h in range(8):
        hm_ref[_SLOT["fq"] + h] = (_head_rms(seg("fq", h), gfq) * qscale).astype(BF16)
        k = _head_rms(seg("fk", h), gfk)
        fox_ref[:, h * 128:(h + 1) * 128] = k
        hm_ref[_SLOT["fk"] + h] = k.astype(BF16)
        v = seg("fv", h)
        fox_ref[:, GW + h * 128:GW + (h + 1) * 128] = v
        hm_ref[_SLOT["fv"] + h] = v.astype(BF16)
        hm_ref[_SLOT["mq"] + h] = (_head_rms(seg("mq", h), gmq) * qscale).astype(BF16)
        hm_ref[_SLOT["nq"] + h] = (_head_rms(seg("nq", h), gnq) * qscale).astype(BF16)
    for h in range(4):
        k = _head_rms(seg("mk", h), gmk)
        moba_ref[:, h * 128:(h + 1) * 128] = k
        hm_ref[_SLOT["mk"] + h] = k.astype(BF16)
        if prompt:
            km_ref[h:h + 1, :] = jnp.mean(k, axis=0, keepdims=True)
        v = seg("mv", h)
        moba_ref[:, 512 + h * 128:512 + (h + 1) * 128] = v
        hm_ref[_SLOT["mv"] + h] = v.astype(BF16)
    for g in range(2):
        kc, vc = seg("nkc", g), seg("nvc", g)
        nsa_ref[:, g * 128:(g + 1) * 128] = kc
        nsa_ref[:, 256 + g * 128:256 + (g + 1) * 128] = vc
        if prompt:
            cmpin_ref[g] = kc
            cmpin_ref[2 + g] = vc
        ks = _head_rms(seg("nks", g), gnks)
        nsa_ref[:, 512 + g * 128:512 + (g + 1) * 128] = ks
        hm_ref[_SLOT["nks"] + g] = ks.astype(BF16)
        vs = seg("nvs", g)
        nsa_ref[:, 768 + g * 128:768 + (g + 1) * 128] = vs
        hm_ref[_SLOT["nvs"] + g] = vs.astype(BF16)
        kw = _head_rms(seg("nkw", g), gnkw)
        win_ref[:, g * 128:(g + 1) * 128] = kw
        hm_ref[_SLOT["nkw"] + g] = kw.astype(BF16)
        vw = seg("nvw", g)
        win_ref[:, 256 + g * 128:256 + (g + 1) * 128] = vw
        hm_ref[_SLOT["nvw"] + g] = vw.astype(BF16)

    sblk = p_ref[:, _SEG["small"]:_SEG["small"] + LANES]
    lane = _iota(sblk.shape, 1)
    logf = _log_sigmoid(sblk + fb_ref[...])
    gates = 1.0 / (1.0 + jnp.exp(-sblk))
    logf = jnp.where(lane < 8, logf, 0.0)
    sm_ref[...] = jnp.where(lane < 8, logf, jnp.where(lane < 32, gates, 0.0))

    if prompt:
        t = pl.program_id(0) % tiles_per_seq

        @pl.when(t == 0)
        def _():
            carry_ref[...] = jnp.zeros_like(carry_ref)

        tri = (_iota((tm, tm), 1) <= _iota((tm, tm), 0)).astype(BF16)
        c = _dot3_left(tri, logf) + carry_ref[...]
        carry_ref[...] = c[tm - 1:tm, :]
        c2 = c * LOG2E
        cc_ref[...] = c2
        crow_ref[...] = c2.T[0:8, :]


def _dot3_left(m, x):
    hi, mid, lo = _split3(x)
    return _dot(m, hi) + _dot(m, mid) + _dot(m, lo)


def _prep(proj, gf, gm, gn, fbias, *, B, T, prompt, layer=0, depth=1, stacks=None):
    M = B * T
    tm = 256 if prompt else M
    nt = M // tm
    row = lambda w: pl.BlockSpec((tm, w), lambda i: (i, 0))
    slab = lambda w: pl.BlockSpec((None, tm, w), lambda i: (layer, i, 0))
    full = lambda a: pl.BlockSpec(a.shape, lambda i: (0,) * a.ndim)
    fb = jnp.zeros((1, LANES), F32).at[0, :8].set(fbias)
    tiles_per_seq = T // tm if prompt else 1
    out_shape = [jax.ShapeDtypeStruct((depth, M, 2 * GW), F32), jax.ShapeDtypeStruct((depth, M, GW), F32),
                 jax.ShapeDtypeStruct((depth, M, GW), F32), jax.ShapeDtypeStruct((M, 512), F32),
                 jax.ShapeDtypeStruct((M, LANES), F32), jax.ShapeDtypeStruct((B, NSLOT, T, HEAD_DIM), BF16)]
    hm_map = (lambda i: (i // tiles_per_seq, 0, i % tiles_per_seq, 0))
    out_specs = [slab(2 * GW), slab(GW), slab(GW), row(512), row(LANES),
                 pl.BlockSpec((None, NSLOT, tm, HEAD_DIM), hm_map)]
    stacks = tuple(stacks or ())
    n_fixed = 5
    aliases = {n_fixed + k: k for k in range(len(stacks))}
    scratch = []
    if prompt:
        out_shape += [jax.ShapeDtypeStruct((B, 4, T, HEAD_DIM), F32), jax.ShapeDtypeStruct((M, LANES), F32),
                      jax.ShapeDtypeStruct((B, T // tm, 8, tm), F32),
                      jax.ShapeDtypeStruct((B, T // MOBA_BLOCK, 4, HEAD_DIM), F32)]
        out_specs += [pl.BlockSpec((None, 4, tm, HEAD_DIM), hm_map), row(LANES),
                      pl.BlockSpec((None, None, 8, tm), lambda i: (i // tiles_per_seq, i % tiles_per_seq, 0, 0)),
                      pl.BlockSpec((None, None, 4, HEAD_DIM), lambda i: (i // tiles_per_seq, i % tiles_per_seq, 0, 0))]
        scratch = [pltpu.VMEM((1, LANES), F32)]
    return pl.pallas_call(
        functools.partial(_prep_kernel, tm=tm, prompt=prompt, tiles_per_seq=tiles_per_seq, n_alias=len(stacks)),
        grid=(nt,),
        in_specs=[row(PW), full(gf), full(gm), full(gn), full(fb)] + [pl.BlockSpec(memory_space=pl.ANY)] * len(stacks),
        out_specs=out_specs,
        out_shape=out_shape,
        scratch_shapes=scratch,
        input_output_aliases=aliases,
        compiler_params=_cparams(("arbitrary",)),
        name="prep_prompt" if prompt else "prep_sample",
    )(proj, gf, gm, gn, fb, *stacks)


def _lane_pick(x, lane_idx):
    return jnp.sum(jnp.where(_iota(x.shape, 1) == lane_idx, x, 0.0), axis=1, keepdims=True)


def _attend_pieces(pieces):
    m = None
    for s, _ in pieces:
        mi = jnp.max(s, axis=1, keepdims=True)
        m = mi if m is None else jnp.maximum(m, mi)
    l, acc = None, None
    for s, v in pieces:
        e = jnp.exp2(s - m)
        li = jnp.sum(e, axis=1, keepdims=True)
        ai = _dot(e.astype(BF16), v)
        l = li if l is None else l + li
        acc = ai if acc is None else acc + ai
    return acc / jnp.maximum(l, 1e-30)


def _per_tile(qi, nq, body):
    for n in range(nq):
        pl.when(qi == n)(functools.partial(body, n))


def _causal_tile(tq):
    return _iota((1, tq), 1) <= _iota((tq, 1), 0)


HPS = 4


def _fox_kernel(q_ref, k_ref, v_ref, cc_ref, crow_ref, o_ref, *, tq):
    hp, qi = pl.program_id(1), pl.program_id(2)
    nq = k_ref.shape[1] // tq
    cc = cc_ref[...]

    def body(n):
        d0 = n * tq
        for i in range(HPS):
            h = hp * HPS + i
            q = q_ref[i]
            cq = _lane_pick(cc, h)
            sd = _dot_nt(q, k_ref[i, d0:d0 + tq, :]) + (cq - crow_ref[n, pl.ds(h, 1), :])
            pieces = [(jnp.where(_causal_tile(tq), sd, NEG), v_ref[i, d0:d0 + tq, :])]
            if n > 0:
                ck = jnp.concatenate([crow_ref[j, pl.ds(h, 1), :] for j in range(n)], axis=1)
                pieces.append((_dot_nt(q, k_ref[i, 0:d0, :]) + (cq - ck), v_ref[i, 0:d0, :]))
            o_ref[:, i * HEAD_DIM:(i + 1) * HEAD_DIM] = _attend_pieces(pieces).astype(o_ref.dtype)

    _per_tile(qi, nq, body)


def _fox_prompt(hm, cc, crow, *, B, T):
    tq = 256
    nq = T // tq
    slot = lambda s0: pl.BlockSpec((None, HPS, T, HEAD_DIM), lambda b, h, q: (b, s0 // HPS + h, 0, 0))
    return pl.pallas_call(
        functools.partial(_fox_kernel, tq=tq),
        grid=(B, 8 // HPS, nq),
        in_specs=[pl.BlockSpec((None, HPS, tq, HEAD_DIM), lambda b, h, q: (b, _SLOT["fq"] // HPS + h, q, 0)),
                  slot(_SLOT["fk"]), slot(_SLOT["fv"]),
                  pl.BlockSpec((tq, LANES), lambda b, h, q: (b * nq + q, 0)),
                  pl.BlockSpec((None, nq, 8, tq), lambda b, h, q: (b, 0, 0, 0))],
        out_specs=pl.BlockSpec((tq, HPS * HEAD_DIM), lambda b, h, q: (b * nq + q, h)),
        out_shape=jax.ShapeDtypeStruct((B * T, GW), BF16),
        compiler_params=_cparams(("parallel", "parallel", "arbitrary")),
        name="fox_prompt",
    )(hm, hm, hm, cc, crow)


def _rank_select(score, n_cand, own, n_keep):
    lane = _iota(score.shape, 1)
    rank = jnp.zeros(score.shape, jnp.int32)
    for j in range(n_cand):
        sj = score[:, j:j + 1]
        beats = (sj > score) | ((sj == score) & (j < lane))
        rank = rank + jnp.where(beats & (j < own), 1, 0)
    return ((lane < own) & (rank < n_keep)) | (lane == own)


def _moba_kernel(q_ref, k_ref, v_ref, km_ref, o_ref, *, tq):
    qi = pl.program_id(2)
    kps, T = k_ref.shape[0], k_ref.shape[1]
    n_blk = T // MOBA_BLOCK
    pos = qi * tq + _iota((tq, 1), 0)
    own = pos // MOBA_BLOCK
    rep = q_ref.shape[0] // kps
    sels = []
    for i in range(kps * rep):
        gate = _dot_nt(q_ref[i], km_ref[i // rep].astype(BF16))
        sels.append(jnp.where(_rank_select(gate, n_blk, own, MOBA_TOPK), 1.0, 0.0))

    def body(n):
        d0 = n * tq
        for i in range(kps * rep):
            q, sel, kv = q_ref[i], sels[i], i // rep
            sd = _dot_nt(q, k_ref[kv, d0:d0 + tq, :])
            pieces = [(jnp.where(_causal_tile(tq), sd, NEG), v_ref[kv, d0:d0 + tq, :])]
            if n > 0:
                keep = jnp.concatenate([jnp.broadcast_to(sel[:, j:j + 1], (tq, tq)) for j in range(n)], axis=1)
                pieces.append((jnp.where(keep > 0.5, _dot_nt(q, k_ref[kv, 0:d0, :]), NEG), v_ref[kv, 0:d0, :]))
            o_ref[:, i * HEAD_DIM:(i + 1) * HEAD_DIM] = _attend_pieces(pieces).astype(o_ref.dtype)

    _per_tile(qi, n_blk, body)


def _moba_prompt(hm, kmean, *, B, T):
    tq = MOBA_BLOCK
    nq = T // tq
    n_blk = T // MOBA_BLOCK
    n_kv = kmean.shape[1]
    rep = 8 // n_kv
    kps = 2
    slot = lambda s0: pl.BlockSpec((None, kps, T, HEAD_DIM), lambda b, h, q: (b, s0 // kps + h, 0, 0))
    return pl.pallas_call(
        functools.partial(_moba_kernel, tq=tq),
        grid=(B, n_kv // kps, nq),
        in_specs=[pl.BlockSpec((None, kps * rep, tq, HEAD_DIM), lambda b, h, q: (b, _SLOT["mq"] // (kps * rep) + h, q, 0)),
                  slot(_SLOT["mk"]), slot(_SLOT["mv"]),
                  pl.BlockSpec((None, kps, n_blk, HEAD_DIM), lambda b, h, q: (b, h, 0, 0))],
        out_specs=pl.BlockSpec((tq, kps * rep * HEAD_DIM), lambda b, h, q: (b * nq + q, h)),
        out_shape=jax.ShapeDtypeStruct((B * T, GW), BF16),
        compiler_params=_cparams(("parallel", "parallel", "arbitrary")),
        name="moba_prompt",
    )(hm, hm, hm, kmean)


def _sel_matrix(n_cmp_rows, n_cols, n_c, transposed=False):
    shape = (n_cols, n_cmp_rows) if transposed else (n_cmp_rows, n_cols)
    n = _iota(shape, 1 if transposed else 0)
    j = _iota(shape, 0 if transposed else 1)
    ratio = NSA_SEL_BLOCK // NSA_CMP_STRIDE
    lo = jnp.clip(ratio * j - NSA_CMP_BLOCK // NSA_CMP_STRIDE + 1, 0, n_c)
    hi = jnp.clip(ratio * (j + 1), 0, n_c)
    return ((n >= lo) & (n < hi)).astype(BF16)


def _rank_select_t(score, n_cand, own, n_keep):
    sub = _iota(score.shape, 0)
    rank = jnp.zeros(score.shape, jnp.int32)
    for j in range(n_cand):
        sj = score[j:j + 1, :]
        beats = (sj > score) | ((sj == score) & (j < sub))
        rank = rank + jnp.where(beats & (j < own), 1, 0)
    return ((sub < own) & (rank < n_keep)) | (sub == own)


def _nsa_kernel(q_ref, kc_ref, vc_ref, ks_ref, vs_ref, kw_ref, vw_ref, sm_ref, o_ref, *, tq):
    g, qi = pl.program_id(1), pl.program_id(2)
    T = ks_ref.shape[0]
    NC = kc_ref.shape[0]
    n_c = NC - 1
    n_sel = T // NSA_SEL_BLOCK
    R = q_ref.shape[0]
    pos = qi * tq + _iota((tq, 1), 0)

    kc = kc_ref[...].astype(BF16)
    vc = vc_ref[...].astype(BF16)
    ncol = _iota((1, NC), 1)
    mask_c = (ncol * NSA_CMP_STRIDE + (NSA_CMP_BLOCK - 1) <= pos) & (ncol < n_c)
    o_cmp = []
    psum = jnp.zeros((tq, NC), F32)
    for r in range(R):
        e, l = _softmax_unnorm(_dot_nt(q_ref[r], kc), mask_c, base2=True)
        p = e / l
        psum = psum + p
        o_cmp.append(_dot(p.astype(BF16), vc))
    hi, mid, lo = _split3(psum)
    msel_t = _sel_matrix(NC, LANES, n_c, transposed=True)
    imp_t = _dot_nt(msel_t, hi) + _dot_nt(msel_t, mid) + _dot_nt(msel_t, lo)
    own_t = (qi * tq + _iota((1, tq), 1)) // NSA_SEL_BLOCK
    sel_t = jnp.where(_rank_select_t(imp_t[0:n_sel, :], n_sel, own_t, NSA_SEL_COUNT - 1), 1.0, 0.0)
    sel_t = jnp.concatenate([sel_t, jnp.zeros((LANES - n_sel, tq), F32)], axis=0)
    sel = sel_t.T.astype(BF16)
    gates = sm_ref[...]
    gate_cols = [[_lane_pick(gates, GATE_LANE0 + 8 * k + g * R + r) for k in range(3)] for r in range(R)]
    n_win = (NSA_WINDOW + tq - 1) // tq

    def body(n):
        d0 = n * tq
        causal = _causal_tile(tq)
        blk_of_key = _iota((LANES, d0 + tq), 1) // NSA_SEL_BLOCK
        selk = _dot(sel, (blk_of_key == _iota((LANES, d0 + tq), 0)).astype(BF16))
        mask_d = (selk[:, d0:d0 + tq] > 0.5) & causal
        w_lo = max(n - n_win, 0)
        bias_d = jnp.where(mask_d, 0.0, NEG)
        bias_a = jnp.where(selk[:, 0:d0] > 0.5, 0.0, NEG) if n > 0 else None
        bias_c = jnp.where(causal, 0.0, NEG)
        bias_w = {}
        for j in range(w_lo, n):
            if (n - j + 1) * tq - 1 >= NSA_WINDOW:
                diff = (n - j) * tq + _iota((tq, 1), 0) - _iota((1, tq), 1)
                bias_w[j] = jnp.where(diff < NSA_WINDOW, 0.0, NEG)
        for r in range(R):
            q = q_ref[r]
            pieces = [(_dot_nt(q, ks_ref[d0:d0 + tq, :]) + bias_d, vs_ref[d0:d0 + tq, :])]
            if n > 0:
                pieces.append((_dot_nt(q, ks_ref[0:d0, :]) + bias_a, vs_ref[0:d0, :]))
            o_sel = _attend_pieces(pieces)
            pieces = [(_dot_nt(q, kw_ref[d0:d0 + tq, :]) + bias_c, vw_ref[d0:d0 + tq, :])]
            for j in range(w_lo, n):
                s = _dot_nt(q, kw_ref[j * tq:(j + 1) * tq, :])
                if j in bias_w:
                    s = s + bias_w[j]
                pieces.append((s, vw_ref[j * tq:(j + 1) * tq, :]))
            o_win = _attend_pieces(pieces)
            g0, g1, g2 = gate_cols[r]
            o = g0 * o_cmp[r] + g1 * o_sel + g2 * o_win
            o_ref[:, r * HEAD_DIM:(r + 1) * HEAD_DIM] = o.astype(o_ref.dtype)

    _per_tile(qi, T // tq, body)


def _nsa_prompt(hm, cmp_kv, sm, *, B, T):
    tq = 256
    nq = T // tq
    G = NSA_KV_HEADS
    R = 8 // G
    NC = cmp_kv.shape[2]
    slot = lambda s0: pl.BlockSpec((None, None, T, HEAD_DIM), lambda b, g, q: (b, s0 + g, 0, 0))
    return pl.pallas_call(
        functools.partial(_nsa_kernel, tq=tq),
        grid=(B, G, nq),
        in_specs=[pl.BlockSpec((None, R, tq, HEAD_DIM), lambda b, g, q: (b, _SLOT["nq"] // R + g, q, 0)),
                  pl.BlockSpec((None, None, NC, HEAD_DIM), lambda b, g, q: (b, g, 0, 0)),
                  pl.BlockSpec((None, None, NC, HEAD_DIM), lambda b, g, q: (b, G + g, 0, 0)),
                  slot(_SLOT["nks"]), slot(_SLOT["nvs"]), slot(_SLOT["nkw"]), slot(_SLOT["nvw"]),
                  pl.BlockSpec((tq, LANES), lambda b, g, q: (b * nq + q, 0))],
        out_specs=pl.BlockSpec((tq, R * HEAD_DIM), lambda b, g, q: (b * nq + q, g)),
        out_shape=jax.ShapeDtypeStruct((B * T, GW), BF16),
        compiler_params=_cparams(("parallel", "parallel", "arbitrary")),
        name="nsa_prompt",
    )(hm, cmp_kv, cmp_kv, hm, hm, hm, hm, sm)


def _compress_kernel(c_ref, pe_ref, w1_ref, w2_ref, g_ref, o_ref):
    c = pl.program_id(1)
    x = c_ref[...]
    NC = x.shape[0]
    a = _dot((x + pe_ref[0]).astype(BF16), w1_ref[0])
    bm = _dot((x + pe_ref[1]).astype(BF16), w1_ref[1])
    pre = a + pltpu.roll(bm, NC - 1, 0)
    hid = pre * (1.0 / (1.0 + jnp.exp(-pre)))
    out = _dot(hid.astype(BF16), w2_ref[...])
    out = jnp.where(c < NSA_KV_HEADS, _head_rms(out, g_ref[...]), out)
    o_ref[...] = jnp.where(_iota(out.shape, 0) < NC - 1, out, 0.0)


def _compress(x, pe, w1, w2, gain):
    B, C, NC, W = x.shape
    G = NSA_KV_HEADS
    return pl.pallas_call(
        _compress_kernel,
        grid=(B, C),
        in_specs=[pl.BlockSpec((None, None, NC, W), lambda b, c: (b, c, 0, 0)),
                  pl.BlockSpec((None, 2, 1, W), lambda b, c: (c // G, 0, 0, 0)),
                  pl.BlockSpec((None, 2, W, HEAD_DIM), lambda b, c: (c // G, 0, 0, 0)),
                  pl.BlockSpec((None, HEAD_DIM, HEAD_DIM), lambda b, c: (c // G, 0, 0)),
                  pl.BlockSpec((1, HEAD_DIM), lambda b, c: (0, 0))],
        out_specs=pl.BlockSpec((None, None, NC, HEAD_DIM), lambda b, c: (b, c, 0, 0)),
        out_shape=jax.ShapeDtypeStruct((B, C, NC, HEAD_DIM), F32),
        compiler_params=_cparams(("parallel", "parallel")),
        name="nsa_compress",
    )(x, pe, w1, w2, gain)


def _pool_kernel(u_ref, halo_ref, w_ref, sc_ref, o_ref, ext_ref, *, tp):
    t = pl.program_id(1)
    HALO = halo_ref.shape[0]
    halo = halo_ref[...]
    ext_ref[0:HALO, :] = jnp.where(t > 0, halo, 0.0)
    ext_ref[HALO:HALO + tp, :] = u_ref[...]
    pos = t * tp + _iota((tp, 1), 0)
    PG = GW // len(POOL_WINDOWS)
    for gi, w in enumerate(POOL_WINDOWS):
        c0 = gi * PG
        u = ext_ref[HALO:HALO + tp, c0:c0 + PG]
        win = u
        for k in range(1, w):
            win = win + ext_ref[HALO - k:HALO - k + tp, c0:c0 + PG]
        cnt = jnp.minimum(w, pos + 1).astype(F32)
        d = win / cnt - u
        o = _dot(d.astype(BF16), w_ref[gi]) * sc_ref[:, c0:c0 + PG]
        o_ref[:, c0:c0 + PG] = o.astype(o_ref.dtype)


def _pool_prompt(proj, pool_w, pool_scale, *, B, T):
    tp = 512
    HALO = 16
    nt = T // tp
    cb = _SEG["pu"] // GW
    return pl.pallas_call(
        functools.partial(_pool_kernel, tp=tp),
        grid=(B, nt),
        in_specs=[pl.BlockSpec((tp, GW), lambda b, t: (b * nt + t, cb)),
                  pl.BlockSpec((HALO, GW), lambda b, t: (jnp.maximum((b * nt + t) * (tp // HALO) - 1, 0), cb)),
                  pl.BlockSpec(pool_w.shape, lambda b, t: (0, 0, 0)),
                  pl.BlockSpec((1, GW), lambda b, t: (0, 0))],
        out_specs=pl.BlockSpec((tp, GW), lambda b, t: (b * nt + t, 0)),
        out_shape=jax.ShapeDtypeStruct((B * T, GW), BF16),
        scratch_shapes=[pltpu.VMEM((HALO + tp, GW), F32)],
        compiler_params=_cparams(("parallel", "arbitrary")),
        name="pool_prompt",
    )(proj, proj, pool_w, pool_scale)


def _diag_col(row, lane0, n):
    b = jnp.broadcast_to(row, (n, row.shape[1]))
    keep = _iota(b.shape, 1) == _iota(b.shape, 0) + lane0
    return jnp.sum(jnp.where(keep, b, 0.0), axis=1, keepdims=True)


def _pad_rows(x, n):
    return jnp.concatenate([x, jnp.zeros((n - x.shape[0],) + x.shape[1:], x.dtype)], axis=0)


def _sfox_kernel(pt_ref, *refs, pps):
    kv_refs, lf_refs = refs[:pps], refs[pps:2 * pps]
    later_ref, q_ref, new_ref, sm_ref, o_ref, m_ref, l_ref, acc_ref, carry_ref = refs[2 * pps:]
    b, i = pl.program_id(0), pl.program_id(1)
    H = 8
    P = kv_refs[0].shape[0]
    q = q_ref[...]

    @pl.when(i == 0)
    def _():
        knew, vnew = new_ref[0:H, :], new_ref[H:2 * H, :]
        s_new = jnp.sum(q * knew, axis=1, keepdims=True)
        m_ref[...] = jnp.broadcast_to(s_new, m_ref.shape)
        l_ref[...] = jnp.ones_like(l_ref)
        acc_ref[...] = vnew
        carry_ref[...] = jnp.broadcast_to(_diag_col(sm_ref[pl.ds(b, 1), :], 0, H), carry_ref.shape)

    ones = jnp.ones((HEAD_DIM, HEAD_DIM), BF16)
    carry = carry_ref[...]
    scores = []
    for kv_ref, lf_ref in zip(kv_refs, lf_refs):
        lf = lf_ref[...]
        z = (kv_ref[:, 0] * q[None] + later_ref[...] * lf[None]).reshape(P * H, HEAD_DIM)
        hi = z.astype(BF16)
        mid = (z - hi.astype(F32)).astype(BF16)
        scores.append((_dot(hi, ones) + _dot(mid, ones)).reshape(P, H, HEAD_DIM) + carry[None])
        carry = carry + jnp.sum(lf, axis=1, keepdims=True)
    m_old = m_ref[...]
    m_new = m_old
    for s in scores:
        m_new = jnp.maximum(m_new, jnp.max(s, axis=0))
    alpha = jnp.exp(m_old - m_new)
    l = alpha * l_ref[...]
    acc = alpha * acc_ref[...]
    for s, kv_ref in zip(scores, kv_refs):
        p = jnp.exp(s - m_new[None])
        l = l + jnp.sum(p, axis=0)
        acc = acc + jnp.sum(p * kv_ref[:, 1], axis=0)
    l_ref[...] = l
    acc_ref[...] = acc
    m_ref[...] = m_new
    carry_ref[...] = carry

    @pl.when(i == pl.num_programs(1) - 1)
    def _():
        o_ref[...] = acc_ref[...] / jnp.maximum(l_ref[...], 1e-30)


def _fox_sample(layer, page_table, cache_kv, cache_lfT, hm_s, fox_new, sm_s, *, Bs):
    NP = page_table.shape[1]
    PS = cache_kv.shape[2]
    H = 8
    r = jnp.arange(PS, dtype=jnp.int32)
    later = jnp.broadcast_to((r[None, :] > r[:, None]).astype(F32)[:, None, :], (PS, H, PS))
    rep = pltpu.VMEM((H, HEAD_DIM), F32)
    pps = 8 if NP % 8 == 0 else 1
    page = lambda k: (lambda b, i, pt: pt[b, NP - 1 - (pps * i + k)])
    kv_spec = lambda k: pl.BlockSpec((None, None, PS, 2, H, HEAD_DIM),
                                     lambda b, i, pt: (layer, page(k)(b, i, pt), 0, 0, 0, 0))
    lf_spec = lambda k: pl.BlockSpec((None, None, H, PS), lambda b, i, pt: (layer, page(k)(b, i, pt), 0, 0))
    grid_spec = pltpu.PrefetchScalarGridSpec(
        num_scalar_prefetch=1,
        grid=(Bs, NP // pps),
        in_specs=[kv_spec(k) for k in range(pps)] + [lf_spec(k) for k in range(pps)]
        + [pl.BlockSpec((PS, H, PS), lambda b, i, pt: (0, 0, 0)),
           pl.BlockSpec((None, None, H, HEAD_DIM), lambda b, i, pt: (b, _SLOT["fq"] // H, 0, 0)),
           pl.BlockSpec((None, 2 * H, HEAD_DIM), lambda b, i, pt: (b, 0, 0)),
           pl.BlockSpec(sm_s.shape, lambda b, i, pt: (0, 0))],
        out_specs=pl.BlockSpec((None, H, HEAD_DIM), lambda b, i, pt: (b, 0, 0)),
        scratch_shapes=[rep, rep, rep, rep],
    )
    assert PS == HEAD_DIM
    return pl.pallas_call(
        functools.partial(_sfox_kernel, pps=pps),
        grid_spec=grid_spec,
        out_shape=jax.ShapeDtypeStruct((Bs, H, HEAD_DIM), F32),
        compiler_params=_cparams(("parallel", "arbitrary")),
        name="fox_sample",
    )(page_table, *([cache_kv] * pps), *([cache_lfT] * pps), later,
      hm_s.reshape(hm_s.shape[0], NSLOT // H, H, HEAD_DIM), fox_new, sm_s)


def _argmax_rounds(score, n_rounds, out_lanes):
    rows, L = score.shape
    lane = _iota(score.shape, 1)
    olane = _iota((rows, out_lanes), 1)
    out = jnp.zeros((rows, out_lanes), jnp.int32)
    for t in range(n_rounds):
        m = jnp.max(score, axis=1, keepdims=True)
        idx = jnp.min(jnp.where(score == m, lane, L), axis=1, keepdims=True)
        ok = jnp.where(m > NEG, 1, 0)
        out = jnp.where(olane == t, idx, out)
        out = jnp.where(olane == n_rounds + t, ok, out)
        score = jnp.where(lane == idx, NEG, score)
    return out


def _smoba_gate_kernel(pt_ref, *refs, ppb, bps):
    k_refs = refs[:bps * ppb]
    q_ref, o_ref, g_ref = refs[bps * ppb:]
    i = pl.program_id(1)
    H = 8

    @pl.when(i == 0)
    def _():
        g_ref[...] = jnp.full_like(g_ref, NEG)

    q = q_ref[...]
    g = g_ref[...]
    for jb in range(bps):
        ksum = jnp.sum(k_refs[jb * ppb][...], axis=0)
        for t in range(1, ppb):
            ksum = ksum + jnp.sum(k_refs[jb * ppb + t][...], axis=0)
        kmean = ksum / float(MOBA_BLOCK)
        kme = jnp.concatenate([kmean[h // 2:h // 2 + 1, :] for h in range(H)], axis=0)
        gate = jnp.sum(q * kme, axis=1, keepdims=True)
        g = jnp.where(_iota(g.shape, 1) == i * bps + jb, gate, g)
    g_ref[...] = g

    @pl.when(i == pl.num_programs(1) - 1)
    def _():
        o_ref[...] = _argmax_rounds(g, MOBA_TOPK, LANES)


def _moba_sample_gate(layer, page_table, cache_kv, hm_s, *, Bs):
    NP = page_table.shape[1]
    PS = cache_kv.shape[2]
    nkv = cache_kv.shape[4]
    ppb = MOBA_BLOCK // PS
    NB = NP // ppb
    bps = 8 if NB % 8 == 0 else (4 if NB % 4 == 0 else 1)
    assert NP % ppb == 0
    page = lambda j: pl.BlockSpec((None, None, PS, None, nkv, HEAD_DIM),
                                  lambda b, i, pt: (layer, pt[b, bps * ppb * i + j], 0, 0, 0, 0))
    grid_spec = pltpu.PrefetchScalarGridSpec(
        num_scalar_prefetch=1,
        grid=(Bs, NB // bps),
        in_specs=[page(j) for j in range(bps * ppb)]
        + [pl.BlockSpec((None, None, 8, HEAD_DIM), lambda b, i, pt: (b, _SLOT["mq"] // 8, 0, 0))],
        out_specs=pl.BlockSpec((None, 8, LANES), lambda b, i, pt: (b, 0, 0)),
        scratch_shapes=[pltpu.VMEM((8, max(LANES, NB)), F32)],
    )
    return pl.pallas_call(
        functools.partial(_smoba_gate_kernel, ppb=ppb, bps=bps),
        grid_spec=grid_spec,
        out_shape=jax.ShapeDtypeStruct((Bs, 8, LANES), jnp.int32),
        compiler_params=_cparams(("parallel", "arbitrary")),
        name="moba_sample_gate",
    )(page_table, *([cache_kv] * (bps * ppb)), hm_s.reshape(hm_s.shape[0], NSLOT // 8, 8, HEAD_DIM))


def _smoba_attn_kernel(pg_ref, ok_ref, *refs, n_pages, ppb):
    H = 8
    kv_refs = refs[:H]
    q_ref, new_ref, o_ref, ks_ref, vs_ref = refs[H:]
    b, j = pl.program_id(0), pl.program_id(1)
    for h in range(H):
        ks_ref[h, j] = kv_refs[h][:, 0, h // 2, :]
        vs_ref[h, j] = kv_refs[h][:, 1, h // 2, :]

    @pl.when(j == n_pages - 1)
    def _():
        for h in range(H):
            kvh = h // 2
            q = q_ref[h:h + 1, :]
            knew = new_ref[kvh:kvh + 1, :]
            vnew = new_ref[4 + kvh:5 + kvh, :]
            s_new = jnp.sum(q * knew, axis=1, keepdims=True)
            s, ok = [], []
            for t in range(n_pages):
                s.append(jnp.sum(ks_ref[h, t] * q, axis=1, keepdims=True))
                ok.append(ok_ref[(b * H + h) * (n_pages // ppb) + t // ppb] > 0)
            m = s_new
            for t in range(n_pages):
                m = jnp.maximum(m, jnp.max(jnp.where(ok[t], s[t], NEG), axis=0, keepdims=True))
            l = jnp.exp(s_new - m)
            acc = l * vnew
            for t in range(n_pages):
                e = jnp.where(ok[t], jnp.exp(s[t] - m), 0.0)
                l = l + jnp.sum(e, axis=0, keepdims=True)
                acc = acc + jnp.sum(e * vs_ref[h, t], axis=0, keepdims=True)
            o_ref[h:h + 1, :] = acc / jnp.maximum(l, 1e-30)


def _moba_sample_attn(layer, pages, oks, cache_kv, hm_s, moba_new, *, Bs):
    PS = cache_kv.shape[2]
    nkv = cache_kv.shape[4]
    ppb = MOBA_BLOCK // PS
    n_pages = MOBA_TOPK * ppb
    H = 8
    page = lambda h: pl.BlockSpec((None, None, PS, 2, nkv, HEAD_DIM),
                                  lambda b, j, pg, ok: (layer, pg[(b * H + h) * n_pages + j], 0, 0, 0, 0))
    grid_spec = pltpu.PrefetchScalarGridSpec(
        num_scalar_prefetch=2,
        grid=(Bs, n_pages),
        in_specs=[page(h) for h in range(H)]
        + [pl.BlockSpec((None, None, 8, HEAD_DIM), lambda b, j, pg, ok: (b, _SLOT["mq"] // 8, 0, 0)),
           pl.BlockSpec((None, 8, HEAD_DIM), lambda b, j, pg, ok: (b, 0, 0))],
        out_specs=pl.BlockSpec((None, 8, HEAD_DIM), lambda b, j, pg, ok: (b, 0, 0)),
        scratch_shapes=[pltpu.VMEM((H, n_pages, PS, HEAD_DIM), F32), pltpu.VMEM((H, n_pages, PS, HEAD_DIM), F32)],
    )
    return pl.pallas_call(
        functools.partial(_smoba_attn_kernel, n_pages=n_pages, ppb=ppb),
        grid_spec=grid_spec,
        out_shape=jax.ShapeDtypeStruct((Bs, 8, HEAD_DIM), F32),
        compiler_params=_cparams(("parallel", "arbitrary")),
        name="moba_sample_attn",
    )(pages, oks, *([cache_kv] * H), hm_s.reshape(hm_s.shape[0], NSLOT // 8, 8, HEAD_DIM), moba_new)


def _snsa_gather_kernel(pt_ref, *refs, pps):
    x_refs, o_ref = refs[:pps], refs[pps]
    PS = x_refs[0].shape[0]
    G = x_refs[0].shape[2]
    S = NSA_CMP_STRIDE
    cpp = PS // S
    for k in range(pps):
        for j in range(2):
            for g in range(G):
                for r in range(S):
                    o_ref[j * G + g, k * cpp:(k + 1) * cpp, r * HEAD_DIM:(r + 1) * HEAD_DIM] = (
                        x_refs[k][pl.ds(r, cpp, stride=S), j, g, :])


def _nsa_sample_gather(layer, page_table, cache_kv, *, Bs):
    NP = page_table.shape[1]
    PS = cache_kv.shape[2]
    G = cache_kv.shape[4]
    C = 2 * G
    S = NSA_CMP_STRIDE
    pps = next(p for p in (16, 8, 4, 1) if NP % p == 0)
    cpp = PS // S
    page = lambda k: pl.BlockSpec((None, None, PS, 2, G, HEAD_DIM),
                                  lambda b, i, pt: (layer, pt[b, pps * i + k], 0, 0, 0, 0))
    grid_spec = pltpu.PrefetchScalarGridSpec(
        num_scalar_prefetch=1,
        grid=(Bs, NP // pps),
        in_specs=[page(k) for k in range(pps)],
        out_specs=pl.BlockSpec((None, C, pps * cpp, S * HEAD_DIM), lambda b, i, pt: (b, 0, i, 0)),
    )
    return pl.pallas_call(
        functools.partial(_snsa_gather_kernel, pps=pps),
        grid_spec=grid_spec,
        out_shape=jax.ShapeDtypeStruct((Bs, C, NP * cpp, S * HEAD_DIM), F32),
        compiler_params=_cparams(("parallel", "arbitrary")),
        name="nsa_sample_gather",
    )(page_table, *([cache_kv] * pps))


def _snsa_cmp_kernel(q_ref, kc_ref, vc_ref, o_ref, sel_ref, *, n_sel_past):
    R = q_ref.shape[0]
    NC = kc_ref.shape[0]
    n_c = NC - 1
    q = _pad_rows(q_ref[...], 16).astype(BF16)
    s = _dot_nt(q, kc_ref[...].astype(BF16))[0:R, :]
    mask = _iota((1, NC), 1) < n_c
    e, l = _softmax_unnorm(s, mask)
    p = e / l
    o_ref[...] = _dot(_pad_rows(p, 16).astype(BF16), vc_ref[...].astype(BF16))[0:R, :]
    psum = jnp.sum(p, axis=0, keepdims=True)
    imp = _dot3(_pad_rows(psum, 16), _sel_matrix(NC, n_sel_past, n_c))[0:1, :]
    sel_ref[...] = _argmax_rounds(imp, NSA_SEL_COUNT - 1, LANES)


def _nsa_sample_cmp(hm_s, cmp_kv, *, Bs):
    G = NSA_KV_HEADS
    R = 8 // G
    NC = cmp_kv.shape[2]
    n_sel_past = NC * NSA_CMP_STRIDE // NSA_SEL_BLOCK
    return pl.pallas_call(
        functools.partial(_snsa_cmp_kernel, n_sel_past=n_sel_past),
        grid=(Bs, G),
        in_specs=[pl.BlockSpec((None, None, R, HEAD_DIM), lambda b, g: (b, _SLOT["nq"] // R + g, 0, 0)),
                  pl.BlockSpec((None, None, NC, HEAD_DIM), lambda b, g: (b, g, 0, 0)),
                  pl.BlockSpec((None, None, NC, HEAD_DIM), lambda b, g: (b, G + g, 0, 0))],
        out_specs=[pl.BlockSpec((None, None, R, HEAD_DIM), lambda b, g: (b, g, 0, 0)),
                   pl.BlockSpec((None, None, 1, LANES), lambda b, g: (b, g, 0, 0))],
        out_shape=[jax.ShapeDtypeStruct((Bs, G, R, HEAD_DIM), F32), jax.ShapeDtypeStruct((Bs, G, 1, LANES), jnp.int32)],
        compiler_params=_cparams(("parallel", "parallel")),
        name="nsa_sample_cmp",
    )(hm_s.reshape(hm_s.shape[0], NSLOT // R, R, HEAD_DIM), cmp_kv, cmp_kv)


def _snsa_final_kernel(pg_ref, hf_ref, ok_ref, *refs, n_sel):
    G = NSA_KV_HEADS
    blk_refs = refs[:G]
    q_ref, new_ref, win_ref, wnew_ref, ocmp_ref, sm_ref, o_ref, ks_ref, vs_ref = refs[G:]
    b, j = pl.program_id(0), pl.program_id(1)
    for g in range(G):
        ks_ref[g, j] = blk_refs[g][:, 0, g, :]
        vs_ref[g, j] = blk_refs[g][:, 1, g, :]

    @pl.when(j == n_sel - 1)
    def _():
        R = q_ref.shape[0] // G
        gates = sm_ref[pl.ds(b, 1), :]
        for g in range(G):
            qf = q_ref[g * R:(g + 1) * R, :]
            q = _pad_rows(qf, 16).astype(BF16)
            ks_new = new_ref[2 * G + g:2 * G + g + 1, :]
            vs_new = new_ref[3 * G + g:3 * G + g + 1, :]
            s_new = jnp.sum(qf * ks_new, axis=1, keepdims=True)
            s, ok = [], []
            for t in range(n_sel):
                s.append(_dot_nt(q, ks_ref[g, t].astype(BF16))[0:R, :])
                ok.append(ok_ref[(b * G + g) * n_sel + t] > 0)
            m = s_new
            for t in range(n_sel):
                m = jnp.maximum(m, jnp.max(jnp.where(ok[t], s[t], NEG), axis=1, keepdims=True))
            l = jnp.exp(s_new - m)
            acc = l * vs_new
            for t in range(n_sel):
                e = jnp.where(ok[t], jnp.exp(s[t] - m), 0.0)
                l = l + jnp.sum(e, axis=1, keepdims=True)
                acc = acc + _dot(_pad_rows(e, 16).astype(BF16), vs_ref[g, t].astype(BF16))[0:R, :]
            o_sel = acc / jnp.maximum(l, 1e-30)
            WB = win_ref.shape[0]
            kw_new = wnew_ref[g:g + 1, :]
            vw_new = wnew_ref[G + g:G + g + 1, :]
            sw = _dot_nt(q, win_ref[:, 0, g, :].astype(BF16))[0:R, :]
            mask_w = (WB - _iota((1, WB), 1)) < NSA_WINDOW
            sw_new = jnp.sum(qf * kw_new, axis=1, keepdims=True)
            mw = jnp.maximum(sw_new, jnp.max(jnp.where(mask_w, sw, NEG), axis=1, keepdims=True))
            ew = jnp.where(mask_w, jnp.exp(sw - mw), 0.0)
            ew_new = jnp.exp(sw_new - mw)
            lw = ew_new + jnp.sum(ew, axis=1, keepdims=True)
            pv = _dot(_pad_rows(ew, 16).astype(BF16), win_ref[:, 1, g, :].astype(BF16))[0:R, :]
            o_win = (ew_new * vw_new + pv) / jnp.maximum(lw, 1e-30)
            g0 = _diag_col(gates, GATE_LANE0 + g * R, R)
            g1 = _diag_col(gates, GATE_LANE0 + 8 + g * R, R)
            g2 = _diag_col(gates, GATE_LANE0 + 16 + g * R, R)
            o_ref[g * R:(g + 1) * R, :] = g0 * ocmp_ref[g * R:(g + 1) * R, :] + g1 * o_sel + g2 * o_win


def _nsa_sample_final(layer, pages, halves, oks, cache_kv, hm_s, nsa_new, win_state, win_new, o_cmp, sm_s, *, Bs):
    G = NSA_KV_HEADS
    n_sel = NSA_SEL_COUNT - 1
    SB = NSA_SEL_BLOCK
    WB = win_state.shape[2]
    idx = lambda b, g, j: (b * G + g) * n_sel + j
    blk = lambda g: pl.BlockSpec((None, None, SB, 2, G, HEAD_DIM),
                                 lambda b, j, pg, hf, ok: (layer, pg[idx(b, g, j)], hf[idx(b, g, j)], 1, 0, 0))
    grid_spec = pltpu.PrefetchScalarGridSpec(
        num_scalar_prefetch=3,
        grid=(Bs, n_sel),
        in_specs=[blk(g) for g in range(G)]
        + [pl.BlockSpec((None, None, 8, HEAD_DIM), lambda b, j, pg, hf, ok: (b, _SLOT["nq"] // 8, 0, 0)),
           pl.BlockSpec((None, 8, HEAD_DIM), lambda b, j, pg, hf, ok: (b, 0, 0)),
           pl.BlockSpec((None, None, WB, 2, G, HEAD_DIM), lambda b, j, pg, hf, ok: (layer, b, 0, 0, 0, 0)),
           pl.BlockSpec((None, 2 * G, HEAD_DIM), lambda b, j, pg, hf, ok: (b, 0, 0)),
           pl.BlockSpec((None, 8, HEAD_DIM), lambda b, j, pg, hf, ok: (b, 0, 0)),
           pl.BlockSpec(sm_s.shape, lambda b, j, pg, hf, ok: (0, 0))],
        out_specs=pl.BlockSpec((None, 8, HEAD_DIM), lambda b, j, pg, hf, ok: (b, 0, 0)),
        scratch_shapes=[pltpu.VMEM((G, n_sel, SB, HEAD_DIM), F32), pltpu.VMEM((G, n_sel, SB, HEAD_DIM), F32)],
    )
    return pl.pallas_call(
        functools.partial(_snsa_final_kernel, n_sel=n_sel),
        grid_spec=grid_spec,
        out_shape=jax.ShapeDtypeStruct((Bs, 8, HEAD_DIM), F32),
        compiler_params=_cparams(("parallel", "arbitrary")),
        name="nsa_sample_final",
    )(pages, halves, oks, *([cache_kv] * G), hm_s.reshape(hm_s.shape[0], NSLOT // 8, 8, HEAD_DIM), nsa_new, win_state,
      win_new, o_cmp.reshape(Bs, 8, HEAD_DIM), sm_s)


def _spool_kernel(st_ref, u_ref, w_ref, sc_ref, o_ref, *, Bs, pos0):
    u_all = u_ref[...]
    PG = GW // len(POOL_WINDOWS)
    NB = st_ref.shape[1]
    out = []
    for gi, w in enumerate(POOL_WINDOWS):
        c0 = gi * PG
        u = u_all[:, c0:c0 + PG]
        win = u
        for k in range(1, w):
            prev = st_ref[:, NB - k, c0:c0 + PG]
            if prev.shape[0] != u.shape[0]:
                prev = _pad_rows(prev, u.shape[0])
            win = win + prev
        d = win / float(min(w, pos0 + 1)) - u
        out.append(_dot(d.astype(BF16), w_ref[gi]) * sc_ref[:, c0:c0 + PG])
    o_ref[...] = jnp.concatenate(out, axis=1).astype(o_ref.dtype)


def _pool_sample(layer, state_pool, proj_s, pool_w, pool_scale, *, Bs, pos0):
    Ms = proj_s.shape[0]
    NB = state_pool.shape[2]
    assert NB >= max(POOL_WINDOWS) - 1 and pos0 >= NB
    cb = _SEG["pu"] // GW
    return pl.pallas_call(
        functools.partial(_spool_kernel, Bs=Bs, pos0=pos0),
        grid=(1,),
        in_specs=[pl.BlockSpec((None, Bs, NB, GW), lambda i: (layer, 0, 0, 0)),
                  pl.BlockSpec((Ms, GW), lambda i: (0, cb)),
                  pl.BlockSpec(pool_w.shape, lambda i: (0, 0, 0)),
                  pl.BlockSpec((1, GW), lambda i: (0, 0))],
        out_specs=pl.BlockSpec((Ms, GW), lambda i: (0, 0)),
        out_shape=jax.ShapeDtypeStruct((Ms, GW), BF16),
        compiler_params=_cparams(("arbitrary",)),
        name="pool_sample",
    )(state_pool, proj_s, pool_w, pool_scale)


def _layer_weights(l, norm1_g, w_in, fox_f_bias, fox_qk_g, moba_qk_g, nsa_qk_g, pool_w, pool_scale, cmp_pe, cmp_w1,
                   cmp_w2, w_out, norm2_g, w_up, w_down):
    D = w_in.shape[1]
    segs, c = {}, 0
    for name, n in _ORIG:
        segs[name] = w_in[l, :, c:c + n]
        c += n
    cols = [segs[n] for n in _NEW_ORDER]
    used = sum(x.shape[1] for x in cols)
    cols.append(jnp.zeros((D, PW - used), w_in.dtype))
    W = NSA_CMP_STRIDE * HEAD_DIM
    return dict(
        norm1_g=norm1_g[l], norm2_g=norm2_g[l],
        w_in=jnp.concatenate(cols, axis=1).astype(BF16),
        w_out=_cast_bf16(w_out, l), w_up=_cast_bf16(w_up, l), w_down=_cast_bf16(w_down, l),
        fox_f_bias=fox_f_bias[l], fox_qk_g=fox_qk_g[l], moba_qk_g=moba_qk_g[l], nsa_qk_g=nsa_qk_g[l],
        pool_w=pool_w[l].astype(BF16), pool_scale=pool_scale[l].reshape(1, GW),
        cmp_pe=cmp_pe[l].reshape(2, 2, 1, W), cmp_w1=cmp_w1[l].reshape(2, 2, W, HEAD_DIM).astype(BF16),
        cmp_w2=cmp_w2[l].astype(BF16), kc_gain=nsa_qk_g[l][1:2],
    )


def _mlp(x1, lw, *, tm):
    h2 = _rms(x1, lw["norm2_g"], tm=min(tm, 512))
    u = _matmul(h2, lw["w_up"], tm=tm, tn=1024, act="relu2", out_dtype=BF16, name="mm_up")
    return _matmul_down(x1, u, lw["w_down"], tm=tm, tn=1024, tk=4096)


def _prompt_layer(x, lw, *, B, T, layer, depth, stacks):
    tm = min(1024, B * T)
    h = _rms(x, lw["norm1_g"], tm=min(tm, 512))
    proj = _matmul(h, lw["w_in"], tm=tm, tn=1024, name="mm_in")
    fox_kv, moba_kv, nsa_kv, win, sm, hm, cmp_in, cc, crow, kmean = _prep(
        proj, lw["fox_qk_g"], lw["moba_qk_g"], lw["nsa_qk_g"], lw["fox_f_bias"], B=B, T=T, prompt=True,
        layer=layer, depth=depth, stacks=stacks)
    o_fox = _fox_prompt(hm, cc, crow, B=B, T=T)
    o_pool = _pool_prompt(proj, lw["pool_w"], lw["pool_scale"], B=B, T=T)
    o_moba = _moba_prompt(hm, jnp.swapaxes(kmean, 1, 2), B=B, T=T)
    cmp_in = cmp_in.reshape(B, 4, T // NSA_CMP_STRIDE, NSA_CMP_STRIDE * HEAD_DIM)
    cmp_kv = _compress(cmp_in, lw["cmp_pe"], lw["cmp_w1"], lw["cmp_w2"], lw["kc_gain"])
    o_nsa = _nsa_prompt(hm, cmp_kv, sm, B=B, T=T)
    x1 = _matmul_out(x, (o_fox, o_pool, o_moba, o_nsa), lw["w_out"], tm=tm, tn=1024)
    y = _mlp(x1, lw, tm=tm)
    wb = min(NSA_WINDOW, T)
    new = (sm[:, 0:8].reshape(B, T, 8), win.reshape(B, T, 2, NSA_KV_HEADS, HEAD_DIM)[:, T - wb:],
           proj.reshape(B, T, PW)[:, T - POOL_BUF:, _SEG["pu"]:_SEG["pu"] + GW])
    return y, new, (fox_kv, moba_kv, nsa_kv)


def _sample_layer(x, l, lw, caches, page_table, *, Bs):
    cache_fox_kv, cache_fox_lfT, cache_moba_kv, cache_nsa_kv, state_nsa_win, state_pool = caches
    Ms = x.shape[0]
    NP = page_table.shape[1]
    PS = cache_fox_kv.shape[2]
    past = NP * PS
    G = NSA_KV_HEADS
    h = _rms(x, lw["norm1_g"], tm=Ms)
    proj = _matmul(h, lw["w_in"], tm=Ms, tn=1024, name="mm_in_s")
    fox_kv, moba_kv, nsa_kv, win, sm, hm = _prep(
        proj, lw["fox_qk_g"], lw["moba_qk_g"], lw["nsa_qk_g"], lw["fox_f_bias"], B=1, T=Ms, prompt=False)
    fox_kv, moba_kv, nsa_kv = fox_kv[0], moba_kv[0], nsa_kv[0]
    hm_s = jnp.swapaxes(hm[0], 0, 1).astype(F32)
    fox_new = fox_kv.reshape(Ms, 16, HEAD_DIM)
    moba_new = moba_kv.reshape(Ms, 8, HEAD_DIM)
    nsa_new = nsa_kv.reshape(Ms, 8, HEAD_DIM)
    win_new = win.reshape(Ms, 4, HEAD_DIM)

    o_fox = _fox_sample(l, page_table, cache_fox_kv, cache_fox_lfT, hm_s, fox_new, sm, Bs=Bs)

    o_pool = _pool_sample(l, state_pool, proj, lw["pool_w"], lw["pool_scale"], Bs=Bs, pos0=past)

    top = _moba_sample_gate(l, page_table, cache_moba_kv, hm_s, Bs=Bs)
    ppb = MOBA_BLOCK // PS
    blk = top[:, :, 0:MOBA_TOPK]
    oks = top[:, :, MOBA_TOPK:2 * MOBA_TOPK]
    blk = jnp.where(oks > 0, blk, 0)
    pidx = (blk[..., None] * ppb + jnp.arange(ppb, dtype=jnp.int32)).reshape(Bs, 8 * MOBA_TOPK * ppb)
    pages = jnp.take_along_axis(page_table, pidx, axis=1).reshape(-1)
    o_moba = _moba_sample_attn(l, pages, oks.reshape(-1), cache_moba_kv, hm_s, moba_new, Bs=Bs)

    cmp_in = _nsa_sample_gather(l, page_table, cache_nsa_kv, Bs=Bs)
    cmp_kv = _compress(cmp_in, lw["cmp_pe"], lw["cmp_w1"], lw["cmp_w2"], lw["kc_gain"])
    o_cmp, sel = _nsa_sample_cmp(hm_s, cmp_kv, Bs=Bs)
    n_sel = NSA_SEL_COUNT - 1
    sblk = sel[:, :, 0, 0:n_sel]
    soks = sel[:, :, 0, n_sel:2 * n_sel]
    sblk = jnp.where(soks > 0, sblk, 0)
    spp = PS // NSA_SEL_BLOCK
    spages = jnp.take_along_axis(page_table, (sblk // spp).reshape(Bs, G * n_sel), axis=1).reshape(-1)
    o_nsa = _nsa_sample_final(l, spages, (sblk % spp).reshape(-1), soks.reshape(-1), cache_nsa_kv, hm_s, nsa_new,
                              state_nsa_win, win_new, o_cmp, sm, Bs=Bs)

    def rows(o):
        o = o.reshape(Bs, GW).astype(BF16)
        return jnp.concatenate([o, jnp.zeros((Ms - Bs, GW), BF16)], axis=0)

    x1 = _matmul_out(x, (rows(o_fox), o_pool, rows(o_moba), rows(o_nsa)), lw["w_out"], tm=Ms, tn=1024)
    y = _mlp(x1, lw, tm=Ms)
    new = (fox_kv[:Bs].reshape(Bs, 1, 2, 8, HEAD_DIM), sm[:Bs, 0:8].reshape(Bs, 1, 8),
           moba_kv[:Bs].reshape(Bs, 1, 2, 4, HEAD_DIM), nsa_kv[:Bs].reshape(Bs, 1, 4, G, HEAD_DIM),
           win[:Bs].reshape(Bs, 1, 2, G, HEAD_DIM), proj[:Bs, _SEG["pu"]:_SEG["pu"] + GW].reshape(Bs, 1, GW))
    return y, new


def kernel(x_prompt, x_sample, cache_fox_kv, cache_fox_logf, cache_moba_kv, cache_nsa_kv, state_nsa_win, state_pool,
           page_table, norm1_g, w_in, fox_f_bias, fox_qk_g, moba_qk_g, nsa_qk_g, pool_w, pool_scale, cmp_pe, cmp_w1,
           cmp_w2, w_out, norm2_g, w_up, w_down):
    B, T, D = x_prompt.shape
    Bs, Ts, _ = x_sample.shape
    assert Ts == 1 and D == N_MIXERS * GW
    depth = w_in.shape[0]
    n_pool, PS = cache_fox_kv.shape[1:3]
    WB = state_nsa_win.shape[2]
    Ms = 16
    assert Bs <= Ms

    caches = (cache_fox_kv, jnp.swapaxes(cache_fox_logf, 2, 3), cache_moba_kv, cache_nsa_kv, state_nsa_win, state_pool)

    xp = x_prompt.reshape(B * T, D)
    xs = jnp.concatenate([x_sample.reshape(Bs, D), jnp.zeros((Ms - Bs, D), x_sample.dtype)], axis=0)
    new_p, new_s, stacks = [], [], None
    for l in range(depth):
        lw = _layer_weights(l, norm1_g, w_in, fox_f_bias, fox_qk_g, moba_qk_g, nsa_qk_g, pool_w, pool_scale, cmp_pe,
                            cmp_w1, cmp_w2, w_out, norm2_g, w_up, w_down)
        xp, rows_p, stacks = _prompt_layer(xp, lw, B=B, T=T, layer=l, depth=depth, stacks=stacks)
        xs, rows_s = _sample_layer(xs, l, lw, caches, page_table, Bs=Bs)
        new_p.append(rows_p)
        new_s.append(rows_s)

    def stk(rows, i):
        return jnp.stack([r[i] for r in rows], axis=0)

    fox_kv_p, moba_kv_p, nsa_kv_p = stacks
    win_s = jnp.concatenate([state_nsa_win[:, :, 1:], stk(new_s, 4)], axis=2) if WB > 0 else stk(new_s, 4)[:, :, :0]
    pool_s = jnp.concatenate([state_pool[:, :, 1:], stk(new_s, 5)], axis=2)
    return (xp.reshape(B, T, D), xs[:Bs].reshape(Bs, 1, D),
            fox_kv_p.reshape(depth, B, T, 2, 8, HEAD_DIM), stk(new_s, 0), stk(new_p, 0), stk(new_s, 1),
            moba_kv_p.reshape(depth, B, T, 2, 4, HEAD_DIM), stk(new_s, 2),
            nsa_kv_p.reshape(depth, B, T, 4, NSA_KV_HEADS, HEAD_DIM), stk(new_s, 3),
            stk(new_p, 1), win_s, stk(new_p, 2), pool_s)
```
